```python
import math
import jax, jax.numpy as jnp
from jax import lax
import numpy as np

D_MODEL = 2048
BATCH = 16
SEQ = 256
DEPTH = 2
DEC_BATCH = 8
DEC_SEQ = 4096
PAST_LEN = 256

GRID_W = 64
ROPE_BASE = 10000.0
EPS = 1e-6
Q_BLOCK = 128
N_MOD = 6

GROUP_W = D_MODEL // 4
DIFF_H = 4
DIFF_DH = GROUP_W // (2 * DIFF_H)
MLA_H = 4
MLA_NOPE = 64
MLA_ROPE = 32
MLA_V = GROUP_W // MLA_H
MLA_Q_LORA = 384
MLA_KV_LORA = 128
NA_H = 8
NA_DH = GROUP_W // NA_H
NA_ROWS = 8
NA_COLS = 16
NA_CB = GRID_W // NA_COLS
NA_BAND = 2 * NA_COLS
POOL_WINDOWS = (2, 4, 8, 16)
POOL_G = 4
POOL_C = GROUP_W // POOL_G
D_FF = 5632
CONV_W = 3

IN_SIZES = (DIFF_H * 2 * DIFF_DH, DIFF_H * 2 * DIFF_DH, DIFF_H * 2 * DIFF_DH,
            MLA_Q_LORA, MLA_KV_LORA, MLA_ROPE,
            GROUP_W, GROUP_W, GROUP_W,
            GROUP_W)
IN_W = sum(IN_SIZES)

kernel_name = 'hybrid_diffusion_prefix_step'

F32 = jnp.float32


def _rms(x, g):
    xf = x.astype(F32)
    y = xf * lax.rsqrt(jnp.mean(xf * xf, axis=-1, keepdims=True) + EPS)
    return (y * g.astype(F32)).astype(x.dtype)


def _modulate(x, g, shift, scale):
    return _rms(x, g) * (1 + scale[:, None]) + shift[:, None]


def _axial_rope(n, dim):
    t = jnp.arange(n)
    rows = (t // GRID_W).astype(F32)
    cols = (t % GRID_W).astype(F32)
    quarter = dim // 4
    inv = ROPE_BASE ** (-jnp.arange(quarter, dtype=F32) / quarter)
    ang = jnp.concatenate([rows[:, None] * inv, cols[:, None] * inv], axis=-1)
    return jnp.cos(ang), jnp.sin(ang)


def _rope(x, cos, sin):
    shape = (cos.shape[0],) + (1,) * (x.ndim - 3) + (cos.shape[1],)
    c = cos.reshape(shape)
    s = sin.reshape(shape)
    x1, x2 = jnp.split(x.astype(F32), 2, axis=-1)
    return jnp.concatenate([x1 * c - x2 * s, x1 * s + x2 * c], axis=-1).astype(x.dtype)


def _q_blocks(q):
    b, n = q.shape[:2]
    return jnp.moveaxis(q.reshape((b, n // Q_BLOCK, Q_BLOCK) + q.shape[2:]), 1, 0)


def _unblock(o):
    o = jnp.moveaxis(o, 0, 1)
    return o.reshape((o.shape[0], o.shape[1] * o.shape[2]) + o.shape[3:])


def _dense_attend(q, k, v):
    scale = q.shape[-1] ** -0.5

    def blk(qb):
        s = jnp.einsum('bqhd,bkhd->bhqk', qb, k).astype(F32) * scale
        p = jax.nn.softmax(s, axis=-1)
        return jnp.einsum('bhqk,bkhe->bqhe', p.astype(v.dtype), v)

    return _unblock(lax.map(blk, _q_blocks(q)))


def _diff_attend(q, k, v, lam):
    scale = q.shape[-1] ** -0.5

    def blk(qb):
        s = jnp.einsum('bqhjd,bkhjd->bhjqk', qb, k).astype(F32) * scale
        p = jax.nn.softmax(s, axis=-1)
        a = p[:, :, 0] - lam * p[:, :, 1]
        return jnp.einsum('bhqk,bkhe->bqhe', a.astype(v.dtype), v)

    return _unblock(lax.map(blk, _q_blocks(q)))


def _diff_qkv(zq, zk, zv, qn_g, kn_g):
    b, n = zq.shape[:2]
    q = _rms(zq.reshape(b, n, DIFF_H, 2, DIFF_DH), qn_g)
    k = _rms(zk.reshape(b, n, DIFF_H, 2, DIFF_DH), kn_g)
    v = zv.reshape(b, n, DIFF_H, 2 * DIFF_DH)
    return q, k, v


def _diff_lambda(lam_p, layer):
    lam_init = 0.8 - 0.6 * math.exp(-0.3 * layer)
    lf = lam_p.astype(F32)
    lam = jnp.exp(jnp.sum(lf[0] * lf[1])) - jnp.exp(jnp.sum(lf[2] * lf[3])) + lam_init
    return lam, lam_init


def _diff_out(o, sub_g, lam_init):
    o = _rms(o, sub_g) * (1 - lam_init)
    return o.reshape(o.shape[0], o.shape[1], DIFF_H * 2 * DIFF_DH)


def _mla_q(zcq, qa_g, w_uq, qn_g):
    b, n = zcq.shape[:2]
    q = (_rms(zcq, qa_g) @ w_uq).reshape(b, n, MLA_H, MLA_NOPE + MLA_ROPE)
    return _rms(q, qn_g)


def _mla_kv(ckv, kpe, w_ukv, kn_g):
    b, m = ckv.shape[:2]
    kv = (ckv @ w_ukv).reshape(b, m, MLA_H, MLA_NOPE + MLA_V)
    k_nope, v = kv[..., :MLA_NOPE], kv[..., MLA_NOPE:]
    k = jnp.concatenate([k_nope, jnp.broadcast_to(kpe[:, :, None, :], (b, m, MLA_H, MLA_ROPE))], axis=-1)
    return _rms(k, kn_g), v


def _rope_tail(x, cos, sin):
    return jnp.concatenate([x[..., :MLA_NOPE], _rope(x[..., MLA_NOPE:], cos, sin)], axis=-1)


def _na_qkv(zq, zk, zv, qn_g, kn_g):
    b, n = zq.shape[:2]
    sh = (b, n, NA_H, NA_DH)
    return _rms(zq.reshape(sh), qn_g), _rms(zk.reshape(sh), kn_g), zv.reshape(sh)


def _na_latent(q, k, v, k_ctx, v_ctx, bias_tab):
    b, n = q.shape[:2]
    rows = n // GRID_W
    kr = min(NA_ROWS, rows)
    scale = NA_DH ** -0.5
    grid = (b, rows, GRID_W, NA_H, NA_DH)
    qg, kg, vg = q.reshape(grid), k.reshape(grid), v.reshape(grid)
    row_start = jnp.clip(jnp.arange(rows) - kr // 2, 0, rows - kr)
    qcol = jnp.arange(NA_CB)[:, None] * NA_COLS + jnp.arange(NA_COLS)[None, :]
    band0 = jnp.clip(jnp.arange(NA_CB) * NA_COLS - NA_COLS // 2, 0, GRID_W - NA_BAND)
    kcol = band0[:, None] + jnp.arange(NA_BAND)[None, :]
    cstart = jnp.clip(qcol - NA_COLS // 2, 0, GRID_W - NA_COLS)
    in_win = (kcol[:, None, :] >= cstart[:, :, None]) & (kcol[:, None, :] < cstart[:, :, None] + NA_COLS)
    dc_idx = jnp.clip(kcol[:, None, :] - qcol[:, :, None], 1 - NA_COLS, NA_COLS - 1) + NA_COLS - 1
    n_loc = kr * NA_BAND

    def row_fn(args):
        r, q_row = args
        rs = row_start[r]
        k_rows = lax.dynamic_slice_in_dim(kg, rs, kr, axis=1)
        v_rows = lax.dynamic_slice_in_dim(vg, rs, kr, axis=1)
        k_band = k_rows[:, :, kcol]
        v_band = v_rows[:, :, kcol]
        qb = q_row.reshape(b, NA_CB, NA_COLS, NA_H, NA_DH)
        dr_idx = rs + jnp.arange(kr) - r + NA_ROWS - 1
        bias = bias_tab[:, dr_idx[None, None, :, None], dc_idx[:, :, None, :]].astype(F32)
        s_loc = jnp.einsum('bjchd,bijkhd->bhjcik', qb, k_band).astype(F32) * scale + bias
        s_loc = jnp.where(in_win[:, :, None, :], s_loc, -jnp.inf)
        s_ctx = jnp.einsum('bjchd,bmhd->bhjcm', qb, k_ctx).astype(F32) * scale
        s = jnp.concatenate([s_loc.reshape(b, NA_H, NA_CB, NA_COLS, n_loc), s_ctx], axis=-1)
        p = jax.nn.softmax(s, axis=-1).astype(v.dtype)
        p_loc = p[..., :n_loc].reshape(b, NA_H, NA_CB, NA_COLS, kr, NA_BAND)
        o = (jnp.einsum('bhjcik,bijkhd->bjchd', p_loc, v_band)
             + jnp.einsum('bhjcm,bmhd->bjchd', p[..., n_loc:], v_ctx))
        return o.reshape(b, GRID_W, NA_H, NA_DH)

    o = lax.map(row_fn, (jnp.arange(rows), jnp.moveaxis(qg, 1, 0)))
    return jnp.moveaxis(o, 0, 1).reshape(b, n, NA_H * NA_DH)


def _pool_mixer(u, w_pool, scale):
    b, n = u.shape[:2]
    uf = u.astype(F32).reshape(b, n, POOL_G, POOL_C)
    cs = jnp.concatenate([jnp.zeros((b, 1, POOL_G, POOL_C), F32), jnp.cumsum(uf, axis=1)], axis=1)
    win = jnp.array(POOL_WINDOWS, dtype=jnp.int32)[None, :]
    t = jnp.arange(n, dtype=jnp.int32)[:, None]
    lo = jnp.clip(t - win // 2, 0, n)
    hi = jnp.clip(t - win // 2 + win, 0, n)
    grp = jnp.arange(POOL_G)[None, :]
    mean = (cs[:, hi, grp] - cs[:, lo, grp]) / (hi - lo).astype(F32)[None, :, :, None]
    d = (mean - uf).astype(u.dtype)
    y = jnp.einsum('bngc,gce->bnge', d, w_pool).reshape(b, n, GROUP_W)
    return y * scale


def _conv_ffn(h, w_up, conv_w, conv_b, w_down):
    u = h @ w_up
    up = jnp.pad(u, ((0, 0), (1, 1), (0, 0)))
    u = up[:, :-2] * conv_w[0] + up[:, 1:-1] * conv_w[1] + up[:, 2:] * conv_w[2] + conv_b
    gate, val = jnp.split(u, 2, axis=-1)
    return (jax.nn.silu(gate) * val) @ w_down


def _layer(x, cvec, P, l, cache):
    b, n = x.shape[:2]
    sh1, sc1, g1, sh2, sc2, g2 = jnp.split(jax.nn.silu(cvec) @ P['ada_w'][l] + P['ada_b'][l], N_MOD, axis=-1)
    h = _modulate(x, P['norm1_g'][l], sh1, sc1)
    z = h @ P['w_in'][l]
    splits = [int(s) for s in np.cumsum(IN_SIZES)[:-1]]
    dq, dk, dv, mcq, mckv, mkpe, nq, nk, nv, pu = jnp.split(z, splits, axis=-1)
    qa, ka, va = _diff_qkv(dq, dk, dv, P['diff_qn_g'][l], P['diff_kn_g'][l])
    lam, lam_init = _diff_lambda(P['diff_lam'][l], l)
    ckv = _rms(mckv, P['mla_kva_g'][l])
    qb = _mla_q(mcq, P['mla_qa_g'][l], P['mla_w_uq'][l], P['mla_qn_g'][l])
    kb, vb = _mla_kv(ckv, mkpe, P['mla_w_ukv'][l], P['mla_kn_g'][l])
    qc, kc, vc = _na_qkv(nq, nk, nv, P['na_qn_g'][l], P['na_kn_g'][l])
    if cache is None:
        oa = _diff_attend(qa, ka, va, lam)
        ob = _dense_attend(qb, kb, vb)
        oc = _dense_attend(qc, kc, vc).reshape(b, n, GROUP_W)
        state = (ka, va, ckv, mkpe, kc, vc)
    else:
        a_k, a_v, b_ckv, b_kpe, c_k, c_v = cache
        cos_a, sin_a = _axial_rope(n, DIFF_DH)
        cos_b, sin_b = _axial_rope(n, MLA_ROPE)
        oa = _diff_attend(_rope(qa, cos_a, sin_a),
                          jnp.concatenate([a_k, _rope(ka, cos_a, sin_a)], axis=1),
                          jnp.concatenate([a_v, va], axis=1), lam)
        kb_ctx, vb_ctx = _mla_kv(b_ckv, b_kpe, P['mla_w_ukv'][l], P['mla_kn_g'][l])
        ob = _dense_attend(_rope_tail(qb, cos_b, sin_b),
                           jnp.concatenate([kb_ctx, _rope_tail(kb, cos_b, sin_b)], axis=1),
                           jnp.concatenate([vb_ctx, vb], axis=1))
        oc = _na_latent(qc, kc, vc, c_k, c_v, P['na_bias'][l])
        state = ()
    oa = _diff_out(oa, P['diff_sub_g'][l], lam_init)
    ob = ob.reshape(b, n, GROUP_W)
    od = _pool_mixer(pu, P['pool_w'][l], P['pool_scale'][l])
    mix = jnp.concatenate([oa, ob, oc, od], axis=-1) @ P['w_out'][l]
    x = x + g1[:, None] * mix
    h = _modulate(x, P['norm2_g'][l], sh2, sc2)
    x = x + g2[:, None] * _conv_ffn(h, P['w_up'][l], P['conv_w'][l], P['conv_b'][l], P['w_down'][l])
    return x, state


def setup_inputs(seed: int = 0) -> dict:
    key = jax.random.key(seed)
    ks = iter(jax.random.split(key, 40))

    def nrm(shape, s):
        return jax.random.normal(next(ks), shape, F32) * s

    def gain(shape):
        return 1.0 + nrm(shape, 0.05)

    D = D_MODEL
    return {
        'x_prompt': nrm((BATCH, SEQ, D), 1.0),
        'x_sample': nrm((DEC_BATCH, DEC_SEQ, D), 1.0),
        'cache_diff_k': nrm((DEC_BATCH, DEPTH, PAST_LEN, DIFF_H, 2, DIFF_DH), 1.0),
        'cache_diff_v': nrm((DEC_BATCH, DEPTH, PAST_LEN, DIFF_H, 2 * DIFF_DH), 1.0),
        'cache_mla_ckv': nrm((DEC_BATCH, DEPTH, PAST_LEN, MLA_KV_LORA), 1.0),
        'cache_mla_kpe': nrm((DEC_BATCH, DEPTH, PAST_LEN, MLA_ROPE), 1.0),
        'cache_na_k': nrm((DEC_BATCH, DEPTH, PAST_LEN, NA_H, NA_DH), 1.0),
        'cache_na_v': nrm((DEC_BATCH, DEPTH, PAST_LEN, NA_H, NA_DH), 1.0),
        'c': nrm((DEC_BATCH, D), 1.0),
        'c_ctx': nrm((D,), 1.0),
        'norm1_g': gain((DEPTH, D)),
        'norm2_g': gain((DEPTH, D)),
        'ada_w': nrm((DEPTH, D, N_MOD * D), 0.5 * D ** -0.5),
        'ada_b': nrm((DEPTH, N_MOD * D), 0.02),
        'w_in': nrm((DEPTH, D, IN_W), D ** -0.5),
        'diff_qn_g': gain((DEPTH, DIFF_DH)),
        'diff_kn_g': gain((DEPTH, DIFF_DH)),
        'diff_lam': nrm((DEPTH, 4, DIFF_DH), 0.1),
        'diff_sub_g': gain((DEPTH, 2 * DIFF_DH)),
        'mla_qa_g': gain((DEPTH, MLA_Q_LORA)),
        'mla_kva_g': gain((DEPTH, MLA_KV_LORA)),
        'mla_w_uq': nrm((DEPTH, MLA_Q_LORA, MLA_H * (MLA_NOPE + MLA_ROPE)), MLA_Q_LORA ** -0.5),
        'mla_w_ukv': nrm((DEPTH, MLA_KV_LORA, MLA_H * (MLA_NOPE + MLA_V)), MLA_KV_LORA ** -0.5),
        'mla_qn_g': gain((DEPTH, MLA_NOPE + MLA_ROPE)),
        'mla_kn_g': gain((DEPTH, MLA_NOPE + MLA_ROPE)),
        'na_qn_g': gain((DEPTH, NA_DH)),
        'na_kn_g': gain((DEPTH, NA_DH)),
        'na_bias': nrm((DEPTH, NA_H, 2 * NA_ROWS - 1, 2 * NA_COLS - 1), 0.1),
        'pool_w': nrm((DEPTH, POOL_G, POOL_C, POOL_C), POOL_C ** -0.5),
        'pool_scale': gain((DEPTH, GROUP_W)),
        'w_out': nrm((DEPTH, D, D), D ** -0.5),
        'w_up': nrm((DEPTH, D, 2 * D_FF), D ** -0.5),
        'conv_w': nrm((DEPTH, CONV_W, 2 * D_FF), CONV_W ** -0.5),
        'conv_b': nrm((DEPTH, 2 * D_FF), 0.02),
        'w_down': nrm((DEPTH, D_FF, D), D_FF ** -0.5),
    }


def reference(x_prompt, x_sample, cache_diff_k, cache_diff_v, cache_mla_ckv, cache_mla_kpe, cache_na_k,
              cache_na_v, c, c_ctx, norm1_g, norm2_g, ada_w, ada_b, w_in, diff_qn_g, diff_kn_g, diff_lam,
              diff_sub_g, mla_qa_g, mla_kva_g, mla_w_uq, mla_w_ukv, mla_qn_g, mla_kn_g, na_qn_g, na_kn_g,
              na_bias, pool_w, pool_scale, w_out, w_up, conv_w, conv_b, w_down):
    P = dict(norm1_g=norm1_g, norm2_g=norm2_g, ada_w=ada_w, ada_b=ada_b, w_in=w_in,
             diff_qn_g=diff_qn_g, diff_kn_g=diff_kn_g, diff_lam=diff_lam, diff_sub_g=diff_sub_g,
             mla_qa_g=mla_qa_g, mla_kva_g=mla_kva_g, mla_w_uq=mla_w_uq, mla_w_ukv=mla_w_ukv,
             mla_qn_g=mla_qn_g, mla_kn_g=mla_kn_g, na_qn_g=na_qn_g, na_kn_g=na_kn_g, na_bias=na_bias,
             pool_w=pool_w, pool_scale=pool_scale, w_out=w_out, w_up=w_up, conv_w=conv_w,
             conv_b=conv_b, w_down=w_down)
    xp = x_prompt
    states = []
    for l in range(DEPTH):
        xp, st = _layer(xp, c_ctx[None, :], P, l, None)
        states.append(st)
    xs = x_sample
    for l in range(DEPTH):
        cache_l = (cache_diff_k[:, l], cache_diff_v[:, l], cache_mla_ckv[:, l], cache_mla_kpe[:, l],
                   cache_na_k[:, l], cache_na_v[:, l])
        xs, _ = _layer(xs, c, P, l, cache_l)
    new_diff_k = jnp.stack([s[0] for s in states], axis=1)
    new_diff_v = jnp.stack([s[1] for s in states], axis=1)
    new_mla_ckv = jnp.stack([s[2] for s in states], axis=1)
    new_mla_kpe = jnp.stack([s[3] for s in states], axis=1)
    new_na_k = jnp.stack([s[4] for s in states], axis=1)
    new_na_v = jnp.stack([s[5] for s in states], axis=1)
    return (xp, xs, new_diff_k, new_diff_v, new_mla_ckv, new_mla_kpe, new_na_k, new_na_v)
```

```python
import functools
import math

import numpy as np
import jax
import jax.numpy as jnp
from jax import lax
from jax.experimental import pallas as pl
from jax.experimental.pallas import tpu as pltpu

F32 = jnp.float32
BF16 = jnp.bfloat16

GRID_W = 64
ROPE_BASE = 10000.0
EPS = 1e-6
N_MOD = 6
DIFF_H = 4
DIFF_DH = 64
MLA_H = 4
MLA_NOPE = 64
MLA_ROPE = 32
MLA_V = 128
MLA_Q_LORA = 384
MLA_KV_LORA = 128
NA_H = 8
NA_DH = 64
NA_ROWS = 8
NA_COLS = 16
POOL_WINDOWS = (2, 4, 8, 16)
POOL_G = 4
POOL_C = 128
CONV_W = 3

GW = 512
LANES = 128
NSLOT = GW // LANES
MLA_QK = MLA_NOPE + MLA_ROPE
V7X_VMEM_LIMIT = 56 * 1024 * 1024
NEG = -1e30

C_DQ, C_DK, C_DV, C_NQ, C_NK, C_NV, C_PU, C_ML = 0, 512, 1024, 1536, 2048, 2560, 3072, 3584
ML_W = 640
IN_WP = C_ML + ML_W


def _cp(sem, vmem=V7X_VMEM_LIMIT):
    return pltpu.CompilerParams(dimension_semantics=sem, vmem_limit_bytes=vmem)


def _const_spec(shape):
    nd = len(shape)
    return pl.BlockSpec(shape, lambda *_: (0,) * nd)


def _ada_kernel(c_ref, w_ref, b_ref, o_ref):
    c = c_ref[...]
    s = c * jax.nn.sigmoid(c)
    o_ref[0] = jnp.dot(s.astype(BF16), w_ref[0].astype(BF16), preferred_element_type=F32) + b_ref[0]


def _ada(cv, ada_w, ada_b, tn=1024):
    depth, d, n = ada_w.shape
    r = cv.shape[0]
    return pl.pallas_call(
        _ada_kernel,
        grid=(depth, n // tn),
        in_specs=[pl.BlockSpec((r, d), lambda l, j: (0, 0)),
                  pl.BlockSpec((1, d, tn), lambda l, j: (l, 0, j)),
                  pl.BlockSpec((1, 1, tn), lambda l, j: (l, 0, j))],
        out_specs=pl.BlockSpec((1, r, tn), lambda l, j: (l, 0, j)),
        out_shape=jax.ShapeDtypeStruct((depth, r, n), F32),
        compiler_params=_cp(("parallel", "parallel")),
        name="adaln",
    )(cv, ada_w, ada_b.reshape(depth, 1, n))


def _modulated(x, g, mod, k):
    ms = jnp.mean(x * x, axis=-1, keepdims=True)
    y = x * lax.rsqrt(ms + EPS) * g
    return y * (1.0 + mod[k + 1:k + 2, :]) + mod[k:k + 1, :]


def _group_rms(z, gsum_ref, inv_n, gain):
    ss = jnp.dot((z * z).astype(BF16), gsum_ref[...], preferred_element_type=F32)
    return z * lax.rsqrt(ss * inv_n + EPS) * gain


def _rope_chunks(z, cos_ref, sin_ref, lo_pred, shift_lo, shift_hi):
    c = cos_ref[...]
    s = sin_ref[...]
    outs = []
    for j in range(z.shape[1] // LANES):
        zc = z[:, j * LANES:(j + 1) * LANES]
        partner = jnp.where(lo_pred, pltpu.roll(zc, shift_lo, 1), pltpu.roll(zc, shift_hi, 1))
        outs.append(zc * c + partner * s)
    return jnp.concatenate(outs, axis=1)


def _mla_kv(ckv_b, kp, wuk_ref, wuv_ref, place_ref, g128_ref, kgain):
    kp_hi = kp.astype(BF16)
    kp_lo = (kp - kp_hi.astype(F32)).astype(BF16)
    kh = (jnp.dot(ckv_b, wuk_ref[...], preferred_element_type=F32)
          + jnp.dot(kp_hi, place_ref[...], preferred_element_type=F32)
          + jnp.dot(kp_lo, place_ref[...], preferred_element_type=F32))
    k = _group_rms(kh, g128_ref, 1.0 / MLA_QK, kgain)
    v = jnp.dot(ckv_b, wuv_ref[...], preferred_element_type=F32)
    return k, v


def _inproj_kernel(*refs, rope, states):
    it = iter(refs)
    x_ref, mod_ref, g_ref, w_ref, wuq_ref, wuk_ref, wuv_ref = (next(it) for _ in range(7))
    gains_ref, g64_ref, g128_ref, place_ref = (next(it) for _ in range(4))
    if rope:
        rope_ref = next(it)
    qd_ref, kd_ref, vd_ref, qn_ref, kn_ref, ven_ref, von_ref, pu_ref, qm_ref, km_ref, vm_ref = (
        next(it) for _ in range(11))
    if states:
        ska_ref, sva_ref, sckv_ref, skpe_ref, skc_ref, svc_ref = (next(it) for _ in range(6))

    h = _modulated(x_ref[...], g_ref[...], mod_ref[0], 0).astype(BF16)
    gains = gains_ref[...]
    lane = lax.broadcasted_iota(jnp.int32, (1, LANES), 1)

    def seg(c0, w=GW):
        return jnp.dot(h, w_ref[:, c0:c0 + w], preferred_element_type=F32)

    def rope_d(z):
        if not rope:
            return z
        return _rope_chunks(z, rope_ref.at[0], rope_ref.at[1], (lane % 64) < 32, LANES - 32, 32)

    def rope_m(z):
        if not rope:
            return z
        return _rope_chunks(z, rope_ref.at[2], rope_ref.at[3], lane < MLA_NOPE + MLA_ROPE // 2,
                            LANES - MLA_ROPE // 2, MLA_ROPE // 2)

    qd_ref[...] = rope_d(_group_rms(seg(C_DQ), g64_ref, 1.0 / DIFF_DH, gains[0:1])).astype(BF16)
    kd = _group_rms(seg(C_DK), g64_ref, 1.0 / DIFF_DH, gains[1:2])
    kd_ref[...] = rope_d(kd).astype(BF16)
    vd = seg(C_DV)
    vd_ref[...] = vd.astype(BF16)
    qn_ref[...] = _group_rms(seg(C_NQ), g64_ref, 1.0 / NA_DH, gains[2:3]).astype(BF16)
    kn = _group_rms(seg(C_NK), g64_ref, 1.0 / NA_DH, gains[3:4])
    kn_ref[...] = kn.astype(BF16)
    vn = seg(C_NV)
    lane_w = lax.broadcasted_iota(jnp.int32, (1, GW), 1)
    even = (lane_w % LANES) < NA_DH
    ven_ref[...] = jnp.where(even, vn, 0.0).astype(BF16)
    von_ref[...] = jnp.where(even, 0.0, vn).astype(BF16)
    pu_ref[...] = seg(C_PU)
    zm = seg(C_ML, ML_W)
    cq = zm[:, :MLA_Q_LORA]
    cq = cq * lax.rsqrt(jnp.mean(cq * cq, axis=-1, keepdims=True) + EPS) * gains[6:7, :MLA_Q_LORA]
    qh = jnp.dot(cq.astype(BF16), wuq_ref[...], preferred_element_type=F32)
    qm_ref[...] = rope_m(_group_rms(qh, g128_ref, 1.0 / MLA_QK, gains[4:5])).astype(BF16)
    ck = zm[:, MLA_Q_LORA:MLA_Q_LORA + MLA_KV_LORA]
    ckv = ck * lax.rsqrt(jnp.mean(ck * ck, axis=-1, keepdims=True) + EPS) * gains[7:8, :MLA_KV_LORA]
    kp = zm[:, GW:GW + LANES]
    km, vm = _mla_kv(ckv.astype(BF16), kp, wuk_ref, wuv_ref, place_ref, g128_ref, gains[5:6])
    km_ref[...] = rope_m(km).astype(BF16)
    vm_ref[...] = vm.astype(BF16)
    if states:
        ska_ref[...] = kd
        sva_ref[...] = vd
        sckv_ref[...] = ckv
        skpe_ref[...] = kp[:, :MLA_ROPE]
        skc_ref[...] = kn
        svc_ref[...] = vn


def _inproj(x2, mod, mod_row, g1, W, rope_tab, seq, states, tm=256):
    t, d = x2.shape
    rope = rope_tab is not None
    tiles_per_seq = seq // tm
    tok = lambda w: pl.BlockSpec((tm, w), lambda i: (i, 0))
    in_specs = [tok(d),
                pl.BlockSpec((1, N_MOD, d), lambda i: (mod_row(i), 0, 0)),
                _const_spec((1, d)),
                _const_spec(W["w_in"].shape), _const_spec(W["wuq"].shape),
                _const_spec(W["wuk"].shape), _const_spec(W["wuv"].shape),
                _const_spec(W["gains"].shape), _const_spec(W["g64"].shape),
                _const_spec(W["g128"].shape), _const_spec(W["place"].shape)]
    args = [x2, mod, g1, W["w_in"], W["wuq"], W["wuk"], W["wuv"], W["gains"], W["g64"], W["g128"],
            W["place"]]
    if rope:
        in_specs.append(pl.BlockSpec((4, tm, LANES), lambda i: (0, i % tiles_per_seq, 0)))
        args.append(rope_tab)
    out_specs = [tok(GW)] * 11
    out_shape = [jax.ShapeDtypeStruct((t, GW), BF16)] * 7 + [jax.ShapeDtypeStruct((t, GW), F32)] + \
                [jax.ShapeDtypeStruct((t, GW), BF16)] * 3
    if states:
        widths = (GW, GW, MLA_KV_LORA, MLA_ROPE, GW, GW)
        out_specs += [tok(w) for w in widths]
        out_shape += [jax.ShapeDtypeStruct((t, w), F32) for w in widths]
    return pl.pallas_call(
        functools.partial(_inproj_kernel, rope=rope, states=states),
        grid=(t // tm,),
        in_specs=in_specs, out_specs=out_specs, out_shape=out_shape,
        compiler_params=_cp(("parallel",)),
        name="inproj",
    )(*args)


def _mla_cache_kernel(ckv_ref, kp_ref, wuk_ref, wuv_ref, place_ref, g128_ref, gains_ref, k_ref, v_ref):
    k, v = _mla_kv(ckv_ref[...].astype(BF16), kp_ref[...], wuk_ref, wuv_ref, place_ref, g128_ref,
                   gains_ref[5:6, :])
    k_ref[...] = k.astype(BF16)
    v_ref[...] = v.astype(BF16)


def _mla_cache(ckv2, kp2, W, tm=256):
    t = ckv2.shape[0]
    tok = lambda w: pl.BlockSpec((tm, w), lambda i: (i, 0))
    return pl.pallas_call(
        _mla_cache_kernel,
        grid=(t // tm,),
        in_specs=[tok(MLA_KV_LORA), tok(LANES), _const_spec(W["wuk"].shape), _const_spec(W["wuv"].shape),
                  _const_spec(W["place"].shape), _const_spec(W["g128"].shape),
                  _const_spec(W["gains"].shape)],
        out_specs=[tok(GW), tok(GW)],
        out_shape=[jax.ShapeDtypeStruct((t, GW), BF16)] * 2,
        compiler_params=_cp(("parallel",)),
        name="mla_cache_kv",
    )(ckv2, kp2, W["wuk"], W["wuv"], W["place"], W["g128"], W["gains"])


def _attn_kernel(*refs, mode, chunks, lam_init):
    it = iter(refs)
    q_ref = next(it)
    srcs = []
    for _ in chunks:
        k_ref = next(it)
        v_refs = (next(it), next(it)) if mode == "pair" else (next(it),)
        srcs.append((k_ref, v_refs))
    if mode == "diff":
        lam_ref = next(it)
        subg_ref = next(it)
    o_ref = next(it)

    tq = q_ref.shape[1]
    lane = lax.broadcasted_iota(jnp.int32, (1, LANES), 1)
    nstream = 1 if mode == "full" else 2

    if mode == "diff":
        lf = lam_ref[...]
        lam = (jnp.exp(jnp.sum(lf[0:1] * lf[1:2], axis=-1, keepdims=True))
               - jnp.exp(jnp.sum(lf[2:3] * lf[3:4], axis=-1, keepdims=True)) + lam_init)

    for slot in range(NSLOT):
        sl = slice(slot * LANES, (slot + 1) * LANES)
        q = q_ref[0, :, sl]
        if mode == "full":
            qs = [q]
        else:
            zero = jnp.zeros_like(q)
            qs = [jnp.where(lane < 64, q, zero), jnp.where(lane < 64, zero, q)]

        def step(k_c, v_cs, state):
            new = []
            for si in range(nstream):
                m, l, acc = state[3 * si:3 * si + 3]
                s = lax.dot_general(qs[si], k_c, (((1,), (1,)), ((), ())), preferred_element_type=F32)
                m_new = jnp.maximum(m, jnp.max(s, axis=-1, keepdims=True))
                alpha = jnp.exp(m - m_new)
                p = jnp.exp(s - m_new)
                l = alpha * l + jnp.sum(p, axis=-1, keepdims=True)
                acc = alpha * acc + jnp.dot(p.astype(BF16), v_cs[si % len(v_cs)],
                                            preferred_element_type=F32)
                new += [m_new, l, acc]
            return tuple(new)

        state = ()
        for _ in range(nstream):
            state += (jnp.full((tq, 1), NEG, F32), jnp.zeros((tq, 1), F32), jnp.zeros((tq, LANES), F32))
        for (k_ref, v_refs), (n_chunk, tk) in zip(srcs, chunks):
            if n_chunk == 1:
                state = step(k_ref[0, :, sl], [v[0, :, sl] for v in v_refs], state)
            else:
                def body(c, st, k_ref=k_ref, v_refs=v_refs, tk=tk):
                    rows = pl.ds(pl.multiple_of(c * tk, tk), tk)
                    return step(k_ref[0, rows, sl], [v[0, rows, sl] for v in v_refs], st)
                state = lax.fori_loop(0, n_chunk, body, state)

        outs = [state[3 * si + 2] / state[3 * si + 1] for si in range(nstream)]
        if mode == "full":
            o = outs[0]
        elif mode == "pair":
            o = outs[0] + outs[1]
        else:
            o = outs[0] - lam * outs[1]
            o = o * lax.rsqrt(jnp.mean(o * o, axis=-1, keepdims=True) + EPS) * subg_ref[...]
            o = o * (1.0 - lam_init)
        o_ref[0, :, sl] = o.astype(o_ref.dtype)


def _attention(mode, q, srcs, tq, tk, lam=None, subg=None, lam_init=0.0):
    b, n, _ = q.shape
    in_specs = [pl.BlockSpec((1, tq, GW), lambda bi, qi: (bi, qi, 0))]
    args = [q]
    chunks = []
    for src in srcs:
        ks = src[0].shape[1]
        step = min(tk, ks)
        chunks.append((ks // step, step))
        for a in src:
            in_specs.append(pl.BlockSpec((1, ks, GW), lambda bi, qi: (bi, 0, 0)))
            args.append(a)
    if mode == "diff":
        in_specs += [_const_spec(lam.shape), _const_spec(subg.shape)]
        args += [lam, subg]
    return pl.pallas_call(
        functools.partial(_attn_kernel, mode=mode, chunks=tuple(chunks), lam_init=lam_init),
        grid=(b, n // tq),
        in_specs=in_specs,
        out_specs=pl.BlockSpec((1, tq, GW), lambda bi, qi: (bi, qi, 0)),
        out_shape=jax.ShapeDtypeStruct((b, n, GW), BF16),
        compiler_params=_cp(("parallel", "parallel")),
        name="attn_" + mode,
    )(*args)


def _na_row_start(r, rows):
    kr = min(NA_ROWS, rows)
    return jnp.clip(r - kr // 2, 0, rows - kr)


def _na_kernel(q_ref, k_ref, ve_ref, vo_ref, kc_ref, vec_ref, voc_ref, bias_ref, o_ref, *, rows, n_loc):
    r = pl.program_id(1)
    rs = _na_row_start(r, rows)
    loc = pl.ds(pl.multiple_of(rs * GRID_W, GRID_W), n_loc)
    lane = lax.broadcasted_iota(jnp.int32, (1, LANES), 1)
    for slot in range(NSLOT):
        sl = slice(slot * LANES, (slot + 1) * LANES)
        q = q_ref[0, :, sl]
        zero = jnp.zeros_like(q)
        k_loc = k_ref[0, loc, sl]
        k_ctx = kc_ref[0, :, sl]
        o = None
        for half in range(2):
            qh = jnp.where(lane < 64, q, zero) if half == 0 else jnp.where(lane < 64, zero, q)
            v_loc = (ve_ref if half == 0 else vo_ref)[0, loc, sl]
            v_ctx = (vec_ref if half == 0 else voc_ref)[0, :, sl]
            nt = (((1,), (1,)), ((), ()))
            s_loc = lax.dot_general(qh, k_loc, nt, preferred_element_type=F32) + bias_ref[0, 2 * slot + half]
            s_ctx = lax.dot_general(qh, k_ctx, nt, preferred_element_type=F32)
            m = jnp.maximum(jnp.max(s_loc, axis=-1, keepdims=True), jnp.max(s_ctx, axis=-1, keepdims=True))
            p_loc = jnp.exp(s_loc - m)
            p_ctx = jnp.exp(s_ctx - m)
            l = jnp.sum(p_loc, axis=-1, keepdims=True) + jnp.sum(p_ctx, axis=-1, keepdims=True)
            oh = (jnp.dot(p_loc.astype(BF16), v_loc, preferred_element_type=F32)
                  + jnp.dot(p_ctx.astype(BF16), v_ctx, preferred_element_type=F32)) / l
            o = oh if o is None else o + oh
        o_ref[0, :, sl] = o.astype(o_ref.dtype)


def _na_bias_table(bias_tab, rows):
    kr = min(NA_ROWS, rows)
    qc = np.arange(GRID_W)[:, None]
    kc = np.arange(GRID_W)[None, :]
    cstart = np.clip(qc - NA_COLS // 2, 0, GRID_W - NA_COLS)
    in_win = (kc >= cstart) & (kc < cstart + NA_COLS)
    dc_idx = np.clip(kc - qc, 1 - NA_COLS, NA_COLS - 1) + NA_COLS - 1
    var = np.arange(kr)[:, None]
    dr_idx = np.arange(kr)[None, :] - var + NA_ROWS - 1
    tab = bias_tab.astype(F32)[:, dr_idx[:, :, None, None], dc_idx[None, None, :, :]]
    tab = jnp.where(in_win[None, None, None], tab, NEG)
    tab = jnp.transpose(tab, (1, 0, 3, 2, 4))
    return tab.reshape(kr, NA_H, GRID_W, kr * GRID_W)


def _na_latent(q, k, ve, vo, kc, vec, voc, bias):
    b, n, _ = q.shape
    rows = n // GRID_W
    kr = min(NA_ROWS, rows)
    n_loc = kr * GRID_W
    pc = kc.shape[1]
    full = lambda m: pl.BlockSpec((1, m, GW), lambda bi, r: (bi, 0, 0))
    return pl.pallas_call(
        functools.partial(_na_kernel, rows=rows, n_loc=n_loc),
        grid=(b, rows),
        in_specs=[pl.BlockSpec((1, GRID_W, GW), lambda bi, r: (bi, r, 0)),
                  full(n), full(n), full(n), full(pc), full(pc), full(pc),
                  pl.BlockSpec((1, NA_H, GRID_W, n_loc),
                               lambda bi, r: (r - _na_row_start(r, rows), 0, 0, 0))],
        out_specs=pl.BlockSpec((1, GRID_W, GW), lambda bi, r: (bi, r, 0)),
        out_shape=jax.ShapeDtypeStruct((b, n, GW), BF16),
        compiler_params=_cp(("parallel", "parallel")),
        name="attn_neighbourhood",
    )(q, k, ve, vo, kc, vec, voc, bias)


POOL_HALO = 64


def _pool_kernel(prev_ref, main_ref, next_ref, w_ref, scale_ref, o_ref, *, seq):
    tp = main_ref.shape[1]
    t0 = pl.program_id(1) * tp
    ext = jnp.concatenate([prev_ref[0], main_ref[0], next_ref[0]], axis=0)
    ext_hi = ext.astype(BF16)
    ext_lo = (ext - ext_hi.astype(F32)).astype(BF16)
    t = t0 + lax.broadcasted_iota(jnp.int32, (tp, 1), 0)
    pos = t0 - POOL_HALO + lax.broadcasted_iota(jnp.int32, (1, tp + 2 * POOL_HALO), 1)
    for g, win in enumerate(POOL_WINDOWS):
        sl = slice(g * POOL_C, (g + 1) * POOL_C)
        lo = jnp.clip(t - win // 2, 0, seq)
        hi = jnp.clip(t - win // 2 + win, 0, seq)
        band = jnp.where((pos >= lo) & (pos < hi), 1.0, 0.0).astype(BF16)
        tot = (jnp.dot(band, ext_hi[:, sl], preferred_element_type=F32)
               + jnp.dot(band, ext_lo[:, sl], preferred_element_type=F32))
        mean = tot / (hi - lo).astype(F32)
        d = (mean - main_ref[0, :, sl]).astype(BF16)
        y = jnp.dot(d, w_ref[g], preferred_element_type=F32) * scale_ref[:, sl]
        o_ref[0, :, sl] = y.astype(o_ref.dtype)


def _pool(pu, w_pool, scale, tp=256):
    b, n, _ = pu.shape
    hb = tp // POOL_HALO
    last = n // POOL_HALO - 1
    return pl.pallas_call(
        functools.partial(_pool_kernel, seq=n),
        grid=(b, n // tp),
        in_specs=[pl.BlockSpec((1, POOL_HALO, GW), lambda bi, i: (bi, jnp.maximum(i * hb - 1, 0), 0)),
                  pl.BlockSpec((1, tp, GW), lambda bi, i: (bi, i, 0)),
                  pl.BlockSpec((1, POOL_HALO, GW), lambda bi, i: (bi, jnp.minimum((i + 1) * hb, last), 0)),
                  _const_spec(w_pool.shape), _const_spec(scale.shape)],
        out_specs=pl.BlockSpec((1, tp, GW), lambda bi, i: (bi, i, 0)),
        out_shape=jax.ShapeDtypeStruct((b, n, GW), BF16),
        compiler_params=_cp(("parallel", "parallel")),
        name="pool_mixer",
    )(pu, pu, pu, w_pool, scale)


def _outproj_kernel(oa_ref, ob_ref, oc_ref, od_ref, x_ref, mod_ref, w_ref, o_ref):
    mix = jnp.dot(oa_ref[...], w_ref[0:GW, :], preferred_element_type=F32)
    mix += jnp.dot(ob_ref[...], w_ref[GW:2 * GW, :], preferred_element_type=F32)
    mix += jnp.dot(oc_ref[...], w_ref[2 * GW:3 * GW, :], preferred_element_type=F32)
    mix += jnp.dot(od_ref[...], w_ref[3 * GW:4 * GW, :], preferred_element_type=F32)
    o_ref[...] = x_ref[...] + mod_ref[0, 2:3, :] * mix


def _outproj(oa, ob, oc, od, x2, mod, mod_row, w_out, tm=256):
    t, d = x2.shape
    tok = lambda w: pl.BlockSpec((tm, w), lambda i: (i, 0))
    return pl.pallas_call(
        _outproj_kernel,
        grid=(t // tm,),
        in_specs=[tok(GW)] * 4 + [tok(d), pl.BlockSpec((1, N_MOD, d), lambda i: (mod_row(i), 0, 0)),
                                   _const_spec(w_out.shape)],
        out_specs=tok(d),
        out_shape=jax.ShapeDtypeStruct((t, d), F32),
        compiler_params=_cp(("parallel",)),
        name="outproj",
    )(oa, ob, oc, od, x2, mod, w_out)


FFN_HALO = 16


def _ffn_kernel(prev_ref, x_ref, next_ref, mod_ref, g_ref, wg_ref, wv_ref, cwg_ref, cwv_ref, cbg_ref,
                cbv_ref, wd_ref, o_ref, h_ref, acc_ref, *, tiles_per_seq):
    i = pl.program_id(0)
    j = pl.program_id(1)
    tm = x_ref.shape[0]

    @pl.when(j == 0)
    def _():
        mod = mod_ref[0]
        g = g_ref[...]
        first = (i % tiles_per_seq) == 0
        last = (i % tiles_per_seq) == tiles_per_seq - 1
        hp = jnp.where(first, 0.0, _modulated(prev_ref[...], g, mod, 3))
        hn = jnp.where(last, 0.0, _modulated(next_ref[...], g, mod, 3))
        h_ref[0:FFN_HALO, :] = hp.astype(BF16)
        h_ref[FFN_HALO:FFN_HALO + tm, :] = _modulated(x_ref[...], g, mod, 3).astype(BF16)
        h_ref[FFN_HALO + tm:, :] = hn.astype(BF16)
        acc_ref[...] = jnp.zeros_like(acc_ref)

    h = h_ref[...]
    rows = tm + 2 * FFN_HALO

    def conv(w_ref, cw_ref, cb_ref):
        u = jnp.dot(h, w_ref[...], preferred_element_type=F32)
        cw = cw_ref[...]
        y = (pltpu.roll(u, 1, 0) * cw[0:1] + u * cw[1:2] + pltpu.roll(u, rows - 1, 0) * cw[2:3])
        return y[FFN_HALO:FFN_HALO + tm] + cb_ref[...]

    gate = conv(wg_ref, cwg_ref, cbg_ref)
    val = conv(wv_ref, cwv_ref, cbv_ref)
    a = (gate * jax.nn.sigmoid(gate) * val).astype(BF16)
    acc_ref[...] += jnp.dot(a, wd_ref[...], preferred_element_type=F32)

    @pl.when(j == pl.num_programs(1) - 1)
    def _():
        o_ref[...] = x_ref[...] + mod_ref[0, 5:6, :] * acc_ref[...]


def _ffn(x2, mod, mod_row, g2, w_up, conv_w, conv_b, w_down, seq, tm=512, tn=512):
    t, d = x2.shape
    dff = w_down.shape[0]
    tm = min(tm, seq)
    tiles_per_seq = seq // tm
    nj = dff // tn
    hb = tm // FFN_HALO
    nhb = t // FFN_HALO
    return pl.pallas_call(
        functools.partial(_ffn_kernel, tiles_per_seq=tiles_per_seq),
        grid=(t // tm, nj),
        in_specs=[pl.BlockSpec((FFN_HALO, d), lambda i, j: (jnp.maximum(i * hb - 1, 0), 0)),
                  pl.BlockSpec((tm, d), lambda i, j: (i, 0)),
                  pl.BlockSpec((FFN_HALO, d), lambda i, j: (jnp.minimum((i + 1) * hb, nhb - 1), 0)),
                  pl.BlockSpec((1, N_MOD, d), lambda i, j: (mod_row(i), 0, 0)),
                  _const_spec((1, d)),
                  pl.BlockSpec((d, tn), lambda i, j: (0, j)),
                  pl.BlockSpec((d, tn), lambda i, j: (0, nj + j)),
                  pl.BlockSpec((CONV_W, tn), lambda i, j: (0, j)),
                  pl.BlockSpec((CONV_W, tn), lambda i, j: (0, nj + j)),
                  pl.BlockSpec((1, tn), lambda i, j: (0, j)),
                  pl.BlockSpec((1, tn), lambda i, j: (0, nj + j)),
                  pl.BlockSpec((tn, d), lambda i, j: (j, 0))],
        out_specs=pl.BlockSpec((tm, d), lambda i, j: (i, 0)),
        out_shape=jax.ShapeDtypeStruct((t, d), F32),
        scratch_shapes=[pltpu.VMEM((tm + 2 * FFN_HALO, d), BF16), pltpu.VMEM((tm, d), F32)],
        compiler_params=_cp(("parallel", "arbitrary")),
        name="conv_ffn",
    )(x2, x2, x2, mod, g2, w_up, w_up, conv_w, conv_w, conv_b, conv_b, w_down)


def _block_ones(group):
    idx = np.arange(GW) // group
    return jnp.asarray(idx[:, None] == idx[None, :], dtype=BF16)


def _pad_heads(w, heads, width):
    lead = w.shape[:-1]
    w = w.reshape(lead + (heads, width))
    w = jnp.pad(w, [(0, 0)] * len(lead) + [(0, 0), (0, LANES - width)])
    return w.reshape(lead + (heads * LANES,))


def _layer_weights(P, l):
    d = P["w_in"].shape[1]
    w_in = P["w_in"][l]
    o = np.cumsum((0, 512, 512, 512, MLA_Q_LORA, MLA_KV_LORA, MLA_ROPE, 512, 512, 512, 512))
    col = lambda k: w_in[:, o[k]:o[k + 1]]
    w_all = jnp.concatenate(
        [col(0), col(1), col(2), col(6), col(7), col(8), col(9), col(3), col(4), col(5),
         jnp.zeros((d, ML_W - MLA_Q_LORA - MLA_KV_LORA - MLA_ROPE), F32)], axis=1).astype(BF16)
    wukv = P["mla_w_ukv"][l].reshape(MLA_KV_LORA, MLA_H, MLA_NOPE + MLA_V)
    wuk = _pad_heads(wukv[:, :, :MLA_NOPE].reshape(MLA_KV_LORA, MLA_H * MLA_NOPE), MLA_H, MLA_NOPE)
    wuv = wukv[:, :, MLA_NOPE:].reshape(MLA_KV_LORA, MLA_H * MLA_V)
    place = np.zeros((LANES, GW), np.float32)
    for hh in range(MLA_H):
        place[np.arange(MLA_ROPE), hh * LANES + MLA_NOPE + np.arange(MLA_ROPE)] = 1.0
    tile = lambda g, reps: jnp.tile(g.astype(F32), reps)
    padw = lambda g: jnp.pad(g.astype(F32), (0, GW - g.shape[0]))
    gains = jnp.stack([
        tile(P["diff_qn_g"][l], 8) * DIFF_DH ** -0.5,
        tile(P["diff_kn_g"][l], 8),
        tile(P["na_qn_g"][l], 8) * NA_DH ** -0.5,
        tile(P["na_kn_g"][l], 8),
        _pad_heads(tile(P["mla_qn_g"][l], MLA_H), MLA_H, MLA_QK) * MLA_QK ** -0.5,
        _pad_heads(tile(P["mla_kn_g"][l], MLA_H), MLA_H, MLA_QK),
        padw(P["mla_qa_g"][l]),
        padw(P["mla_kva_g"][l]),
    ])
    return dict(
        w_in=w_all,
        wuq=_pad_heads(P["mla_w_uq"][l], MLA_H, MLA_QK).astype(BF16),
        wuk=wuk.astype(BF16), wuv=wuv.astype(BF16),
        place=jnp.asarray(place, dtype=BF16),
        gains=gains, g64=_block_ones(64), g128=_block_ones(LANES),
        norm1=P["norm1_g"][l].reshape(1, d).astype(F32),
        norm2=P["norm2_g"][l].reshape(1, d).astype(F32),
        lam=P["diff_lam"][l].astype(F32),
        subg=P["diff_sub_g"][l].reshape(1, 2 * DIFF_DH).astype(F32),
        pool_w=P["pool_w"][l].astype(BF16),
        pool_scale=P["pool_scale"][l].reshape(1, GW).astype(F32),
        w_out=P["w_out"][l].astype(BF16),
        w_up=P["w_up"][l].astype(BF16),
        conv_w=P["conv_w"][l].astype(F32),
        conv_b=P["conv_b"][l].reshape(1, -1).astype(F32),
        w_down=P["w_down"][l].astype(BF16),
    )


def _rope_tables(n):
    t = jnp.arange(n)
    rows = (t // GRID_W).astype(F32)
    cols = (t % GRID_W).astype(F32)

    def ang(dim):
        quarter = dim // 4
        inv = ROPE_BASE ** (-jnp.arange(quarter, dtype=F32) / quarter)
        return jnp.concatenate([rows[:, None] * inv, cols[:, None] * inv], axis=-1)

    a = ang(DIFF_DH)
    cos_a, sin_a = jnp.cos(a), jnp.sin(a)
    cd = jnp.tile(cos_a, (1, 4))
    sd = jnp.tile(jnp.concatenate([-sin_a, sin_a], axis=1), (1, 2))
    b = ang(MLA_ROPE)
    cos_b, sin_b = jnp.cos(b), jnp.sin(b)
    ones = jnp.ones((n, MLA_NOPE), F32)
    pad = LANES - MLA_NOPE - MLA_ROPE
    cm = jnp.concatenate([ones, cos_b, cos_b, jnp.ones((n, pad), F32)], axis=1)
    sm = jnp.concatenate([0.0 * ones, -sin_b, sin_b, jnp.zeros((n, pad), F32)], axis=1)
    return jnp.stack([cd, sd, cm, sm])


def _layer(x, mod, mod_row_of_batch, W, layer, cache, rope_tab, bias_tab):
    b, n, d = x.shape
    t = b * n
    x2 = x.reshape(t, d)
    lam_init = 0.8 - 0.6 * math.exp(-0.3 * layer)
    tm = 256
    row_in = lambda i: mod_row_of_batch(i // (n // tm))
    outs = _inproj(x2, mod, row_in, W["norm1"], W, rope_tab, n, states=cache is None, tm=tm)
    qd, kd, vd, qn, kn, ven, von, pu, qm, km, vm = [a.reshape(b, n, GW) for a in outs[:11]]
    if cache is None:
        oa = _attention("diff", qd, [(kd, vd)], n, n, W["lam"], W["subg"], lam_init)
        ob = _attention("full", qm, [(km, vm)], n, n)
        oc = _attention("pair", qn, [(kn, ven, von)], n, n)
        state = outs[11:]
    else:
        a_k, a_v, b_ckv, b_kpe, c_k, c_v = cache
        pc = a_k.shape[1]
        ck = a_k.reshape(b, pc, GW).astype(BF16)
        cv = a_v.reshape(b, pc, GW).astype(BF16)
        oa = _attention("diff", qd, [(ck, cv), (kd, vd)], 256, 512, W["lam"], W["subg"], lam_init)
        kpe_pad = jnp.pad(b_kpe.reshape(b * pc, MLA_ROPE), ((0, 0), (0, LANES - MLA_ROPE)))
        kmc, vmc = _mla_cache(b_ckv.reshape(b * pc, MLA_KV_LORA), kpe_pad, W)
        ob = _attention("full", qm, [(kmc.reshape(b, pc, GW), vmc.reshape(b, pc, GW)), (km, vm)], 256, 512)
        nk = c_k.reshape(b, pc, GW).astype(BF16)
        nv = c_v.reshape(b, pc, GW)
        even = (jnp.arange(GW) % LANES) < NA_DH
        nve = jnp.where(even, nv, 0.0).astype(BF16)
        nvo = jnp.where(even, 0.0, nv).astype(BF16)
        oc = _na_latent(qn, kn, ven, von, nk, nve, nvo, bias_tab)
        state = ()
    od = _pool(pu, W["pool_w"], W["pool_scale"])
    flat = lambda a: a.reshape(t, GW)
    x2 = _outproj(flat(oa), flat(ob), flat(oc), flat(od), x2, mod, row_in, W["w_out"], tm=tm)
    tmf = min(512, n)
    row_ffn = lambda i: mod_row_of_batch(i // (n // tmf))
    x2 = _ffn(x2, mod, row_ffn, W["norm2"], W["w_up"], W["conv_w"], W["conv_b"], W["w_down"], n, tm=tmf)
    return x2.reshape(b, n, d), state


def kernel(x_prompt, x_sample, cache_diff_k, cache_diff_v, cache_mla_ckv, cache_mla_kpe, cache_na_k, cache_na_v, c, c_ctx, norm1_g, norm2_g, ada_w, ada_b, w_in, diff_qn_g, diff_kn_g, diff_lam, diff_sub_g, mla_qa_g, mla_kva_g, mla_w_uq, mla_w_ukv, mla_qn_g, mla_kn_g, na_qn_g, na_kn_g, na_bias, pool_w, pool_scale, w_out, w_up, conv_w, conv_b, w_down):
    P = dict(norm1_g=norm1_g, norm2_g=norm2_g, w_in=w_in, diff_qn_g=diff_qn_g, diff_kn_g=diff_kn_g,
             diff_lam=diff_lam, diff_sub_g=diff_sub_g, mla_qa_g=mla_qa_g, mla_kva_g=mla_kva_g,
             mla_w_uq=mla_w_uq, mla_w_ukv=mla_w_ukv, mla_qn_g=mla_qn_g, mla_kn_g=mla_kn_g,
             na_qn_g=na_qn_g, na_kn_g=na_kn_g, pool_w=pool_w, pool_scale=pool_scale, w_out=w_out,
             w_up=w_up, conv_w=conv_w, conv_b=conv_b, w_down=w_down)
    depth = w_in.shape[0]
    d = x_prompt.shape[-1]
    bd, nd = x_sample.shape[:2]
    n_rows = -(-(bd + 1) // 8) * 8
    cv = jnp.concatenate([c, c_ctx[None, :], jnp.zeros((n_rows - bd - 1, d), F32)], axis=0)
    mod_all = _ada(cv, ada_w, ada_b).reshape(depth, n_rows, N_MOD, d)
    rope_tab = _rope_tables(nd)
    Ws = [_layer_weights(P, l) for l in range(depth)]

    xp = x_prompt
    states = []
    for l in range(depth):
        xp, st = _layer(xp, mod_all[l], lambda bi: bd, Ws[l], l, None, None, None)
        states.append(st)
    xs = x_sample
    for l in range(depth):
        cache_l = (cache_diff_k[:, l], cache_diff_v[:, l], cache_mla_ckv[:, l], cache_mla_kpe[:, l],
                   cache_na_k[:, l], cache_na_v[:, l])
        bias_tab = _na_bias_table(na_bias[l], nd // GRID_W)
        xs, _ = _layer(xs, mod_all[l], lambda bi: bi, Ws[l], l, cache_l, rope_tab, bias_tab)

    bp, npr = x_prompt.shape[:2]

    def stack(k, shape):
        return jnp.stack([s[k].reshape((bp, npr) + shape) for s in states], axis=1)

    return (xp, xs,
            stack(0, (DIFF_H, 2, DIFF_DH)), stack(1, (DIFF_H, 2 * DIFF_DH)),
            stack(2, (MLA_KV_LORA,)), stack(3, (MLA_ROPE,)),
            stack(4, (NA_H, NA_DH)), stack(5, (NA_H, NA_DH)))
```

```python
import functools
import math

import numpy as np
import jax
import jax.numpy as jnp
from jax import lax
from jax.experimental import pallas as pl
from jax.experimental.pallas import tpu as pltpu

F32 = jnp.float32
BF16 = jnp.bfloat16

GRID_W = 64
ROPE_BASE = 10000.0
EPS = 1e-6
N_MOD = 6
DIFF_H = 4
DIFF_DH = 64
MLA_H = 4
MLA_NOPE = 64
MLA_ROPE = 32
MLA_V = 128
MLA_Q_LORA = 384
MLA_KV_LORA = 128
NA_H = 8
NA_DH = 64
NA_ROWS = 8
NA_COLS = 16
POOL_WINDOWS = (2, 4, 8, 16)
POOL_G = 4
POOL_C = 128
CONV_W = 3

GW = 512
LANES = 128
NSLOT = GW // LANES
MLA_QK = MLA_NOPE + MLA_ROPE
V7X_VMEM_LIMIT = 56 * 1024 * 1024
NEG = -1e30
LOG2E = math.log2(math.e)

C_DQ, C_DK, C_DV, C_NQ, C_NK, C_NV, C_PU, C_ML = 0, 512, 1024, 1536, 2048, 2560, 3072, 3584
ML_W = 640
IN_WP = C_ML + ML_W


def _cp(sem, vmem=V7X_VMEM_LIMIT):
    return pltpu.CompilerParams(dimension_semantics=sem, vmem_limit_bytes=vmem)


def _const_spec(shape):
    nd = len(shape)
    return pl.BlockSpec(shape, lambda *_: (0,) * nd)


def _ada_kernel(c_ref, w_ref, b_ref, o_ref):
    c = c_ref[...]
    s = c * jax.nn.sigmoid(c)
    o_ref[0] = jnp.dot(s.astype(BF16), w_ref[0].astype(BF16), preferred_element_type=F32) + b_ref[0]


def _ada(cv, ada_w, ada_b, tn=1024):
    depth, d, n = ada_w.shape
    r = cv.shape[0]
    return pl.pallas_call(
        _ada_kernel,
        grid=(depth, n // tn),
        in_specs=[pl.BlockSpec((r, d), lambda l, j: (0, 0)),
                  pl.BlockSpec((1, d, tn), lambda l, j: (l, 0, j)),
                  pl.BlockSpec((1, 1, tn), lambda l, j: (l, 0, j))],
        out_specs=pl.BlockSpec((1, r, tn), lambda l, j: (l, 0, j)),
        out_shape=jax.ShapeDtypeStruct((depth, r, n), F32),
        compiler_params=_cp(("parallel", "parallel")),
        name="adaln",
    )(cv, ada_w, ada_b.reshape(depth, 1, n))


def _modulated(x, g, mod, k):
    ms = jnp.mean(x * x, axis=-1, keepdims=True)
    y = x * lax.rsqrt(ms + EPS) * g
    return y * (1.0 + mod[k + 1:k + 2, :]) + mod[k:k + 1, :]


def _group_rms(z, gsum_ref, inv_n, gain):
    ss = jnp.dot((z * z).astype(BF16), gsum_ref[...], preferred_element_type=F32)
    return z * lax.rsqrt(ss * inv_n + EPS) * gain


def _rope_chunks(z, cos_ref, sin_ref, lo_pred, shift_lo, shift_hi):
    c = cos_ref[...]
    s = sin_ref[...]
    outs = []
    for j in range(z.shape[1] // LANES):
        zc = z[:, j * LANES:(j + 1) * LANES]
        partner = jnp.where(lo_pred, pltpu.roll(zc, shift_lo, 1), pltpu.roll(zc, shift_hi, 1))
        outs.append(zc * c + partner * s)
    return jnp.concatenate(outs, axis=1)


def _mla_kv(ckv_b, kp, wuk_ref, wuv_ref, place_ref, g128_ref, kgain):
    kp_hi = kp.astype(BF16)
    kp_lo = (kp - kp_hi.astype(F32)).astype(BF16)
    kh = (jnp.dot(ckv_b, wuk_ref[...], preferred_element_type=F32)
          + jnp.dot(kp_hi, place_ref[...], preferred_element_type=F32)
          + jnp.dot(kp_lo, place_ref[...], preferred_element_type=F32))
    k = _group_rms(kh, g128_ref, 1.0 / MLA_QK, kgain)
    v = jnp.dot(ckv_b, wuv_ref[...], preferred_element_type=F32)
    return k, v


def _inproj_kernel(*refs, rope, states):
    it = iter(refs)
    x_ref, mod_ref, g_ref, w_ref, wuq_ref, wuk_ref, wuv_ref = (next(it) for _ in range(7))
    gains_ref, g64_ref, g128_ref, place_ref = (next(it) for _ in range(4))
    if rope:
        rope_ref = next(it)
    qd_ref, kd_ref, vd_ref, qn_ref, kn_ref, ven_ref, von_ref, pu_ref, qm_ref, km_ref, vm_ref = (
        next(it) for _ in range(11))
    if states:
        ska_ref, sva_ref, sckv_ref, skpe_ref, skc_ref, svc_ref = (next(it) for _ in range(6))

    h = _modulated(x_ref[...], g_ref[...], mod_ref[0], 0).astype(BF16)
    gains = gains_ref[...]
    lane = lax.broadcasted_iota(jnp.int32, (1, LANES), 1)

    def seg(c0, w=GW):
        return jnp.dot(h, w_ref[:, c0:c0 + w], preferred_element_type=F32)

    def rope_d(z):
        if not rope:
            return z
        return _rope_chunks(z, rope_ref.at[0], rope_ref.at[1], (lane % 64) < 32, LANES - 32, 32)

    def rope_m(z):
        if not rope:
            return z
        return _rope_chunks(z, rope_ref.at[2], rope_ref.at[3], lane < MLA_NOPE + MLA_ROPE // 2,
                            LANES - MLA_ROPE // 2, MLA_ROPE // 2)

    qd_ref[...] = rope_d(_group_rms(seg(C_DQ), g64_ref, 1.0 / DIFF_DH, gains[0:1])).astype(BF16)
    kd = _group_rms(seg(C_DK), g64_ref, 1.0 / DIFF_DH, gains[1:2])
    kd_ref[...] = rope_d(kd).astype(BF16)
    vd = seg(C_DV)
    vd_ref[...] = vd.astype(BF16)
    qn_ref[...] = _group_rms(seg(C_NQ), g64_ref, 1.0 / NA_DH, gains[2:3]).astype(BF16)
    kn = _group_rms(seg(C_NK), g64_ref, 1.0 / NA_DH, gains[3:4])
    kn_ref[...] = kn.astype(BF16)
    vn = seg(C_NV)
    lane_w = lax.broadcasted_iota(jnp.int32, (1, GW), 1)
    even = (lane_w % LANES) < NA_DH
    ven_ref[...] = jnp.where(even, vn, 0.0).astype(BF16)
    von_ref[...] = jnp.where(even, 0.0, vn).astype(BF16)
    pu_ref[...] = seg(C_PU)
    zm = seg(C_ML, ML_W)
    cq = zm[:, :MLA_Q_LORA]
    cq = cq * lax.rsqrt(jnp.mean(cq * cq, axis=-1, keepdims=True) + EPS) * gains[6:7, :MLA_Q_LORA]
    qh = jnp.dot(cq.astype(BF16), wuq_ref[...], preferred_element_type=F32)
    qm_ref[...] = rope_m(_group_rms(qh, g128_ref, 1.0 / MLA_QK, gains[4:5])).astype(BF16)
    ck = zm[:, MLA_Q_LORA:MLA_Q_LORA + MLA_KV_LORA]
    ckv = ck * lax.rsqrt(jnp.mean(ck * ck, axis=-1, keepdims=True) + EPS) * gains[7:8, :MLA_KV_LORA]
    kp = zm[:, GW:GW + LANES]
    km, vm = _mla_kv(ckv.astype(BF16), kp, wuk_ref, wuv_ref, place_ref, g128_ref, gains[5:6])
    km_ref[...] = rope_m(km).astype(BF16)
    vm_ref[...] = vm.astype(BF16)
    if states:
        ska_ref[...] = kd
        sva_ref[...] = vd
        sckv_ref[...] = ckv
        skpe_ref[...] = kp[:, :MLA_ROPE]
        skc_ref[...] = kn
        svc_ref[...] = vn


def _inproj(x2, mod, mod_row, g1, W, rope_tab, seq, states, tm=256):
    t, d = x2.shape
    rope = rope_tab is not None
    tiles_per_seq = seq // tm
    tok = lambda w: pl.BlockSpec((tm, w), lambda i: (i, 0))
    in_specs = [tok(d),
                pl.BlockSpec((1, N_MOD, d), lambda i: (mod_row(i), 0, 0)),
                _const_spec((1, d)),
                _const_spec(W["w_in"].shape), _const_spec(W["wuq"].shape),
                _const_spec(W["wuk"].shape), _const_spec(W["wuv"].shape),
                _const_spec(W["gains"].shape), _const_spec(W["g64"].shape),
                _const_spec(W["g128"].shape), _const_spec(W["place"].shape)]
    args = [x2, mod, g1, W["w_in"], W["wuq"], W["wuk"], W["wuv"], W["gains"], W["g64"], W["g128"],
            W["place"]]
    if rope:
        in_specs.append(pl.BlockSpec((4, tm, LANES), lambda i: (0, i % tiles_per_seq, 0)))
        args.append(rope_tab)
    out_specs = [tok(GW)] * 11
    out_shape = [jax.ShapeDtypeStruct((t, GW), BF16)] * 7 + [jax.ShapeDtypeStruct((t, GW), F32)] + \
                [jax.ShapeDtypeStruct((t, GW), BF16)] * 3
    if states:
        widths = (GW, GW, MLA_KV_LORA, MLA_ROPE, GW, GW)
        out_specs += [tok(w) for w in widths]
        out_shape += [jax.ShapeDtypeStruct((t, w), F32) for w in widths]
    return pl.pallas_call(
        functools.partial(_inproj_kernel, rope=rope, states=states),
        grid=(t // tm,),
        in_specs=in_specs, out_specs=out_specs, out_shape=out_shape,
        compiler_params=_cp(("parallel",)),
        name="inproj",
    )(*args)


def _mla_cache_kernel(ckv_ref, kp_ref, wuk_ref, wuv_ref, place_ref, g128_ref, gains_ref, k_ref, v_ref):
    k, v = _mla_kv(ckv_ref[...].astype(BF16), kp_ref[...], wuk_ref, wuv_ref, place_ref, g128_ref,
                   gains_ref[5:6, :])
    k_ref[...] = k.astype(BF16)
    v_ref[...] = v.astype(BF16)


def _mla_cache(ckv2, kp2, W, tm=256):
    t = ckv2.shape[0]
    tok = lambda w: pl.BlockSpec((tm, w), lambda i: (i, 0))
    return pl.pallas_call(
        _mla_cache_kernel,
        grid=(t // tm,),
        in_specs=[tok(MLA_KV_LORA), tok(LANES), _const_spec(W["wuk"].shape), _const_spec(W["wuv"].shape),
                  _const_spec(W["place"].shape), _const_spec(W["g128"].shape),
                  _const_spec(W["gains"].shape)],
        out_specs=[tok(GW), tok(GW)],
        out_shape=[jax.ShapeDtypeStruct((t, GW), BF16)] * 2,
        compiler_params=_cp(("parallel",)),
        name="mla_cache_kv",
    )(ckv2, kp2, W["wuk"], W["wuv"], W["place"], W["g128"], W["gains"])


def _diff_lambda(lam_ref, lam_init):
    lf = lam_ref[...]
    return (jnp.exp(jnp.sum(lf[0:1] * lf[1:2], axis=-1, keepdims=True))
            - jnp.exp(jnp.sum(lf[2:3] * lf[3:4], axis=-1, keepdims=True)) + lam_init)


def _stream_queries(q, mode):
    if mode == "full":
        return [q]
    lane = lax.broadcasted_iota(jnp.int32, (1, LANES), 1)
    zero = jnp.zeros_like(q)
    return [jnp.where(lane < 64, q, zero), jnp.where(lane < 64, zero, q)]


def _combine_streams(outs, mode, lam, subg_ref, lam_init):
    if mode == "full":
        return outs[0]
    if mode == "pair":
        return outs[0] + outs[1]
    o = outs[0] - lam * outs[1]
    o = o * lax.rsqrt(jnp.mean(o * o, axis=-1, keepdims=True) + EPS) * subg_ref[...]
    return o * (1.0 - lam_init)


def _online_attend(q_ref, srcs, chunks, mode, lam, subg_ref, lam_init, o_ref):
    tq = q_ref.shape[1]
    nstream = 1 if mode == "full" else 2
    for slot in range(NSLOT):
        sl = slice(slot * LANES, (slot + 1) * LANES)
        qs = _stream_queries(q_ref[0, :, sl], mode)

        def step(k_c, v_cs, state):
            new = []
            for si in range(nstream):
                m, l, acc = state[3 * si:3 * si + 3]
                s = lax.dot_general(qs[si], k_c, (((1,), (1,)), ((), ())), preferred_element_type=F32)
                m_new = jnp.maximum(m, jnp.max(s, axis=-1, keepdims=True))
                alpha = jnp.exp2(m - m_new)
                p = jnp.exp2(s - m_new)
                l = alpha * l + jnp.sum(p, axis=-1, keepdims=True)
                acc = alpha * acc + jnp.dot(p.astype(BF16), v_cs[si % len(v_cs)],
                                            preferred_element_type=F32)
                new += [m_new, l, acc]
            return tuple(new)

        state = ()
        for _ in range(nstream):
            state += (jnp.full((tq, 1), NEG, F32), jnp.zeros((tq, 1), F32), jnp.zeros((tq, LANES), F32))
        for (k_ref, v_refs), (n_chunk, tk) in zip(srcs, chunks):
            if n_chunk == 1:
                state = step(k_ref[0, :, sl], [v[0, :, sl] for v in v_refs], state)
            else:
                def body(c, st, k_ref=k_ref, v_refs=v_refs, tk=tk):
                    rows = pl.ds(pl.multiple_of(c * tk, tk), tk)
                    return step(k_ref[0, rows, sl], [v[0, rows, sl] for v in v_refs], st)
                state = lax.fori_loop(0, n_chunk, body, state)

        outs = [state[3 * si + 2] / state[3 * si + 1] for si in range(nstream)]
        o_ref[0, :, sl] = _combine_streams(outs, mode, lam, subg_ref, lam_init).astype(o_ref.dtype)


def _attn_kernel(*refs, mode, chunks, lam_init):
    it = iter(refs)
    q_ref = next(it)
    srcs = []
    for _ in chunks:
        k_ref = next(it)
        v_refs = (next(it), next(it)) if mode == "pair" else (next(it),)
        srcs.append((k_ref, v_refs))
    lam, subg_ref = None, None
    if mode == "diff":
        lam = _diff_lambda(next(it), lam_init)
        subg_ref = next(it)
    o_ref = next(it)
    _online_attend(q_ref, srcs, chunks, mode, lam, subg_ref, lam_init, o_ref)


AUG = 2 * LANES
SHIFT_MARGIN = 1.0 + 2.0 ** -6
SHIFT_LIMIT = 48.0


def _attn_fast_kernel(*refs, mode, chunks, lam_init, tq_sub):
    it = iter(refs)
    q_ref, kc_ref, vc_ref, k_ref, v_ref = (next(it) for _ in range(5))
    lam, subg_ref = None, None
    if mode == "diff":
        lam = _diff_lambda(next(it), lam_init)
        subg_ref = next(it)
    o_ref, kaug_ref, vext_ref, kmax_ref, qaug_ref = (next(it) for _ in range(5))

    tq = q_ref.shape[1]
    pc, n = kc_ref.shape[1], k_ref.shape[1]
    kt = pc + n
    nstream = 1 if mode == "full" else 2
    lane = lax.broadcasted_iota(jnp.int32, (1, LANES), 1)

    @pl.when(pl.program_id(1) == 0)
    def _():
        unit = jnp.where(lane == 0, 1.0, 0.0).astype(BF16)
        for slot in range(NSLOT):
            sl = slice(slot * LANES, (slot + 1) * LANES)
            a0 = slot * AUG
            kaug_ref[0:pc, a0:a0 + LANES] = kc_ref[0, :, sl]
            kaug_ref[pc:kt, a0:a0 + LANES] = k_ref[0, :, sl]
            kaug_ref[:, a0 + LANES:a0 + AUG] = jnp.broadcast_to(unit, (kt, LANES))
            vext_ref[0:pc, a0:a0 + LANES] = vc_ref[0, :, sl]
            vext_ref[pc:kt, a0:a0 + LANES] = v_ref[0, :, sl]
            vext_ref[:, a0 + LANES:a0 + AUG] = jnp.ones((kt, LANES), BF16)
            for si in range(nstream):
                best = None
                for src in (kc_ref, k_ref):
                    kk = _stream_queries(src[0, :, sl], mode)[si].astype(F32)
                    nrm = jnp.max(jnp.sum(kk * kk, axis=-1, keepdims=True), axis=0, keepdims=True)
                    best = nrm if best is None else jnp.maximum(best, nrm)
                row = slot * nstream + si
                kmax_ref[row:row + 1, :] = jnp.broadcast_to(jnp.sqrt(best), (1, LANES))

    worst = jnp.zeros((1, 1), F32)
    for slot in range(NSLOT):
        sl = slice(slot * LANES, (slot + 1) * LANES)
        for si, qm in enumerate(_stream_queries(q_ref[0, :, sl], mode)):
            row = slot * nstream + si
            qf = qm.astype(F32)
            shift = (jnp.sqrt(jnp.sum(qf * qf, axis=-1, keepdims=True)) * kmax_ref[row:row + 1, 0:1]
                     * SHIFT_MARGIN)
            worst = jnp.maximum(worst, jnp.max(shift, axis=0, keepdims=True))
            qaug_ref[row, :, 0:LANES] = qm
            qaug_ref[row, :, LANES:AUG] = jnp.where(lane == 0, -shift, 0.0).astype(BF16)
    safe = worst[0, 0] <= SHIFT_LIMIT

    @pl.when(safe)
    def _():
        def sub_tile(t, carry):
            rows = pl.ds(pl.multiple_of(t * tq_sub, tq_sub), tq_sub)
            for slot in range(NSLOT):
                a0 = slot * AUG
                outs = []
                for si in range(nstream):
                    qa = qaug_ref[slot * nstream + si, rows, :]
                    s = lax.dot_general(qa, kaug_ref[:, a0:a0 + AUG], (((1,), (1,)), ((), ())),
                                        preferred_element_type=F32)
                    acc = jnp.dot(jnp.exp2(s).astype(BF16), vext_ref[:, a0:a0 + AUG],
                                  preferred_element_type=F32)
                    outs.append(acc[:, 0:LANES] / acc[:, LANES:AUG])
                o = _combine_streams(outs, mode, lam, subg_ref, lam_init)
                o_ref[0, rows, slot * LANES:(slot + 1) * LANES] = o.astype(o_ref.dtype)
            return carry
        lax.fori_loop(0, tq // tq_sub, sub_tile, 0)

    @pl.when(jnp.logical_not(safe))
    def _():
        _online_attend(q_ref, [(kc_ref, (vc_ref,)), (k_ref, (v_ref,))], chunks, mode, lam, subg_ref,
                       lam_init, o_ref)


def _attention_latent(mode, q, kc, vc, k, v, lam=None, subg=None, lam_init=0.0, tq=256, tq_sub=256, tk=512):
    b, n, _ = q.shape
    pc = kc.shape[1]
    kt = pc + n
    nstream = 1 if mode == "full" else 2
    res = lambda m: pl.BlockSpec((1, m, GW), lambda bi, qi: (bi, 0, 0))
    in_specs = [pl.BlockSpec((1, tq, GW), lambda bi, qi: (bi, qi, 0)), res(pc), res(pc), res(n), res(n)]
    args = [q, kc, vc, k, v]
    if mode == "diff":
        in_specs += [_const_spec(lam.shape), _const_spec(subg.shape)]
        args += [lam, subg]
    chunks = ((1, pc), (n // tk, tk))
    return pl.pallas_call(
        functools.partial(_attn_fast_kernel, mode=mode, chunks=chunks, lam_init=lam_init, tq_sub=tq_sub),
        grid=(b, n // tq),
        in_specs=in_specs,
        out_specs=pl.BlockSpec((1, tq, GW), lambda bi, qi: (bi, qi, 0)),
        out_shape=jax.ShapeDtypeStruct((b, n, GW), BF16),
        scratch_shapes=[pltpu.VMEM((kt, NSLOT * AUG), BF16), pltpu.VMEM((kt, NSLOT * AUG), BF16),
                        pltpu.VMEM((8, LANES), F32), pltpu.VMEM((NSLOT * nstream, tq, AUG), BF16)],
        compiler_params=_cp(("parallel", "arbitrary")),
        name="attn_latent_" + mode,
    )(*args)


def _attention(mode, q, srcs, tq, tk, lam=None, subg=None, lam_init=0.0):
    b, n, _ = q.shape
    in_specs = [pl.BlockSpec((1, tq, GW), lambda bi, qi: (bi, qi, 0))]
    args = [q]
    chunks = []
    for src in srcs:
        ks = src[0].shape[1]
        step = min(tk, ks)
        chunks.append((ks // step, step))
        for a in src:
            in_specs.append(pl.BlockSpec((1, ks, GW), lambda bi, qi: (bi, 0, 0)))
            args.append(a)
    if mode == "diff":
        in_specs += [_const_spec(lam.shape), _const_spec(subg.shape)]
        args += [lam, subg]
    return pl.pallas_call(
        functools.partial(_attn_kernel, mode=mode, chunks=tuple(chunks), lam_init=lam_init),
        grid=(b, n // tq),
        in_specs=in_specs,
        out_specs=pl.BlockSpec((1, tq, GW), lambda bi, qi: (bi, qi, 0)),
        out_shape=jax.ShapeDtypeStruct((b, n, GW), BF16),
        compiler_params=_cp(("parallel", "parallel")),
        name="attn_" + mode,
    )(*args)


def _na_row_start(r, rows):
    kr = min(NA_ROWS, rows)
    return jnp.clip(r - kr // 2, 0, rows - kr)


def _na_kernel(q_ref, k_ref, ve_ref, vo_ref, kc_ref, vec_ref, voc_ref, bias_ref, o_ref, *, rows, n_loc):
    r = pl.program_id(1)
    rs = _na_row_start(r, rows)
    loc = pl.ds(pl.multiple_of(rs * GRID_W, GRID_W), n_loc)
    lane = lax.broadcasted_iota(jnp.int32, (1, LANES), 1)
    for slot in range(NSLOT):
        sl = slice(slot * LANES, (slot + 1) * LANES)
        q = q_ref[0, :, sl]
        zero = jnp.zeros_like(q)
        k_loc = k_ref[0, loc, sl]
        k_ctx = kc_ref[0, :, sl]
        o = None
        for half in range(2):
            qh = jnp.where(lane < 64, q, zero) if half == 0 else jnp.where(lane < 64, zero, q)
            v_loc = (ve_ref if half == 0 else vo_ref)[0, loc, sl]
            v_ctx = (vec_ref if half == 0 else voc_ref)[0, :, sl]
            nt = (((1,), (1,)), ((), ()))
            s_loc = lax.dot_general(qh, k_loc, nt, preferred_element_type=F32) + bias_ref[0, 2 * slot + half]
            s_ctx = lax.dot_general(qh, k_ctx, nt, preferred_element_type=F32)
            m = jnp.maximum(jnp.max(s_loc, axis=-1, keepdims=True), jnp.max(s_ctx, axis=-1, keepdims=True))
            p_loc = jnp.exp2(s_loc - m)
            p_ctx = jnp.exp2(s_ctx - m)
            l = jnp.sum(p_loc, axis=-1, keepdims=True) + jnp.sum(p_ctx, axis=-1, keepdims=True)
            oh = (jnp.dot(p_loc.astype(BF16), v_loc, preferred_element_type=F32)
                  + jnp.dot(p_ctx.astype(BF16), v_ctx, preferred_element_type=F32)) / l
            o = oh if o is None else o + oh
        o_ref[0, :, sl] = o.astype(o_ref.dtype)


def _na_bias_table(bias_tab, rows):
    kr = min(NA_ROWS, rows)
    qc = np.arange(GRID_W)[:, None]
    kc = np.arange(GRID_W)[None, :]
    cstart = np.clip(qc - NA_COLS // 2, 0, GRID_W - NA_COLS)
    in_win = (kc >= cstart) & (kc < cstart + NA_COLS)
    dc_idx = np.clip(kc - qc, 1 - NA_COLS, NA_COLS - 1) + NA_COLS - 1
    onehot = (dc_idx[:, :, None] == np.arange(2 * NA_COLS - 1)).astype(np.float32)
    toep = jnp.einsum("qkd,hrd->hrqk", jnp.asarray(onehot), bias_tab.astype(F32),
                      precision=lax.Precision.HIGHEST) * LOG2E
    toep = jnp.where(in_win, toep, NEG)
    variants = []
    for var in range(kr):
        rows_v = toep[:, NA_ROWS - 1 - var:NA_ROWS - 1 - var + kr]
        variants.append(jnp.transpose(rows_v, (0, 2, 1, 3)).reshape(NA_H, GRID_W, kr * GRID_W))
    return jnp.stack(variants)


def _na_latent(q, k, ve, vo, kc, vec, voc, bias):
    b, n, _ = q.shape
    rows = n // GRID_W
    kr = min(NA_ROWS, rows)
    n_loc = kr * GRID_W
    pc = kc.shape[1]
    full = lambda m: pl.BlockSpec((1, m, GW), lambda bi, r: (bi, 0, 0))
    return pl.pallas_call(
        functools.partial(_na_kernel, rows=rows, n_loc=n_loc),
        grid=(b, rows),
        in_specs=[pl.BlockSpec((1, GRID_W, GW), lambda bi, r: (bi, r, 0)),
                  full(n), full(n), full(n), full(pc), full(pc), full(pc),
                  pl.BlockSpec((1, NA_H, GRID_W, n_loc),
                               lambda bi, r: (r - _na_row_start(r, rows), 0, 0, 0))],
        out_specs=pl.BlockSpec((1, GRID_W, GW), lambda bi, r: (bi, r, 0)),
        out_shape=jax.ShapeDtypeStruct((b, n, GW), BF16),
        compiler_params=_cp(("parallel", "parallel")),
        name="attn_neighbourhood",
    )(q, k, ve, vo, kc, vec, voc, bias)


POOL_HALO = 64


def _pool_kernel(prev_ref, main_ref, next_ref, w_ref, scale_ref, o_ref, *, seq):
    tp = main_ref.shape[1]
    t0 = pl.program_id(1) * tp
    ext = jnp.concatenate([prev_ref[0], main_ref[0], next_ref[0]], axis=0)
    ext_hi = ext.astype(BF16)
    ext_lo = (ext - ext_hi.astype(F32)).astype(BF16)
    t = t0 + lax.broadcasted_iota(jnp.int32, (tp, 1), 0)
    pos = t0 - POOL_HALO + lax.broadcasted_iota(jnp.int32, (1, tp + 2 * POOL_HALO), 1)
    for g, win in enumerate(POOL_WINDOWS):
        sl = slice(g * POOL_C, (g + 1) * POOL_C)
        lo = jnp.clip(t - win // 2, 0, seq)
        hi = jnp.clip(t - win // 2 + win, 0, seq)
        band = jnp.where((pos >= lo) & (pos < hi), 1.0, 0.0).astype(BF16)
        tot = (jnp.dot(band, ext_hi[:, sl], preferred_element_type=F32)
               + jnp.dot(band, ext_lo[:, sl], preferred_element_type=F32))
        mean = tot / (hi - lo).astype(F32)
        d = (mean - main_ref[0, :, sl]).astype(BF16)
        y = jnp.dot(d, w_ref[g], preferred_element_type=F32) * scale_ref[:, sl]
        o_ref[0, :, sl] = y.astype(o_ref.dtype)


def _pool(pu, w_pool, scale, tp=256):
    b, n, _ = pu.shape
    hb = tp // POOL_HALO
    last = n // POOL_HALO - 1
    return pl.pallas_call(
        functools.partial(_pool_kernel, seq=n),
        grid=(b, n // tp),
        in_specs=[pl.BlockSpec((1, POOL_HALO, GW), lambda bi, i: (bi, jnp.maximum(i * hb - 1, 0), 0)),
                  pl.BlockSpec((1, tp, GW), lambda bi, i: (bi, i, 0)),
                  pl.BlockSpec((1, POOL_HALO, GW), lambda bi, i: (bi, jnp.minimum((i + 1) * hb, last), 0)),
                  _const_spec(w_pool.shape), _const_spec(scale.shape)],
        out_specs=pl.BlockSpec((1, tp, GW), lambda bi, i: (bi, i, 0)),
        out_shape=jax.ShapeDtypeStruct((b, n, GW), BF16),
        compiler_params=_cp(("parallel", "parallel")),
        name="pool_mixer",
    )(pu, pu, pu, w_pool, scale)


def _outproj_kernel(oa_ref, ob_ref, oc_ref, od_ref, x_ref, mod_ref, w_ref, o_ref):
    mix = jnp.dot(oa_ref[...], w_ref[0:GW, :], preferred_element_type=F32)
    mix += jnp.dot(ob_ref[...], w_ref[GW:2 * GW, :], preferred_element_type=F32)
    mix += jnp.dot(oc_ref[...], w_ref[2 * GW:3 * GW, :], preferred_element_type=F32)
    mix += jnp.dot(od_ref[...], w_ref[3 * GW:4 * GW, :], preferred_element_type=F32)
    o_ref[...] = x_ref[...] + mod_ref[0, 2:3, :] * mix


def _outproj(oa, ob, oc, od, x2, mod, mod_row, w_out, tm=256):
    t, d = x2.shape
    tok = lambda w: pl.BlockSpec((tm, w), lambda i: (i, 0))
    return pl.pallas_call(
        _outproj_kernel,
        grid=(t // tm,),
        in_specs=[tok(GW)] * 4 + [tok(d), pl.BlockSpec((1, N_MOD, d), lambda i: (mod_row(i), 0, 0)),
                                   _const_spec(w_out.shape)],
        out_specs=tok(d),
        out_shape=jax.ShapeDtypeStruct((t, d), F32),
        compiler_params=_cp(("parallel",)),
        name="outproj",
    )(oa, ob, oc, od, x2, mod, w_out)


FFN_HALO = 16


def _ffn_kernel(prev_ref, x_ref, next_ref, mod_ref, g_ref, wg_ref, wv_ref, cwg_ref, cwv_ref, cbg_ref,
                cbv_ref, wd_ref, o_ref, h_ref, acc_ref, *, tiles_per_seq):
    i = pl.program_id(0)
    j = pl.program_id(1)
    tm = x_ref.shape[0]

    @pl.when(j == 0)
    def _():
        mod = mod_ref[0]
        g = g_ref[...]
        first = (i % tiles_per_seq) == 0
        last = (i % tiles_per_seq) == tiles_per_seq - 1
        hp = jnp.where(first, 0.0, _modulated(prev_ref[...], g, mod, 3))
        hn = jnp.where(last, 0.0, _modulated(next_ref[...], g, mod, 3))
        h_ref[0:FFN_HALO, :] = hp.astype(BF16)
        h_ref[FFN_HALO:FFN_HALO + tm, :] = _modulated(x_ref[...], g, mod, 3).astype(BF16)
        h_ref[FFN_HALO + tm:, :] = hn.astype(BF16)
        acc_ref[...] = jnp.zeros_like(acc_ref)

    h = h_ref[...]
    rows = tm + 2 * FFN_HALO

    def conv(w_ref, cw_ref, cb_ref):
        u = jnp.dot(h, w_ref[...], preferred_element_type=F32)
        cw = cw_ref[...]
        y = (pltpu.roll(u, 1, 0) * cw[0:1] + u * cw[1:2] + pltpu.roll(u, rows - 1, 0) * cw[2:3])
        return y[FFN_HALO:FFN_HALO + tm] + cb_ref[...]

    gate = conv(wg_ref, cwg_ref, cbg_ref)
    val = conv(wv_ref, cwv_ref, cbv_ref)
    a = (gate * jax.nn.sigmoid(gate) * val).astype(BF16)
    acc_ref[...] += jnp.dot(a, wd_ref[...], preferred_element_type=F32)

    @pl.when(j == pl.num_programs(1) - 1)
    def _():
        o_ref[...] = x_ref[...] + mod_ref[0, 5:6, :] * acc_ref[...]


def _ffn(x2, mod, mod_row, g2, w_up, conv_w, conv_b, w_down, seq, tm=512, tn=512):
    t, d = x2.shape
    dff = w_down.shape[0]
    tm = min(tm, seq)
    tiles_per_seq = seq // tm
    nj = dff // tn
    hb = tm // FFN_HALO
    nhb = t // FFN_HALO
    return pl.pallas_call(
        functools.partial(_ffn_kernel, tiles_per_seq=tiles_per_seq),
        grid=(t // tm, nj),
        in_specs=[pl.BlockSpec((FFN_HALO, d), lambda i, j: (jnp.maximum(i * hb - 1, 0), 0)),
                  pl.BlockSpec((tm, d), lambda i, j: (i, 0)),
                  pl.BlockSpec((FFN_HALO, d), lambda i, j: (jnp.minimum((i + 1) * hb, nhb - 1), 0)),
                  pl.BlockSpec((1, N_MOD, d), lambda i, j: (mod_row(i), 0, 0)),
                  _const_spec((1, d)),
                  pl.BlockSpec((d, tn), lambda i, j: (0, j)),
                  pl.BlockSpec((d, tn), lambda i, j: (0, nj + j)),
                  pl.BlockSpec((CONV_W, tn), lambda i, j: (0, j)),
                  pl.BlockSpec((CONV_W, tn), lambda i, j: (0, nj + j)),
                  pl.BlockSpec((1, tn), lambda i, j: (0, j)),
                  pl.BlockSpec((1, tn), lambda i, j: (0, nj + j)),
                  pl.BlockSpec((tn, d), lambda i, j: (j, 0))],
        out_specs=pl.BlockSpec((tm, d), lambda i, j: (i, 0)),
        out_shape=jax.ShapeDtypeStruct((t, d), F32),
        scratch_shapes=[pltpu.VMEM((tm + 2 * FFN_HALO, d), BF16), pltpu.VMEM((tm, d), F32)],
        compiler_params=_cp(("parallel", "arbitrary")),
        name="conv_ffn",
    )(x2, x2, x2, mod, g2, w_up, w_up, conv_w, conv_w, conv_b, conv_b, w_down)


def _block_ones(group):
    idx = np.arange(GW) // group
    return jnp.asarray(idx[:, None] == idx[None, :], dtype=BF16)


def _pad_heads(w, heads, width):
    lead = w.shape[:-1]
    w = w.reshape(lead + (heads, width))
    w = jnp.pad(w, [(0, 0)] * len(lead) + [(0, 0), (0, LANES - width)])
    return w.reshape(lead + (heads * LANES,))


def _layer_weights(P, l):
    d = P["w_in"].shape[1]
    w_in = P["w_in"][l]
    o = np.cumsum((0, 512, 512, 512, MLA_Q_LORA, MLA_KV_LORA, MLA_ROPE, 512, 512, 512, 512))
    col = lambda k: w_in[:, o[k]:o[k + 1]]
    w_all = jnp.concatenate(
        [col(0), col(1), col(2), col(6), col(7), col(8), col(9), col(3), col(4), col(5),
         jnp.zeros((d, ML_W - MLA_Q_LORA - MLA_KV_LORA - MLA_ROPE), F32)], axis=1).astype(BF16)
    wukv = P["mla_w_ukv"][l].reshape(MLA_KV_LORA, MLA_H, MLA_NOPE + MLA_V)
    wuk = _pad_heads(wukv[:, :, :MLA_NOPE].reshape(MLA_KV_LORA, MLA_H * MLA_NOPE), MLA_H, MLA_NOPE)
    wuv = wukv[:, :, MLA_NOPE:].reshape(MLA_KV_LORA, MLA_H * MLA_V)
    place = np.zeros((LANES, GW), np.float32)
    for hh in range(MLA_H):
        place[np.arange(MLA_ROPE), hh * LANES + MLA_NOPE + np.arange(MLA_ROPE)] = 1.0
    tile = lambda g, reps: jnp.tile(g.astype(F32), reps)
    padw = lambda g: jnp.pad(g.astype(F32), (0, GW - g.shape[0]))
    gains = jnp.stack([
        tile(P["diff_qn_g"][l], 8) * (DIFF_DH ** -0.5 * LOG2E),
        tile(P["diff_kn_g"][l], 8),
        tile(P["na_qn_g"][l], 8) * (NA_DH ** -0.5 * LOG2E),
        tile(P["na_kn_g"][l], 8),
        _pad_heads(tile(P["mla_qn_g"][l], MLA_H), MLA_H, MLA_QK) * (MLA_QK ** -0.5 * LOG2E),
        _pad_heads(tile(P["mla_kn_g"][l], MLA_H), MLA_H, MLA_QK),
        padw(P["mla_qa_g"][l]),
        padw(P["mla_kva_g"][l]),
    ])
    return dict(
        w_in=w_all,
        wuq=_pad_heads(P["mla_w_uq"][l], MLA_H, MLA_QK).astype(BF16),
        wuk=wuk.astype(BF16), wuv=wuv.astype(BF16),
        place=jnp.asarray(place, dtype=BF16),
        gains=gains, g64=_block_ones(64), g128=_block_ones(LANES),
        norm1=P["norm1_g"][l].reshape(1, d).astype(F32),
        norm2=P["norm2_g"][l].reshape(1, d).astype(F32),
        lam=P["diff_lam"][l].astype(F32),
        subg=P["diff_sub_g"][l].reshape(1, 2 * DIFF_DH).astype(F32),
        pool_w=P["pool_w"][l].astype(BF16),
        pool_scale=P["pool_scale"][l].reshape(1, GW).astype(F32),
        w_out=P["w_out"][l].astype(BF16),
        w_up=P["w_up"][l].astype(BF16),
        conv_w=P["conv_w"][l].astype(F32),
        conv_b=P["conv_b"][l].reshape(1, -1).astype(F32),
        w_down=P["w_down"][l].astype(BF16),
    )


def _rope_tables(n):
    t = jnp.arange(n)
    rows = (t // GRID_W).astype(F32)
    cols = (t % GRID_W).astype(F32)

    def ang(dim):
        quarter = dim // 4
        inv = ROPE_BASE ** (-jnp.arange(quarter, dtype=F32) / quarter)
        return jnp.concatenate([rows[:, None] * inv, cols[:, None] * inv], axis=-1)

    a = ang(DIFF_DH)
    cos_a, sin_a = jnp.cos(a), jnp.sin(a)
    cd = jnp.tile(cos_a, (1, 4))
    sd = jnp.tile(jnp.concatenate([-sin_a, sin_a], axis=1), (1, 2))
    b = ang(MLA_ROPE)
    cos_b, sin_b = jnp.cos(b), jnp.sin(b)
    ones = jnp.ones((n, MLA_NOPE), F32)
    pad = LANES - MLA_NOPE - MLA_ROPE
    cm = jnp.concatenate([ones, cos_b, cos_b, jnp.ones((n, pad), F32)], axis=1)
    sm = jnp.concatenate([0.0 * ones, -sin_b, sin_b, jnp.zeros((n, pad), F32)], axis=1)
    return jnp.stack([cd, sd, cm, sm])


def _layer(x, mod, mod_row_of_batch, W, layer, cache, rope_tab, bias_tab):
    b, n, d = x.shape
    t = b * n
    x2 = x.reshape(t, d)
    lam_init = 0.8 - 0.6 * math.exp(-0.3 * layer)
    tm = 256
    row_in = lambda i: mod_row_of_batch(i // (n // tm))
    outs = _inproj(x2, mod, row_in, W["norm1"], W, rope_tab, n, states=cache is None, tm=tm)
    qd, kd, vd, qn, kn, ven, von, pu, qm, km, vm = [a.reshape(b, n, GW) for a in outs[:11]]
    if cache is None:
        oa = _attention("diff", qd, [(kd, vd)], n, n, W["lam"], W["subg"], lam_init)
        ob = _attention("full", qm, [(km, vm)], n, n)
        oc = _attention("pair", qn, [(kn, ven, von)], n, n)
        state = outs[11:]
    else:
        a_k, a_v, b_ckv, b_kpe, c_k, c_v = cache
        pc = a_k.shape[1]
        ck = a_k.reshape(b, pc, GW).astype(BF16)
        cv = a_v.reshape(b, pc, GW).astype(BF16)
        oa = _attention_latent("diff", qd, ck, cv, kd, vd, W["lam"], W["subg"], lam_init)
        kpe_pad = jnp.pad(b_kpe.reshape(b * pc, MLA_ROPE), ((0, 0), (0, LANES - MLA_ROPE)))
        kmc, vmc = _mla_cache(b_ckv.reshape(b * pc, MLA_KV_LORA), kpe_pad, W)
        ob = _attention_latent("full", qm, kmc.reshape(b, pc, GW), vmc.reshape(b, pc, GW), km, vm)
        nk = c_k.reshape(b, pc, GW).astype(BF16)
        nv = c_v.reshape(b, pc, GW)
        even = (jnp.arange(GW) % LANES) < NA_DH
        nve = jnp.where(even, nv, 0.0).astype(BF16)
        nvo = jnp.where(even, 0.0, nv).astype(BF16)
        oc = _na_latent(qn, kn, ven, von, nk, nve, nvo, bias_tab)
        state = ()
    od = _pool(pu, W["pool_w"], W["pool_scale"])
    flat = lambda a: a.reshape(t, GW)
    x2 = _outproj(flat(oa), flat(ob), flat(oc), flat(od), x2, mod, row_in, W["w_out"], tm=tm)
    tmf = min(512, n)
    row_ffn = lambda i: mod_row_of_batch(i // (n // tmf))
    x2 = _ffn(x2, mod, row_ffn, W["norm2"], W["w_up"], W["conv_w"], W["conv_b"], W["w_down"], n, tm=tmf)
    return x2.reshape(b, n, d), state


def kernel(x_prompt, x_sample, cache_diff_k, cache_diff_v, cache_mla_ckv, cache_mla_kpe, cache_na_k, cache_na_v, c, c_ctx, norm1_g, norm2_g, ada_w, ada_b, w_in, diff_qn_g, diff_kn_g, diff_lam, diff_sub_g, mla_qa_g, mla_kva_g, mla_w_uq, mla_w_ukv, mla_qn_g, mla_kn_g, na_qn_g, na_kn_g, na_bias, pool_w, pool_scale, w_out, w_up, conv_w, conv_b, w_down):
    P = dict(norm1_g=norm1_g, norm2_g=norm2_g, w_in=w_in, diff_qn_g=diff_qn_g, diff_kn_g=diff_kn_g,
             diff_lam=diff_lam, diff_sub_g=diff_sub_g, mla_qa_g=mla_qa_g, mla_kva_g=mla_kva_g,
             mla_w_uq=mla_w_uq, mla_w_ukv=mla_w_ukv, mla_qn_g=mla_qn_g, mla_kn_g=mla_kn_g,
             na_qn_g=na_qn_g, na_kn_g=na_kn_g, pool_w=pool_w, pool_scale=pool_scale, w_out=w_out,
             w_up=w_up, conv_w=conv_w, conv_b=conv_b, w_down=w_down)
    depth = w_in.shape[0]
    d = x_prompt.shape[-1]
    bd, nd = x_sample.shape[:2]
    n_rows = -(-(bd + 1) // 8) * 8
    cv = jnp.concatenate([c, c_ctx[None, :], jnp.zeros((n_rows - bd - 1, d), F32)], axis=0)
    mod_all = _ada(cv, ada_w, ada_b).reshape(depth, n_rows, N_MOD, d)
    rope_tab = _rope_tables(nd)
    Ws = [_layer_weights(P, l) for l in range(depth)]

    xp = x_prompt
    states = []
    for l in range(depth):
        xp, st = _layer(xp, mod_all[l], lambda bi: bd, Ws[l], l, None, None, None)
        states.append(st)
    xs = x_sample
    for l in range(depth):
        cache_l = (cache_diff_k[:, l], cache_diff_v[:, l], cache_mla_ckv[:, l], cache_mla_kpe[:, l],
                   cache_na_k[:, l], cache_na_v[:, l])
        bias_tab = _na_bias_table(na_bias[l], nd // GRID_W)
        xs, _ = _layer(xs, mod_all[l], lambda bi: bi, Ws[l], l, cache_l, rope_tab, bias_tab)

    bp, npr = x_prompt.shape[:2]

    def stack(k, shape):
        return jnp.stack([s[k].reshape((bp, npr) + shape) for s in states], axis=1)

    return (xp, xs,
            stack(0, (DIFF_H, 2, DIFF_DH)), stack(1, (DIFF_H, 2 * DIFF_DH)),
            stack(2, (MLA_KV_LORA,)), stack(3, (MLA_ROPE,)),
            stack(4, (NA_H, NA_DH)), stack(5, (NA_H, NA_DH)))
```

```python
import functools
import math

import numpy as np
import jax
import jax.numpy as jnp
from jax import lax
from jax.experimental import pallas as pl
from jax.experimental.pallas import tpu as pltpu

F32 = jnp.float32
BF16 = jnp.bfloat16

GRID_W = 64
ROPE_BASE = 10000.0
EPS = 1e-6
N_MOD = 6
DIFF_H = 4
DIFF_DH = 64
MLA_H = 4
MLA_NOPE = 64
MLA_ROPE = 32
MLA_V = 128
MLA_Q_LORA = 384
MLA_KV_LORA = 128
NA_H = 8
NA_DH = 64
NA_ROWS = 8
NA_COLS = 16
POOL_WINDOWS = (2, 4, 8, 16)
POOL_G = 4
POOL_C = 128
CONV_W = 3

GW = 512
LANES = 128
NSLOT = GW // LANES
MLA_QK = MLA_NOPE + MLA_ROPE
V7X_VMEM_LIMIT = 56 * 1024 * 1024
NEG = -1e30
LOG2E = math.log2(math.e)

C_DQ, C_DK, C_DV, C_NQ, C_NK, C_NV, C_PU, C_ML = 0, 512, 1024, 1536, 2048, 2560, 3072, 3584
ML_W = 640
IN_WP = C_ML + ML_W


def _cp(sem, vmem=V7X_VMEM_LIMIT):
    return pltpu.CompilerParams(dimension_semantics=sem, vmem_limit_bytes=vmem)


def _const_spec(shape):
    nd = len(shape)
    return pl.BlockSpec(shape, lambda *_: (0,) * nd, pipeline_mode=pl.Buffered(1))


def _ada_kernel(c_ref, w_ref, b_ref, o_ref):
    c = c_ref[...]
    s = c * jax.nn.sigmoid(c)
    o_ref[0] = jnp.dot(s.astype(BF16), w_ref[0].astype(BF16), preferred_element_type=F32) + b_ref[0]


def _ada(cv, ada_w, ada_b, tn=1024):
    depth, d, n = ada_w.shape
    r = cv.shape[0]
    return pl.pallas_call(
        _ada_kernel,
        grid=(depth, n // tn),
        in_specs=[pl.BlockSpec((r, d), lambda l, j: (0, 0)),
                  pl.BlockSpec((1, d, tn), lambda l, j: (l, 0, j)),
                  pl.BlockSpec((1, 1, tn), lambda l, j: (l, 0, j))],
        out_specs=pl.BlockSpec((1, r, tn), lambda l, j: (l, 0, j)),
        out_shape=jax.ShapeDtypeStruct((depth, r, n), F32),
        compiler_params=_cp(("parallel", "parallel")),
        name="adaln",
    )(cv, ada_w, ada_b.reshape(depth, 1, n))


def _modulated(x, g, mod, k):
    ms = jnp.mean(x * x, axis=-1, keepdims=True)
    y = x * lax.rsqrt(ms + EPS) * g
    return y * (1.0 + mod[k + 1:k + 2, :]) + mod[k:k + 1, :]


def _group_rms(z, gsum_ref, inv_n, gain):
    ss = jnp.dot((z * z).astype(BF16), gsum_ref[...], preferred_element_type=F32)
    return z * lax.rsqrt(ss * inv_n + EPS) * gain


def _rope_chunks(z, cos_ref, sin_ref, lo_pred, shift_lo, shift_hi):
    c = cos_ref[...]
    s = sin_ref[...]
    outs = []
    for j in range(z.shape[1] // LANES):
        zc = z[:, j * LANES:(j + 1) * LANES]
        partner = jnp.where(lo_pred, pltpu.roll(zc, shift_lo, 1), pltpu.roll(zc, shift_hi, 1))
        outs.append(zc * c + partner * s)
    return jnp.concatenate(outs, axis=1)


def _mla_kv(ckv_b, kp, wuk_ref, wuv_ref, place_ref, g128_ref, kgain):
    kp_hi = kp.astype(BF16)
    kp_lo = (kp - kp_hi.astype(F32)).astype(BF16)
    kh = (jnp.dot(ckv_b, wuk_ref[...], preferred_element_type=F32)
          + jnp.dot(kp_hi, place_ref[...], preferred_element_type=F32)
          + jnp.dot(kp_lo, place_ref[...], preferred_element_type=F32))
    k = _group_rms(kh, g128_ref, 1.0 / MLA_QK, kgain)
    v = jnp.dot(ckv_b, wuv_ref[...], preferred_element_type=F32)
    return k, v


def _inproj_kernel(*refs, rope, states):
    it = iter(refs)
    x_ref, mod_ref, g_ref, w_ref, wuq_ref, wuk_ref, wuv_ref = (next(it) for _ in range(7))
    gains_ref, g64_ref, g128_ref, place_ref = (next(it) for _ in range(4))
    if rope:
        rope_ref = next(it)
    qd_ref, kd_ref, vd_ref, qn_ref, kn_ref, vn_ref, pu_ref, qm_ref, km_ref, vm_ref = (
        next(it) for _ in range(10))
    if states:
        ska_ref, sva_ref, sckv_ref, skpe_ref, skc_ref, svc_ref = (next(it) for _ in range(6))

    h = _modulated(x_ref[...], g_ref[...], mod_ref[0], 0).astype(BF16)
    gains = gains_ref[...]
    lane = lax.broadcasted_iota(jnp.int32, (1, LANES), 1)

    def seg(c0, w=GW):
        return jnp.dot(h, w_ref[:, c0:c0 + w], preferred_element_type=F32)

    def rope_d(z):
        if not rope:
            return z
        return _rope_chunks(z, rope_ref.at[0], rope_ref.at[1], (lane % 64) < 32, LANES - 32, 32)

    def rope_m(z):
        if not rope:
            return z
        return _rope_chunks(z, rope_ref.at[2], rope_ref.at[3], lane < MLA_NOPE + MLA_ROPE // 2,
                            LANES - MLA_ROPE // 2, MLA_ROPE // 2)

    qd_ref[...] = rope_d(_group_rms(seg(C_DQ), g64_ref, 1.0 / DIFF_DH, gains[0:1])).astype(BF16)
    kd = _group_rms(seg(C_DK), g64_ref, 1.0 / DIFF_DH, gains[1:2])
    kd_ref[...] = rope_d(kd).astype(BF16)
    vd = seg(C_DV)
    vd_ref[...] = vd.astype(BF16)
    qn_ref[...] = _group_rms(seg(C_NQ), g64_ref, 1.0 / NA_DH, gains[2:3]).astype(BF16)
    kn = _group_rms(seg(C_NK), g64_ref, 1.0 / NA_DH, gains[3:4])
    kn_ref[...] = kn.astype(BF16)
    vn = seg(C_NV)
    vn_ref[...] = vn.astype(BF16)
    pu_ref[...] = seg(C_PU)
    zm = seg(C_ML, ML_W)
    cq = zm[:, :MLA_Q_LORA]
    cq = cq * lax.rsqrt(jnp.mean(cq * cq, axis=-1, keepdims=True) + EPS) * gains[6:7, :MLA_Q_LORA]
    qh = jnp.dot(cq.astype(BF16), wuq_ref[...], preferred_element_type=F32)
    qm_ref[...] = rope_m(_group_rms(qh, g128_ref, 1.0 / MLA_QK, gains[4:5])).astype(BF16)
    ck = zm[:, MLA_Q_LORA:MLA_Q_LORA + MLA_KV_LORA]
    ckv = ck * lax.rsqrt(jnp.mean(ck * ck, axis=-1, keepdims=True) + EPS) * gains[7:8, :MLA_KV_LORA]
    kp = zm[:, GW:GW + LANES]
    km, vm = _mla_kv(ckv.astype(BF16), kp, wuk_ref, wuv_ref, place_ref, g128_ref, gains[5:6])
    km_ref[...] = rope_m(km).astype(BF16)
    vm_ref[...] = vm.astype(BF16)
    if states:
        ska_ref[...] = kd
        sva_ref[...] = vd
        sckv_ref[...] = ckv
        skpe_ref[...] = kp[:, :MLA_ROPE]
        skc_ref[...] = kn
        svc_ref[...] = vn


def _inproj(x2, mod, mod_row, g1, W, rope_tab, seq, states, tm=256):
    t, d = x2.shape
    rope = rope_tab is not None
    tiles_per_seq = seq // tm
    tok = lambda w: pl.BlockSpec((tm, w), lambda i: (i, 0))
    in_specs = [tok(d),
                pl.BlockSpec((1, N_MOD, d), lambda i: (mod_row(i), 0, 0)),
                _const_spec((1, d)),
                _const_spec(W["w_in"].shape), _const_spec(W["wuq"].shape),
                _const_spec(W["wuk"].shape), _const_spec(W["wuv"].shape),
                _const_spec(W["gains"].shape), _const_spec(W["g64"].shape),
                _const_spec(W["g128"].shape), _const_spec(W["place"].shape)]
    args = [x2, mod, g1, W["w_in"], W["wuq"], W["wuk"], W["wuv"], W["gains"], W["g64"], W["g128"],
            W["place"]]
    if rope:
        in_specs.append(pl.BlockSpec((4, tm, LANES), lambda i: (0, i % tiles_per_seq, 0)))
        args.append(rope_tab)
    out_specs = [tok(GW)] * 10
    out_shape = [jax.ShapeDtypeStruct((t, GW), BF16)] * 6 + [jax.ShapeDtypeStruct((t, GW), F32)] + \
                [jax.ShapeDtypeStruct((t, GW), BF16)] * 3
    if states:
        widths = (GW, GW, MLA_KV_LORA, MLA_ROPE, GW, GW)
        out_specs += [tok(w) for w in widths]
        out_shape += [jax.ShapeDtypeStruct((t, w), F32) for w in widths]
    return pl.pallas_call(
        functools.partial(_inproj_kernel, rope=rope, states=states),
        grid=(t // tm,),
        in_specs=in_specs, out_specs=out_specs, out_shape=out_shape,
        compiler_params=_cp(("parallel",)),
        name="inproj",
    )(*args)


def _mla_cache_kernel(ckv_ref, kp_ref, wuk_ref, wuv_ref, place_ref, g128_ref, gains_ref, k_ref, v_ref):
    k, v = _mla_kv(ckv_ref[...].astype(BF16), kp_ref[...], wuk_ref, wuv_ref, place_ref, g128_ref,
                   gains_ref[5:6, :])
    k_ref[...] = k.astype(BF16)
    v_ref[...] = v.astype(BF16)


def _mla_cache(ckv2, kp2, W, tm=256):
    t = ckv2.shape[0]
    tok = lambda w: pl.BlockSpec((tm, w), lambda i: (i, 0))
    return pl.pallas_call(
        _mla_cache_kernel,
        grid=(t // tm,),
        in_specs=[tok(MLA_KV_LORA), tok(LANES), _const_spec(W["wuk"].shape), _const_spec(W["wuv"].shape),
                  _const_spec(W["place"].shape), _const_spec(W["g128"].shape),
                  _const_spec(W["gains"].shape)],
        out_specs=[tok(GW), tok(GW)],
        out_shape=[jax.ShapeDtypeStruct((t, GW), BF16)] * 2,
        compiler_params=_cp(("parallel",)),
        name="mla_cache_kv",
    )(ckv2, kp2, W["wuk"], W["wuv"], W["place"], W["g128"], W["gains"])


def _diff_lambda(lam_ref, lam_init):
    lf = lam_ref[...]
    return (jnp.exp(jnp.sum(lf[0:1] * lf[1:2], axis=-1, keepdims=True))
            - jnp.exp(jnp.sum(lf[2:3] * lf[3:4], axis=-1, keepdims=True)) + lam_init)


def _stream_queries(q, mode):
    if mode == "full":
        return [q]
    lane = lax.broadcasted_iota(jnp.int32, (1, LANES), 1)
    zero = jnp.zeros_like(q)
    return [jnp.where(lane < 64, q, zero), jnp.where(lane < 64, zero, q)]


def _combine_streams(outs, mode, lam, subg_ref, lam_init):
    if mode == "full":
        return outs[0]
    if mode == "pair":
        lane = lax.broadcasted_iota(jnp.int32, (1, LANES), 1)
        return jnp.where(lane < 64, outs[0], outs[1])
    o = outs[0] - lam * outs[1]
    o = o * lax.rsqrt(jnp.mean(o * o, axis=-1, keepdims=True) + EPS) * subg_ref[...]
    return o * (1.0 - lam_init)


def _online_attend(q_ref, srcs, chunks, mode, lam, subg_ref, lam_init, o_ref):
    tq = q_ref.shape[1]
    nstream = 1 if mode == "full" else 2
    for slot in range(NSLOT):
        sl = slice(slot * LANES, (slot + 1) * LANES)
        qs = _stream_queries(q_ref[0, :, sl], mode)

        def step(k_c, v_cs, state):
            new = []
            for si in range(nstream):
                m, l, acc = state[3 * si:3 * si + 3]
                s = lax.dot_general(qs[si], k_c, (((1,), (1,)), ((), ())), preferred_element_type=F32)
                m_new = jnp.maximum(m, jnp.max(s, axis=-1, keepdims=True))
                alpha = jnp.exp2(m - m_new)
                p = jnp.exp2(s - m_new)
                l = alpha * l + jnp.sum(p, axis=-1, keepdims=True)
                acc = alpha * acc + jnp.dot(p.astype(BF16), v_cs[si % len(v_cs)],
                                            preferred_element_type=F32)
                new += [m_new, l, acc]
            return tuple(new)

        state = ()
        for _ in range(nstream):
            state += (jnp.full((tq, 1), NEG, F32), jnp.zeros((tq, 1), F32), jnp.zeros((tq, LANES), F32))
        for (k_ref, v_refs), (n_chunk, tk) in zip(srcs, chunks):
            if n_chunk == 1:
                state = step(k_ref[0, :, sl], [v[0, :, sl] for v in v_refs], state)
            else:
                def body(c, st, k_ref=k_ref, v_refs=v_refs, tk=tk):
                    rows = pl.ds(pl.multiple_of(c * tk, tk), tk)
                    return step(k_ref[0, rows, sl], [v[0, rows, sl] for v in v_refs], st)
                state = lax.fori_loop(0, n_chunk, body, state)

        outs = [state[3 * si + 2] / state[3 * si + 1] for si in range(nstream)]
        o_ref[0, :, sl] = _combine_streams(outs, mode, lam, subg_ref, lam_init).astype(o_ref.dtype)


def _attn_kernel(*refs, mode, chunks, lam_init):
    it = iter(refs)
    q_ref = next(it)
    srcs = []
    for _ in chunks:
        k_ref = next(it)
        srcs.append((k_ref, (next(it),)))
    lam, subg_ref = None, None
    if mode == "diff":
        lam = _diff_lambda(next(it), lam_init)
        subg_ref = next(it)
    o_ref = next(it)
    _online_attend(q_ref, srcs, chunks, mode, lam, subg_ref, lam_init, o_ref)


AUG = 2 * LANES
SHIFT_MARGIN = 1.0 + 2.0 ** -6
SHIFT_LIMIT = 48.0


def _attn_fast_kernel(*refs, mode, chunks, lam_init, tq_sub):
    it = iter(refs)
    q_ref, kc_ref, vc_ref, k_ref, v_ref = (next(it) for _ in range(5))
    lam, subg_ref = None, None
    if mode == "diff":
        lam = _diff_lambda(next(it), lam_init)
        subg_ref = next(it)
    o_ref, kaug_ref, vext_ref, kmax_ref, qaug_ref = (next(it) for _ in range(5))

    tq = q_ref.shape[1]
    pc, n = kc_ref.shape[1], k_ref.shape[1]
    kt = pc + n
    nstream = 1 if mode == "full" else 2
    lane = lax.broadcasted_iota(jnp.int32, (1, LANES), 1)

    @pl.when(pl.program_id(1) == 0)
    def _():
        unit = jnp.where(lane == 0, 1.0, 0.0).astype(BF16)
        for slot in range(NSLOT):
            sl = slice(slot * LANES, (slot + 1) * LANES)
            a0 = slot * AUG
            kaug_ref[0:pc, a0:a0 + LANES] = kc_ref[0, :, sl]
            kaug_ref[pc:kt, a0:a0 + LANES] = k_ref[0, :, sl]
            kaug_ref[:, a0 + LANES:a0 + AUG] = jnp.broadcast_to(unit, (kt, LANES))
            vext_ref[0:pc, a0:a0 + LANES] = vc_ref[0, :, sl]
            vext_ref[pc:kt, a0:a0 + LANES] = v_ref[0, :, sl]
            vext_ref[:, a0 + LANES:a0 + AUG] = jnp.ones((kt, LANES), BF16)
            for si in range(nstream):
                best = None
                for src in (kc_ref, k_ref):
                    kk = _stream_queries(src[0, :, sl], mode)[si].astype(F32)
                    nrm = jnp.max(jnp.sum(kk * kk, axis=-1, keepdims=True), axis=0, keepdims=True)
                    best = nrm if best is None else jnp.maximum(best, nrm)
                row = slot * nstream + si
                kmax_ref[row:row + 1, :] = jnp.broadcast_to(jnp.sqrt(best), (1, LANES))

    worst = jnp.zeros((1, 1), F32)
    for slot in range(NSLOT):
        sl = slice(slot * LANES, (slot + 1) * LANES)
        for si, qm in enumerate(_stream_queries(q_ref[0, :, sl], mode)):
            row = slot * nstream + si
            qf = qm.astype(F32)
            shift = (jnp.sqrt(jnp.sum(qf * qf, axis=-1, keepdims=True)) * kmax_ref[row:row + 1, 0:1]
                     * SHIFT_MARGIN)
            worst = jnp.maximum(worst, jnp.max(shift, axis=0, keepdims=True))
            qaug_ref[row, :, 0:LANES] = qm
            qaug_ref[row, :, LANES:AUG] = jnp.where(lane == 0, -shift, 0.0).astype(BF16)
    safe = worst[0, 0] <= SHIFT_LIMIT

    @pl.when(safe)
    def _():
        def sub_tile(t, carry):
            rows = pl.ds(pl.multiple_of(t * tq_sub, tq_sub), tq_sub)
            for slot in range(NSLOT):
                a0 = slot * AUG
                outs = []
                for si in range(nstream):
                    qa = qaug_ref[slot * nstream + si, rows, :]
                    s = lax.dot_general(qa, kaug_ref[:, a0:a0 + AUG], (((1,), (1,)), ((), ())),
                                        preferred_element_type=F32)
                    acc = jnp.dot(jnp.exp2(s).astype(BF16), vext_ref[:, a0:a0 + AUG],
                                  preferred_element_type=F32)
                    outs.append(acc[:, 0:LANES] / acc[:, LANES:AUG])
                o = _combine_streams(outs, mode, lam, subg_ref, lam_init)
                o_ref[0, rows, slot * LANES:(slot + 1) * LANES] = o.astype(o_ref.dtype)
            return carry
        lax.fori_loop(0, tq // tq_sub, sub_tile, 0)

    @pl.when(jnp.logical_not(safe))
    def _():
        _online_attend(q_ref, [(kc_ref, (vc_ref,)), (k_ref, (v_ref,))], chunks, mode, lam, subg_ref,
                       lam_init, o_ref)


def _attention_latent(mode, q, kc, vc, k, v, lam=None, subg=None, lam_init=0.0, tq=256, tq_sub=256, tk=512):
    b, n, _ = q.shape
    pc = kc.shape[1]
    kt = pc + n
    nstream = 1 if mode == "full" else 2
    res = lambda m: pl.BlockSpec((1, m, GW), lambda bi, qi: (bi, 0, 0))
    in_specs = [pl.BlockSpec((1, tq, GW), lambda bi, qi: (bi, qi, 0)), res(pc), res(pc), res(n), res(n)]
    args = [q, kc, vc, k, v]
    if mode == "diff":
        in_specs += [_const_spec(lam.shape), _const_spec(subg.shape)]
        args += [lam, subg]
    chunks = ((1, pc), (n // tk, tk))
    return pl.pallas_call(
        functools.partial(_attn_fast_kernel, mode=mode, chunks=chunks, lam_init=lam_init, tq_sub=tq_sub),
        grid=(b, n // tq),
        in_specs=in_specs,
        out_specs=pl.BlockSpec((1, tq, GW), lambda bi, qi: (bi, qi, 0)),
        out_shape=jax.ShapeDtypeStruct((b, n, GW), BF16),
        scratch_shapes=[pltpu.VMEM((kt, NSLOT * AUG), BF16), pltpu.VMEM((kt, NSLOT * AUG), BF16),
                        pltpu.VMEM((8, LANES), F32), pltpu.VMEM((NSLOT * nstream, tq, AUG), BF16)],
        compiler_params=_cp(("parallel", "arbitrary")),
        name="attn_latent_" + mode,
    )(*args)


def _attention(mode, q, srcs, tq, tk, lam=None, subg=None, lam_init=0.0):
    b, n, _ = q.shape
    in_specs = [pl.BlockSpec((1, tq, GW), lambda bi, qi: (bi, qi, 0))]
    args = [q]
    chunks = []
    for src in srcs:
        ks = src[0].shape[1]
        step = min(tk, ks)
        chunks.append((ks // step, step))
        for a in src:
            in_specs.append(pl.BlockSpec((1, ks, GW), lambda bi, qi: (bi, 0, 0)))
            args.append(a)
    if mode == "diff":
        in_specs += [_const_spec(lam.shape), _const_spec(subg.shape)]
        args += [lam, subg]
    return pl.pallas_call(
        functools.partial(_attn_kernel, mode=mode, chunks=tuple(chunks), lam_init=lam_init),
        grid=(b, n // tq),
        in_specs=in_specs,
        out_specs=pl.BlockSpec((1, tq, GW), lambda bi, qi: (bi, qi, 0)),
        out_shape=jax.ShapeDtypeStruct((b, n, GW), BF16),
        compiler_params=_cp(("parallel", "parallel")),
        name="attn_" + mode,
    )(*args)


NA_QROWS = 4
NA_KROWS = NA_QROWS + NA_ROWS


def _na_key_start(r0, rows):
    return jnp.clip(r0 - NA_ROWS // 2, 0, rows - NA_KROWS)


def _na_kernel(q_ref, k_ref, v_ref, kc_ref, vc_ref, bias_ref, o_ref, *, rows):
    r0 = pl.program_id(1) * NA_QROWS
    ks = _na_key_start(r0, rows)
    loc = pl.ds(pl.multiple_of(ks * GRID_W, GRID_W), NA_KROWS * GRID_W)
    lane = lax.broadcasted_iota(jnp.int32, (1, LANES), 1)
    nt = (((1,), (1,)), ((), ()))
    for slot in range(NSLOT):
        sl = slice(slot * LANES, (slot + 1) * LANES)
        q = q_ref[0, :, sl]
        k_loc = k_ref[0, loc, sl]
        k_ctx = kc_ref[0, :, sl]
        v_loc = v_ref[0, loc, sl]
        v_ctx = vc_ref[0, :, sl]
        v_loc = jnp.concatenate([v_loc, jnp.ones_like(v_loc)], axis=1)
        v_ctx = jnp.concatenate([v_ctx, jnp.ones_like(v_ctx)], axis=1)
        halves = []
        for half, qh in enumerate(_stream_queries(q, "pair")):
            s_loc = lax.dot_general(qh, k_loc, nt, preferred_element_type=F32) + bias_ref[0, 2 * slot + half]
            s_ctx = lax.dot_general(qh, k_ctx, nt, preferred_element_type=F32)
            m = jnp.maximum(jnp.max(s_loc, axis=-1, keepdims=True), jnp.max(s_ctx, axis=-1, keepdims=True))
            acc = (jnp.dot(jnp.exp2(s_loc - m).astype(BF16), v_loc, preferred_element_type=F32)
                   + jnp.dot(jnp.exp2(s_ctx - m).astype(BF16), v_ctx, preferred_element_type=F32))
            halves.append(acc[:, 0:LANES] / acc[:, LANES:2 * LANES])
        o_ref[0, :, sl] = jnp.where(lane < 64, halves[0], halves[1]).astype(o_ref.dtype)


def _na_bias_table(bias_tab, rows):
    assert rows >= NA_KROWS + NA_QROWS and rows % NA_QROWS == 0
    qc = np.arange(GRID_W)[:, None]
    kc = np.arange(GRID_W)[None, :]
    cstart = np.clip(qc - NA_COLS // 2, 0, GRID_W - NA_COLS)
    in_win = (kc >= cstart) & (kc < cstart + NA_COLS)
    dc_idx = np.clip(kc - qc, 1 - NA_COLS, NA_COLS - 1) + NA_COLS - 1
    onehot = (dc_idx[:, :, None] == np.arange(2 * NA_COLS - 1)).astype(np.float32)
    toep = jnp.einsum("qkd,hrd->hrqk", jnp.asarray(onehot), bias_tab.astype(F32),
                      precision=lax.Precision.HIGHEST) * LOG2E
    toep = jnp.where(in_win, toep, NEG)
    masked = jnp.full((NA_H, GRID_W, GRID_W), NEG, F32)
    variants = []
    for var in (0, NA_ROWS // 2, NA_ROWS):
        q_rows = []
        for j in range(NA_QROWS):
            first = min(max(j + var - NA_ROWS // 2, 0), NA_KROWS - NA_ROWS)
            pieces = [toep[:, i - j - var + NA_ROWS - 1] if first <= i < first + NA_ROWS else masked
                      for i in range(NA_KROWS)]
            q_rows.append(jnp.concatenate(pieces, axis=-1))
        variants.append(jnp.concatenate(q_rows, axis=1))
    return jnp.stack(variants)


def _na_latent(q, k, v, kc, vc, bias):
    b, n, _ = q.shape
    rows = n // GRID_W
    pc = kc.shape[1]
    full = lambda m: pl.BlockSpec((1, m, GW), lambda bi, r: (bi, 0, 0))
    tq = NA_QROWS * GRID_W
    return pl.pallas_call(
        functools.partial(_na_kernel, rows=rows),
        grid=(b, rows // NA_QROWS),
        in_specs=[pl.BlockSpec((1, tq, GW), lambda bi, r: (bi, r, 0)),
                  full(n), full(n), full(pc), full(pc),
                  pl.BlockSpec((1, NA_H, tq, NA_KROWS * GRID_W),
                               lambda bi, r: ((r * NA_QROWS - _na_key_start(r * NA_QROWS, rows))
                                              // (NA_ROWS // 2), 0, 0, 0))],
        out_specs=pl.BlockSpec((1, tq, GW), lambda bi, r: (bi, r, 0)),
        out_shape=jax.ShapeDtypeStruct((b, n, GW), BF16),
        compiler_params=_cp(("parallel", "parallel")),
        name="attn_neighbourhood",
    )(q, k, v, kc, vc, bias)


POOL_HALO = 64


def _pool_kernel(prev_ref, main_ref, next_ref, w_ref, scale_ref, o_ref, *, seq):
    tp = main_ref.shape[1]
    t0 = pl.program_id(1) * tp
    ext = jnp.concatenate([prev_ref[0], main_ref[0], next_ref[0]], axis=0)
    ext_hi = ext.astype(BF16)
    ext_lo = (ext - ext_hi.astype(F32)).astype(BF16)
    t = t0 + lax.broadcasted_iota(jnp.int32, (tp, 1), 0)
    pos = t0 - POOL_HALO + lax.broadcasted_iota(jnp.int32, (1, tp + 2 * POOL_HALO), 1)
    for g, win in enumerate(POOL_WINDOWS):
        sl = slice(g * POOL_C, (g + 1) * POOL_C)
        lo = jnp.clip(t - win // 2, 0, seq)
        hi = jnp.clip(t - win // 2 + win, 0, seq)
        band = jnp.where((pos >= lo) & (pos < hi), 1.0, 0.0).astype(BF16)
        tot = (jnp.dot(band, ext_hi[:, sl], preferred_element_type=F32)
               + jnp.dot(band, ext_lo[:, sl], preferred_element_type=F32))
        mean = tot / (hi - lo).astype(F32)
        d = (mean - main_ref[0, :, sl]).astype(BF16)
        y = jnp.dot(d, w_ref[g], preferred_element_type=F32) * scale_ref[:, sl]
        o_ref[0, :, sl] = y.astype(o_ref.dtype)


def _pool(pu, w_pool, scale, tp=256):
    b, n, _ = pu.shape
    hb = tp // POOL_HALO
    last = n // POOL_HALO - 1
    return pl.pallas_call(
        functools.partial(_pool_kernel, seq=n),
        grid=(b, n // tp),
        in_specs=[pl.BlockSpec((1, POOL_HALO, GW), lambda bi, i: (bi, jnp.maximum(i * hb - 1, 0), 0)),
                  pl.BlockSpec((1, tp, GW), lambda bi, i: (bi, i, 0)),
                  pl.BlockSpec((1, POOL_HALO, GW), lambda bi, i: (bi, jnp.minimum((i + 1) * hb, last), 0)),
                  _const_spec(w_pool.shape), _const_spec(scale.shape)],
        out_specs=pl.BlockSpec((1, tp, GW), lambda bi, i: (bi, i, 0)),
        out_shape=jax.ShapeDtypeStruct((b, n, GW), BF16),
        compiler_params=_cp(("parallel", "parallel")),
        name="pool_mixer",
    )(pu, pu, pu, w_pool, scale)


def _outproj_kernel(oa_ref, ob_ref, oc_ref, od_ref, x_ref, mod_ref, g2_ref, w_ref, o_ref, h_ref):
    mix = jnp.dot(oa_ref[...], w_ref[0:GW, :], preferred_element_type=F32)
    mix += jnp.dot(ob_ref[...], w_ref[GW:2 * GW, :], preferred_element_type=F32)
    mix += jnp.dot(oc_ref[...], w_ref[2 * GW:3 * GW, :], preferred_element_type=F32)
    mix += jnp.dot(od_ref[...], w_ref[3 * GW:4 * GW, :], preferred_element_type=F32)
    mod = mod_ref[0]
    x = x_ref[...] + mod[2:3, :] * mix
    o_ref[...] = x
    h_ref[...] = _modulated(x, g2_ref[...], mod, 3).astype(BF16)


def _outproj(oa, ob, oc, od, x2, mod, mod_row, g2, w_out, tm=256):
    t, d = x2.shape
    tok = lambda w: pl.BlockSpec((tm, w), lambda i: (i, 0))
    return pl.pallas_call(
        _outproj_kernel,
        grid=(t // tm,),
        in_specs=[tok(GW)] * 4 + [tok(d), pl.BlockSpec((1, N_MOD, d), lambda i: (mod_row(i), 0, 0)),
                                   _const_spec((1, d)), _const_spec(w_out.shape)],
        out_specs=[tok(d), tok(d)],
        out_shape=[jax.ShapeDtypeStruct((t, d), F32), jax.ShapeDtypeStruct((t, d), BF16)],
        compiler_params=_cp(("parallel",)),
        name="outproj",
    )(oa, ob, oc, od, x2, mod, g2, w_out)


FFN_HALO = 16


def _ffn_kernel(prev_ref, hm_ref, next_ref, x_ref, mod_ref, wg_ref, wv_ref, cwg_ref, cwv_ref, cbg_ref,
                cbv_ref, wd_ref, o_ref, h_ref, a_ref, *, tiles_per_seq, nj):
    i = pl.program_id(0)
    j = pl.program_id(1)
    tm = x_ref.shape[0]
    rows = tm + 2 * FFN_HALO

    def up(slot):
        h = h_ref[...]

        def conv(w_ref, cw_ref, cb_ref):
            u = jnp.dot(h, w_ref[...], preferred_element_type=F32)
            cw = cw_ref[...]
            y = (pltpu.roll(u, 1, 0) * cw[0:1] + u * cw[1:2] + pltpu.roll(u, rows - 1, 0) * cw[2:3])
            return y[FFN_HALO:FFN_HALO + tm] + cb_ref[...]

        gate = conv(wg_ref, cwg_ref, cbg_ref)
        val = conv(wv_ref, cwv_ref, cbv_ref)
        a_ref[slot] = (gate * jax.nn.sigmoid(gate) * val).astype(BF16)

    def down(slot):
        return jnp.dot(a_ref[slot], wd_ref[...], preferred_element_type=F32)

    @pl.when(j == 0)
    def _():
        first = (i % tiles_per_seq) == 0
        last = (i % tiles_per_seq) == tiles_per_seq - 1
        h_ref[0:FFN_HALO, :] = jnp.where(first, jnp.zeros_like(prev_ref), prev_ref[...])
        h_ref[FFN_HALO:FFN_HALO + tm, :] = hm_ref[...]
        h_ref[FFN_HALO + tm:, :] = jnp.where(last, jnp.zeros_like(next_ref), next_ref[...])
        o_ref[...] = jnp.zeros_like(o_ref)
        up(0)

    for slot in range(2):
        @pl.when(jnp.logical_and(jnp.logical_and(j > 0, j < nj), j % 2 == slot))
        def _(slot=slot):
            up(slot)
            o_ref[...] += down(1 - slot)

    @pl.when(j == nj)
    def _():
        slot = (nj - 1) % 2
        o_ref[...] = x_ref[...] + mod_ref[0, 5:6, :] * (o_ref[...] + down(slot))


def _ffn(x2, h2, mod, mod_row, w_up, conv_w, conv_b, w_down, seq, tm=512, tn=512):
    t, d = x2.shape
    dff = w_down.shape[0]
    tm = min(tm, seq)
    tiles_per_seq = seq // tm
    nj = dff // tn
    hb = tm // FFN_HALO
    nhb = t // FFN_HALO
    up_tile = lambda j: jnp.minimum(j, nj - 1)
    return pl.pallas_call(
        functools.partial(_ffn_kernel, tiles_per_seq=tiles_per_seq, nj=nj),
        grid=(t // tm, nj + 1),
        in_specs=[pl.BlockSpec((FFN_HALO, d), lambda i, j: (jnp.maximum(i * hb - 1, 0), 0)),
                  pl.BlockSpec((tm, d), lambda i, j: (i, 0)),
                  pl.BlockSpec((FFN_HALO, d), lambda i, j: (jnp.minimum((i + 1) * hb, nhb - 1), 0)),
                  pl.BlockSpec((tm, d), lambda i, j: (i, 0)),
                  pl.BlockSpec((1, N_MOD, d), lambda i, j: (mod_row(i), 0, 0)),
                  pl.BlockSpec((d, tn), lambda i, j: (0, up_tile(j))),
                  pl.BlockSpec((d, tn), lambda i, j: (0, nj + up_tile(j))),
                  pl.BlockSpec((CONV_W, tn), lambda i, j: (0, up_tile(j))),
                  pl.BlockSpec((CONV_W, tn), lambda i, j: (0, nj + up_tile(j))),
                  pl.BlockSpec((1, tn), lambda i, j: (0, up_tile(j))),
                  pl.BlockSpec((1, tn), lambda i, j: (0, nj + up_tile(j))),
                  pl.BlockSpec((tn, d), lambda i, j: (jnp.maximum(j - 1, 0), 0))],
        out_specs=pl.BlockSpec((tm, d), lambda i, j: (i, 0)),
        out_shape=jax.ShapeDtypeStruct((t, d), F32),
        scratch_shapes=[pltpu.VMEM((tm + 2 * FFN_HALO, d), BF16), pltpu.VMEM((2, tm, tn), BF16)],
        compiler_params=_cp(("parallel", "arbitrary")),
        name="conv_ffn",
    )(h2, h2, h2, x2, mod, w_up, w_up, conv_w, conv_w, conv_b, conv_b, w_down)


def _block_ones(group):
    idx = np.arange(GW) // group
    return jnp.asarray(idx[:, None] == idx[None, :], dtype=BF16)


def _pad_heads(w, heads, width):
    lead = w.shape[:-1]
    w = w.reshape(lead + (heads, width))
    w = jnp.pad(w, [(0, 0)] * len(lead) + [(0, 0), (0, LANES - width)])
    return w.reshape(lead + (heads * LANES,))


def _layer_weights(P, l):
    d = P["w_in"].shape[1]
    w_in = P["w_in"][l]
    o = np.cumsum((0, 512, 512, 512, MLA_Q_LORA, MLA_KV_LORA, MLA_ROPE, 512, 512, 512, 512))
    col = lambda k: w_in[:, o[k]:o[k + 1]]
    w_all = jnp.concatenate(
        [col(0), col(1), col(2), col(6), col(7), col(8), col(9), col(3), col(4), col(5),
         jnp.zeros((d, ML_W - MLA_Q_LORA - MLA_KV_LORA - MLA_ROPE), F32)], axis=1).astype(BF16)
    wukv = P["mla_w_ukv"][l].reshape(MLA_KV_LORA, MLA_H, MLA_NOPE + MLA_V)
    wuk = _pad_heads(wukv[:, :, :MLA_NOPE].reshape(MLA_KV_LORA, MLA_H * MLA_NOPE), MLA_H, MLA_NOPE)
    wuv = wukv[:, :, MLA_NOPE:].reshape(MLA_KV_LORA, MLA_H * MLA_V)
    place = np.zeros((LANES, GW), np.float32)
    for hh in range(MLA_H):
        place[np.arange(MLA_ROPE), hh * LANES + MLA_NOPE + np.arange(MLA_ROPE)] = 1.0
    tile = lambda g, reps: jnp.tile(g.astype(F32), reps)
    padw = lambda g: jnp.pad(g.astype(F32), (0, GW - g.shape[0]))
    gains = jnp.stack([
        tile(P["diff_qn_g"][l], 8) * (DIFF_DH ** -0.5 * LOG2E),
        tile(P["diff_kn_g"][l], 8),
        tile(P["na_qn_g"][l], 8) * (NA_DH ** -0.5 * LOG2E),
        tile(P["na_kn_g"][l], 8),
        _pad_heads(tile(P["mla_qn_g"][l], MLA_H), MLA_H, MLA_QK) * (MLA_QK ** -0.5 * LOG2E),
        _pad_heads(tile(P["mla_kn_g"][l], MLA_H), MLA_H, MLA_QK),
        padw(P["mla_qa_g"][l]),
        padw(P["mla_kva_g"][l]),
    ])
    return dict(
        w_in=w_all,
        wuq=_pad_heads(P["mla_w_uq"][l], MLA_H, MLA_QK).astype(BF16),
        wuk=wuk.astype(BF16), wuv=wuv.astype(BF16),
        place=jnp.asarray(place, dtype=BF16),
        gains=gains, g64=_block_ones(64), g128=_block_ones(LANES),
        norm1=P["norm1_g"][l].reshape(1, d).astype(F32),
        norm2=P["norm2_g"][l].reshape(1, d).astype(F32),
        lam=P["diff_lam"][l].astype(F32),
        subg=P["diff_sub_g"][l].reshape(1, 2 * DIFF_DH).astype(F32),
        pool_w=P["pool_w"][l].astype(BF16),
        pool_scale=P["pool_scale"][l].reshape(1, GW).astype(F32),
        w_out=P["w_out"][l].astype(BF16),
        w_up=P["w_up"][l].astype(BF16),
        conv_w=P["conv_w"][l].astype(F32),
        conv_b=P["conv_b"][l].reshape(1, -1).astype(F32),
        w_down=P["w_down"][l].astype(BF16),
    )


def _rope_tables(n):
    t = jnp.arange(n)
    rows = (t // GRID_W).astype(F32)
    cols = (t % GRID_W).astype(F32)

    def ang(dim):
        quarter = dim // 4
        inv = ROPE_BASE ** (-jnp.arange(quarter, dtype=F32) / quarter)
        return jnp.concatenate([rows[:, None] * inv, cols[:, None] * inv], axis=-1)

    a = ang(DIFF_DH)
    cos_a, sin_a = jnp.cos(a), jnp.sin(a)
    cd = jnp.tile(cos_a, (1, 4))
    sd = jnp.tile(jnp.concatenate([-sin_a, sin_a], axis=1), (1, 2))
    b = ang(MLA_ROPE)
    cos_b, sin_b = jnp.cos(b), jnp.sin(b)
    ones = jnp.ones((n, MLA_NOPE), F32)
    pad = LANES - MLA_NOPE - MLA_ROPE
    cm = jnp.concatenate([ones, cos_b, cos_b, jnp.ones((n, pad), F32)], axis=1)
    sm = jnp.concatenate([0.0 * ones, -sin_b, sin_b, jnp.zeros((n, pad), F32)], axis=1)
    return jnp.stack([cd, sd, cm, sm])


def _layer(x, mod, mod_row_of_batch, W, layer, cache, rope_tab, bias_tab):
    b, n, d = x.shape
    t = b * n
    x2 = x.reshape(t, d)
    lam_init = 0.8 - 0.6 * math.exp(-0.3 * layer)
    tm = min(512, n)
    row_in = lambda i: mod_row_of_batch(i // (n // tm))
    outs = _inproj(x2, mod, row_in, W["norm1"], W, rope_tab, n, states=cache is None, tm=tm)
    qd, kd, vd, qn, kn, vn, pu, qm, km, vm = [a.reshape(b, n, GW) for a in outs[:10]]
    if cache is None:
        oa = _attention("diff", qd, [(kd, vd)], n, n, W["lam"], W["subg"], lam_init)
        ob = _attention("full", qm, [(km, vm)], n, n)
        oc = _attention("pair", qn, [(kn, vn)], n, n)
        state = outs[10:]
    else:
        a_k, a_v, b_ckv, b_kpe, c_k, c_v = cache
        pc = a_k.shape[1]
        ck = a_k.reshape(b, pc, GW).astype(BF16)
        cv = a_v.reshape(b, pc, GW).astype(BF16)
        oa = _attention_latent("diff", qd, ck, cv, kd, vd, W["lam"], W["subg"], lam_init)
        kpe_pad = jnp.pad(b_kpe.reshape(b * pc, MLA_ROPE), ((0, 0), (0, LANES - MLA_ROPE)))
        kmc, vmc = _mla_cache(b_ckv.reshape(b * pc, MLA_KV_LORA), kpe_pad, W)
        ob = _attention_latent("full", qm, kmc.reshape(b, pc, GW), vmc.reshape(b, pc, GW), km, vm)
        nk = c_k.reshape(b, pc, GW).astype(BF16)
        nv = c_v.reshape(b, pc, GW).astype(BF16)
        oc = _na_latent(qn, kn, vn, nk, nv, bias_tab)
        state = ()
    od = _pool(pu, W["pool_w"], W["pool_scale"])
    flat = lambda a: a.reshape(t, GW)
    x2, h2 = _outproj(flat(oa), flat(ob), flat(oc), flat(od), x2, mod, row_in, W["norm2"], W["w_out"], tm=tm)
    tmf = min(512, n)
    row_ffn = lambda i: mod_row_of_batch(i // (n // tmf))
    x2 = _ffn(x2, h2, mod, row_ffn, W["w_up"], W["conv_w"], W["conv_b"], W["w_down"], n, tm=tmf)
    return x2.reshape(b, n, d), state


def kernel(x_prompt, x_sample, cache_diff_k, cache_diff_v, cache_mla_ckv, cache_mla_kpe, cache_na_k, cache_na_v, c, c_ctx, norm1_g, norm2_g, ada_w, ada_b, w_in, diff_qn_g, diff_kn_g, diff_lam, diff_sub_g, mla_qa_g, mla_kva_g, mla_w_uq, mla_w_ukv, mla_qn_g, mla_kn_g, na_qn_g, na_kn_g, na_bias, pool_w, pool_scale, w_out, w_up, conv_w, conv_b, w_down):
    P = dict(norm1_g=norm1_g, norm2_g=norm2_g, w_in=w_in, diff_qn_g=diff_qn_g, diff_kn_g=diff_kn_g,
             diff_lam=diff_lam, diff_sub_g=diff_sub_g, mla_qa_g=mla_qa_g, mla_kva_g=mla_kva_g,
             mla_w_uq=mla_w_uq, mla_w_ukv=mla_w_ukv, mla_qn_g=mla_qn_g, mla_kn_g=mla_kn_g,
             na_qn_g=na_qn_g, na_kn_g=na_kn_g, pool_w=pool_w, pool_scale=pool_scale, w_out=w_out,
             w_up=w_up, conv_w=conv_w, conv_b=conv_b, w_down=w_down)
    depth = w_in.shape[0]
    d = x_prompt.shape[-1]
    bd, nd = x_sample.shape[:2]
    n_rows = -(-(bd + 1) // 8) * 8
    cv = jnp.concatenate([c, c_ctx[None, :], jnp.zeros((n_rows - bd - 1, d), F32)], axis=0)
    mod_all = _ada(cv, ada_w, ada_b).reshape(depth, n_rows, N_MOD, d)
    rope_tab = _rope_tables(nd)
    Ws = [_layer_weights(P, l) for l in range(depth)]

    xp = x_prompt
    states = []
    for l in range(depth):
        xp, st = _layer(xp, mod_all[l], lambda bi: bd, Ws[l], l, None, None, None)
        states.append(st)
    xs = x_sample
    for l in range(depth):
        cache_l = (cache_diff_k[:, l], cache_diff_v[:, l], cache_mla_ckv[:, l], cache_mla_kpe[:, l],
                   cache_na_k[:, l], cache_na_v[:, l])
        bias_tab = _na_bias_table(na_bias[l], nd // GRID_W)
        xs, _ = _layer(xs, mod_all[l], lambda bi: bi, Ws[l], l, cache_l, rope_tab, bias_tab)

    bp, npr = x_prompt.shape[:2]

    def stack(k, shape):
        return jnp.stack([s[k].reshape((bp, npr) + shape) for s in states], axis=1)

    return (xp, xs,
            stack(0, (DIFF_H, 2, DIFF_DH)), stack(1, (DIFF_H, 2 * DIFF_DH)),
            stack(2, (MLA_KV_LORA,)), stack(3, (MLA_ROPE,)),
            stack(4, (NA_H, NA_DH)), stack(5, (NA_H, NA_DH)))
```

```python
import functools
import math

import numpy as np
import jax
import jax.numpy as jnp
from jax import lax
from jax.experimental import pallas as pl
from jax.experimental.pallas import tpu as pltpu

F32 = jnp.float32
BF16 = jnp.bfloat16

GRID_W = 64
ROPE_BASE = 10000.0
EPS = 1e-6
N_MOD = 6
DIFF_H = 4
DIFF_DH = 64
MLA_H = 4
MLA_NOPE = 64
MLA_ROPE = 32
MLA_V = 128
MLA_Q_LORA = 384
MLA_KV_LORA = 128
NA_H = 8
NA_DH = 64
NA_ROWS = 8
NA_COLS = 16
POOL_WINDOWS = (2, 4, 8, 16)
POOL_G = 4
POOL_C = 128
CONV_W = 3

GW = 512
LANES = 128
NSLOT = GW // LANES
MXU_DIM = 256
MLA_QK = MLA_NOPE + MLA_ROPE
V7X_VMEM_LIMIT = 56 * 1024 * 1024
NEG = -1e30
LOG2E = math.log2(math.e)

C_DQ, C_DK, C_DV, C_NQ, C_NK, C_NV, C_PU, C_ML = 0, 512, 1024, 1536, 2048, 2560, 3072, 3584
ML_W = 640
IN_WP = C_ML + ML_W


def _cp(sem, vmem=V7X_VMEM_LIMIT):
    return pltpu.CompilerParams(dimension_semantics=sem, vmem_limit_bytes=vmem)


def _const_spec(shape):
    nd = len(shape)
    return pl.BlockSpec(shape, lambda *_: (0,) * nd, pipeline_mode=pl.Buffered(1))


def _ada_kernel(c_ref, w_ref, b_ref, o_ref):
    c = c_ref[...]
    s = c * jax.nn.sigmoid(c)
    o_ref[0] = jnp.dot(s.astype(BF16), w_ref[0].astype(BF16), preferred_element_type=F32) + b_ref[0]


def _ada(cv, ada_w, ada_b, tn=1024):
    depth, d, n = ada_w.shape
    r = cv.shape[0]
    return pl.pallas_call(
        _ada_kernel,
        grid=(depth, n // tn),
        in_specs=[pl.BlockSpec((r, d), lambda l, j: (0, 0)),
                  pl.BlockSpec((1, d, tn), lambda l, j: (l, 0, j)),
                  pl.BlockSpec((1, 1, tn), lambda l, j: (l, 0, j))],
        out_specs=pl.BlockSpec((1, r, tn), lambda l, j: (l, 0, j)),
        out_shape=jax.ShapeDtypeStruct((depth, r, n), F32),
        compiler_params=_cp(("parallel", "parallel")),
        name="adaln",
    )(cv, ada_w, ada_b.reshape(depth, 1, n))


def _modulated(x, g, mod, k):
    ms = jnp.mean(x * x, axis=-1, keepdims=True)
    y = x * lax.rsqrt(ms + EPS) * g
    return y * (1.0 + mod[k + 1:k + 2, :]) + mod[k:k + 1, :]


def _group_rms(z, gsum_ref, inv_n, gain):
    sq = (z * z).astype(BF16)
    ss = jnp.concatenate(
        [jnp.dot(sq[:, c:c + MXU_DIM], gsum_ref[c:c + MXU_DIM, c:c + MXU_DIM], preferred_element_type=F32)
         for c in range(0, z.shape[1], MXU_DIM)], axis=1)
    return z * lax.rsqrt(ss * inv_n + EPS) * gain


def _rope_chunks(z, cos_ref, sin_ref, lo_pred, shift_lo, shift_hi):
    c = cos_ref[...]
    s = sin_ref[...]
    outs = []
    for j in range(z.shape[1] // LANES):
        zc = z[:, j * LANES:(j + 1) * LANES]
        partner = jnp.where(lo_pred, pltpu.roll(zc, shift_lo, 1), pltpu.roll(zc, shift_hi, 1))
        outs.append(zc * c + partner * s)
    return jnp.concatenate(outs, axis=1)


def _mla_kv(ckv_b, kp, wuk_ref, wuv_ref, place_ref, g128_ref, kgain):
    kp_hi = kp.astype(BF16)
    kp_lo = (kp - kp_hi.astype(F32)).astype(BF16)
    kh = (jnp.dot(ckv_b, wuk_ref[...], preferred_element_type=F32)
          + jnp.dot(kp_hi, place_ref[...], preferred_element_type=F32)
          + jnp.dot(kp_lo, place_ref[...], preferred_element_type=F32))
    k = _group_rms(kh, g128_ref, 1.0 / MLA_QK, kgain)
    v = jnp.dot(ckv_b, wuv_ref[...], preferred_element_type=F32)
    return k, v


def _inproj_kernel(*refs, rope, states):
    it = iter(refs)
    x_ref, mod_ref, g_ref, w_ref, wuq_ref, wuk_ref, wuv_ref = (next(it) for _ in range(7))
    gains_ref, g64_ref, g128_ref, place_ref = (next(it) for _ in range(4))
    if rope:
        rope_ref = next(it)
    qd_ref, kd_ref, vd_ref, qn_ref, kn_ref, vn_ref, pu_ref, qm_ref, km_ref, vm_ref = (
        next(it) for _ in range(10))
    if states:
        ska_ref, sva_ref, sckv_ref, skpe_ref, skc_ref, svc_ref = (next(it) for _ in range(6))

    h = _modulated(x_ref[...], g_ref[...], mod_ref[0], 0).astype(BF16)
    gains = gains_ref[...]
    lane = lax.broadcasted_iota(jnp.int32, (1, LANES), 1)

    def seg(c0, w=GW):
        return jnp.dot(h, w_ref[:, c0:c0 + w], preferred_element_type=F32)

    def rope_d(z):
        if not rope:
            return z
        return _rope_chunks(z, rope_ref.at[0], rope_ref.at[1], (lane % 64) < 32, LANES - 32, 32)

    def rope_m(z):
        if not rope:
            return z
        return _rope_chunks(z, rope_ref.at[2], rope_ref.at[3], lane < MLA_NOPE + MLA_ROPE // 2,
                            LANES - MLA_ROPE // 2, MLA_ROPE // 2)

    qd_ref[...] = rope_d(_group_rms(seg(C_DQ), g64_ref, 1.0 / DIFF_DH, gains[0:1])).astype(BF16)
    kd = _group_rms(seg(C_DK), g64_ref, 1.0 / DIFF_DH, gains[1:2])
    kd_ref[...] = rope_d(kd).astype(BF16)
    vd = seg(C_DV)
    vd_ref[...] = vd.astype(BF16)
    qn_ref[...] = _group_rms(seg(C_NQ), g64_ref, 1.0 / NA_DH, gains[2:3]).astype(BF16)
    kn = _group_rms(seg(C_NK), g64_ref, 1.0 / NA_DH, gains[3:4])
    kn_ref[...] = kn.astype(BF16)
    vn = seg(C_NV)
    vn_ref[...] = vn.astype(BF16)
    pu_ref[...] = seg(C_PU)
    zm = seg(C_ML, ML_W)
    cq = zm[:, :MLA_Q_LORA]
    cq = cq * lax.rsqrt(jnp.mean(cq * cq, axis=-1, keepdims=True) + EPS) * gains[6:7, :MLA_Q_LORA]
    qh = jnp.dot(cq.astype(BF16), wuq_ref[...], preferred_element_type=F32)
    qm_ref[...] = rope_m(_group_rms(qh, g128_ref, 1.0 / MLA_QK, gains[4:5])).astype(BF16)
    ck = zm[:, MLA_Q_LORA:MLA_Q_LORA + MLA_KV_LORA]
    ckv = ck * lax.rsqrt(jnp.mean(ck * ck, axis=-1, keepdims=True) + EPS) * gains[7:8, :MLA_KV_LORA]
    kp = zm[:, GW:GW + LANES]
    km, vm = _mla_kv(ckv.astype(BF16), kp, wuk_ref, wuv_ref, place_ref, g128_ref, gains[5:6])
    km_ref[...] = rope_m(km).astype(BF16)
    vm_ref[...] = vm.astype(BF16)
    if states:
        ska_ref[...] = kd
        sva_ref[...] = vd
        sckv_ref[...] = ckv
        skpe_ref[...] = kp[:, :MLA_ROPE]
        skc_ref[...] = kn
        svc_ref[...] = vn


def _inproj(x2, mod, mod_row, g1, W, rope_tab, seq, states, tm=256):
    t, d = x2.shape
    rope = rope_tab is not None
    tiles_per_seq = seq // tm
    tok = lambda w: pl.BlockSpec((tm, w), lambda i: (i, 0))
    in_specs = [tok(d),
                pl.BlockSpec((1, N_MOD, d), lambda i: (mod_row(i), 0, 0)),
                _const_spec((1, d)),
                _const_spec(W["w_in"].shape), _const_spec(W["wuq"].shape),
                _const_spec(W["wuk"].shape), _const_spec(W["wuv"].shape),
                _const_spec(W["gains"].shape), _const_spec(W["g64"].shape),
                _const_spec(W["g128"].shape), _const_spec(W["place"].shape)]
    args = [x2, mod, g1, W["w_in"], W["wuq"], W["wuk"], W["wuv"], W["gains"], W["g64"], W["g128"],
            W["place"]]
    if rope:
        in_specs.append(pl.BlockSpec((4, tm, LANES), lambda i: (0, i % tiles_per_seq, 0)))
        args.append(rope_tab)
    out_specs = [tok(GW)] * 10
    out_shape = [jax.ShapeDtypeStruct((t, GW), BF16)] * 6 + [jax.ShapeDtypeStruct((t, GW), F32)] + \
                [jax.ShapeDtypeStruct((t, GW), BF16)] * 3
    if states:
        widths = (GW, GW, MLA_KV_LORA, MLA_ROPE, GW, GW)
        out_specs += [tok(w) for w in widths]
        out_shape += [jax.ShapeDtypeStruct((t, w), F32) for w in widths]
    return pl.pallas_call(
        functools.partial(_inproj_kernel, rope=rope, states=states),
        grid=(t // tm,),
        in_specs=in_specs, out_specs=out_specs, out_shape=out_shape,
        compiler_params=_cp(("parallel",)),
        name="inproj",
    )(*args)


def _mla_cache_kernel(ckv_ref, kp_ref, wuk_ref, wuv_ref, place_ref, g128_ref, gains_ref, k_ref, v_ref):
    k, v = _mla_kv(ckv_ref[...].astype(BF16), kp_ref[...], wuk_ref, wuv_ref, place_ref, g128_ref,
                   gains_ref[5:6, :])
    k_ref[...] = k.astype(BF16)
    v_ref[...] = v.astype(BF16)


def _mla_cache(ckv2, kp2, W, tm=256):
    t = ckv2.shape[0]
    tok = lambda w: pl.BlockSpec((tm, w), lambda i: (i, 0))
    return pl.pallas_call(
        _mla_cache_kernel,
        grid=(t // tm,),
        in_specs=[tok(MLA_KV_LORA), tok(LANES), _const_spec(W["wuk"].shape), _const_spec(W["wuv"].shape),
                  _const_spec(W["place"].shape), _const_spec(W["g128"].shape),
                  _const_spec(W["gains"].shape)],
        out_specs=[tok(GW), tok(GW)],
        out_shape=[jax.ShapeDtypeStruct((t, GW), BF16)] * 2,
        compiler_params=_cp(("parallel",)),
        name="mla_cache_kv",
    )(ckv2, kp2, W["wuk"], W["wuv"], W["place"], W["g128"], W["gains"])


def _diff_lambda(lam_ref, lam_init):
    lf = lam_ref[...]
    return (jnp.exp(jnp.sum(lf[0:1] * lf[1:2], axis=-1, keepdims=True))
            - jnp.exp(jnp.sum(lf[2:3] * lf[3:4], axis=-1, keepdims=True)) + lam_init)


def _stream_queries(q, mode):
    if mode == "full":
        return [q]
    lane = lax.broadcasted_iota(jnp.int32, (1, LANES), 1)
    zero = jnp.zeros_like(q)
    return [jnp.where(lane < 64, q, zero), jnp.where(lane < 64, zero, q)]


def _combine_streams(outs, mode, lam, subg_ref, lam_init):
    if mode == "full":
        return outs[0]
    if mode == "pair":
        lane = lax.broadcasted_iota(jnp.int32, (1, LANES), 1)
        return jnp.where(lane < 64, outs[0], outs[1])
    o = outs[0] - lam * outs[1]
    o = o * lax.rsqrt(jnp.mean(o * o, axis=-1, keepdims=True) + EPS) * subg_ref[...]
    return o * (1.0 - lam_init)


def _online_attend(q_ref, srcs, chunks, mode, lam, subg_ref, lam_init, o_ref):
    tq = q_ref.shape[1]
    nstream = 1 if mode == "full" else 2
    for slot in range(NSLOT):
        sl = slice(slot * LANES, (slot + 1) * LANES)
        qs = _stream_queries(q_ref[0, :, sl], mode)

        def step(k_c, v_cs, state):
            new = []
            for si in range(nstream):
                m, l, acc = state[3 * si:3 * si + 3]
                s = lax.dot_general(qs[si], k_c, (((1,), (1,)), ((), ())), preferred_element_type=F32)
                m_new = jnp.maximum(m, jnp.max(s, axis=-1, keepdims=True))
                alpha = jnp.exp2(m - m_new)
                p = jnp.exp2(s - m_new)
                l = alpha * l + jnp.sum(p, axis=-1, keepdims=True)
                acc = alpha * acc + jnp.dot(p.astype(BF16), v_cs[si % len(v_cs)],
                                            preferred_element_type=F32)
                new += [m_new, l, acc]
            return tuple(new)

        state = ()
        for _ in range(nstream):
            state += (jnp.full((tq, 1), NEG, F32), jnp.zeros((tq, 1), F32), jnp.zeros((tq, LANES), F32))
        for (k_ref, v_refs), (n_chunk, tk) in zip(srcs, chunks):
            if n_chunk == 1:
                state = step(k_ref[0, :, sl], [v[0, :, sl] for v in v_refs], state)
            else:
                def body(c, st, k_ref=k_ref, v_refs=v_refs, tk=tk):
                    rows = pl.ds(pl.multiple_of(c * tk, tk), tk)
                    return step(k_ref[0, rows, sl], [v[0, rows, sl] for v in v_refs], st)
                state = lax.fori_loop(0, n_chunk, body, state)

        outs = [state[3 * si + 2] / state[3 * si + 1] for si in range(nstream)]
        o_ref[0, :, sl] = _combine_streams(outs, mode, lam, subg_ref, lam_init).astype(o_ref.dtype)


def _attn_kernel(*refs, mode, chunks, lam_init):
    it = iter(refs)
    q_ref = next(it)
    srcs = []
    for _ in chunks:
        k_ref = next(it)
        srcs.append((k_ref, (next(it),)))
    lam, subg_ref = None, None
    if mode == "diff":
        lam = _diff_lambda(next(it), lam_init)
        subg_ref = next(it)
    o_ref = next(it)
    _online_attend(q_ref, srcs, chunks, mode, lam, subg_ref, lam_init, o_ref)


AUG = 2 * LANES
SHIFT_MARGIN = 1.0 + 2.0 ** -6
SHIFT_LIMIT = 48.0


def _attn_fast_kernel(*refs, mode, chunks, lam_init, tq_sub):
    it = iter(refs)
    q_ref, kc_ref, vc_ref, k_ref, v_ref = (next(it) for _ in range(5))
    lam, subg_ref = None, None
    if mode == "diff":
        lam = _diff_lambda(next(it), lam_init)
        subg_ref = next(it)
    o_ref, kaug_ref, vext_ref, kmax_ref, qaug_ref = (next(it) for _ in range(5))

    tq = q_ref.shape[1]
    pc, n = kc_ref.shape[1], k_ref.shape[1]
    kt = pc + n
    nstream = 1 if mode == "full" else 2
    lane = lax.broadcasted_iota(jnp.int32, (1, LANES), 1)

    @pl.when(pl.program_id(1) == 0)
    def _():
        unit = jnp.where(lane == 0, 1.0, 0.0).astype(BF16)
        for slot in range(NSLOT):
            sl = slice(slot * LANES, (slot + 1) * LANES)
            a0 = slot * AUG
            kaug_ref[0:pc, a0:a0 + LANES] = kc_ref[0, :, sl]
            kaug_ref[pc:kt, a0:a0 + LANES] = k_ref[0, :, sl]
            kaug_ref[:, a0 + LANES:a0 + AUG] = jnp.broadcast_to(unit, (kt, LANES))
            vext_ref[0:pc, a0:a0 + LANES] = vc_ref[0, :, sl]
            vext_ref[pc:kt, a0:a0 + LANES] = v_ref[0, :, sl]
            vext_ref[:, a0 + LANES:a0 + AUG] = jnp.ones((kt, LANES), BF16)
            for si in range(nstream):
                best = None
                for src in (kc_ref, k_ref):
                    kk = _stream_queries(src[0, :, sl], mode)[si].astype(F32)
                    nrm = jnp.max(jnp.sum(kk * kk, axis=-1, keepdims=True), axis=0, keepdims=True)
                    best = nrm if best is None else jnp.maximum(best, nrm)
                row = slot * nstream + si
                kmax_ref[row:row + 1, :] = jnp.broadcast_to(jnp.sqrt(best), (1, LANES))

    worst = jnp.zeros((1, 1), F32)
    for slot in range(NSLOT):
        sl = slice(slot * LANES, (slot + 1) * LANES)
        for si, qm in enumerate(_stream_queries(q_ref[0, :, sl], mode)):
            row = slot * nstream + si
            qf = qm.astype(F32)
            shift = (jnp.sqrt(jnp.sum(qf * qf, axis=-1, keepdims=True)) * kmax_ref[row:row + 1, 0:1]
                     * SHIFT_MARGIN)
            worst = jnp.maximum(worst, jnp.max(shift, axis=0, keepdims=True))
            qaug_ref[row, :, 0:LANES] = qm
            qaug_ref[row, :, LANES:AUG] = jnp.where(lane == 0, -shift, 0.0).astype(BF16)
    safe = worst[0, 0] <= SHIFT_LIMIT

    @pl.when(safe)
    def _():
        def sub_tile(t, carry):
            rows = pl.ds(pl.multiple_of(t * tq_sub, tq_sub), tq_sub)
            for slot in range(NSLOT):
                a0 = slot * AUG
                outs = []
                for si in range(nstream):
                    qa = qaug_ref[slot * nstream + si, rows, :]
                    s = lax.dot_general(qa, kaug_ref[:, a0:a0 + AUG], (((1,), (1,)), ((), ())),
                                        preferred_element_type=F32)
                    acc = jnp.dot(jnp.exp2(s).astype(BF16), vext_ref[:, a0:a0 + AUG],
                                  preferred_element_type=F32)
                    outs.append(acc[:, 0:LANES] / acc[:, LANES:AUG])
                o = _combine_streams(outs, mode, lam, subg_ref, lam_init)
                o_ref[0, rows, slot * LANES:(slot + 1) * LANES] = o.astype(o_ref.dtype)
            return carry
        lax.fori_loop(0, tq // tq_sub, sub_tile, 0)

    @pl.when(jnp.logical_not(safe))
    def _():
        _online_attend(q_ref, [(kc_ref, (vc_ref,)), (k_ref, (v_ref,))], chunks, mode, lam, subg_ref,
                       lam_init, o_ref)


def _attention_latent(mode, q, kc, vc, k, v, lam=None, subg=None, lam_init=0.0, tq=256, tq_sub=256, tk=512):
    b, n, _ = q.shape
    pc = kc.shape[1]
    kt = pc + n
    nstream = 1 if mode == "full" else 2
    res = lambda m: pl.BlockSpec((1, m, GW), lambda bi, qi: (bi, 0, 0))
    in_specs = [pl.BlockSpec((1, tq, GW), lambda bi, qi: (bi, qi, 0)), res(pc), res(pc), res(n), res(n)]
    args = [q, kc, vc, k, v]
    if mode == "diff":
        in_specs += [_const_spec(lam.shape), _const_spec(subg.shape)]
        args += [lam, subg]
    chunks = ((1, pc), (n // tk, tk))
    return pl.pallas_call(
        functools.partial(_attn_fast_kernel, mode=mode, chunks=chunks, lam_init=lam_init, tq_sub=tq_sub),
        grid=(b, n // tq),
        in_specs=in_specs,
        out_specs=pl.BlockSpec((1, tq, GW), lambda bi, qi: (bi, qi, 0)),
        out_shape=jax.ShapeDtypeStruct((b, n, GW), BF16),
        scratch_shapes=[pltpu.VMEM((kt, NSLOT * AUG), BF16), pltpu.VMEM((kt, NSLOT * AUG), BF16),
                        pltpu.VMEM((8, LANES), F32), pltpu.VMEM((NSLOT * nstream, tq, AUG), BF16)],
        compiler_params=_cp(("parallel", "arbitrary")),
        name="attn_latent_" + mode,
    )(*args)


def _attention(mode, q, srcs, tq, tk, lam=None, subg=None, lam_init=0.0):
    b, n, _ = q.shape
    in_specs = [pl.BlockSpec((1, tq, GW), lambda bi, qi: (bi, qi, 0))]
    args = [q]
    chunks = []
    for src in srcs:
        ks = src[0].shape[1]
        step = min(tk, ks)
        chunks.append((ks // step, step))
        for a in src:
            in_specs.append(pl.BlockSpec((1, ks, GW), lambda bi, qi: (bi, 0, 0)))
            args.append(a)
    if mode == "diff":
        in_specs += [_const_spec(lam.shape), _const_spec(subg.shape)]
        args += [lam, subg]
    return pl.pallas_call(
        functools.partial(_attn_kernel, mode=mode, chunks=tuple(chunks), lam_init=lam_init),
        grid=(b, n // tq),
        in_specs=in_specs,
        out_specs=pl.BlockSpec((1, tq, GW), lambda bi, qi: (bi, qi, 0)),
        out_shape=jax.ShapeDtypeStruct((b, n, GW), BF16),
        compiler_params=_cp(("parallel", "parallel")),
        name="attn_" + mode,
    )(*args)


NA_QROWS = 4
NA_KROWS = NA_QROWS + NA_ROWS


def _na_key_start(r0, rows):
    return jnp.clip(r0 - NA_ROWS // 2, 0, rows - NA_KROWS)


def _na_kernel(q_ref, k_ref, v_ref, kc_ref, vc_ref, bias_ref, o_ref, *, rows):
    r0 = pl.program_id(1) * NA_QROWS
    ks = _na_key_start(r0, rows)
    loc = pl.ds(pl.multiple_of(ks * GRID_W, GRID_W), NA_KROWS * GRID_W)
    lane = lax.broadcasted_iota(jnp.int32, (1, LANES), 1)
    nt = (((1,), (1,)), ((), ()))
    for slot in range(NSLOT):
        sl = slice(slot * LANES, (slot + 1) * LANES)
        q = q_ref[0, :, sl]
        k_loc = k_ref[0, loc, sl]
        k_ctx = kc_ref[0, :, sl]
        v_loc = v_ref[0, loc, sl]
        v_ctx = vc_ref[0, :, sl]
        v_loc = jnp.concatenate([v_loc, jnp.ones_like(v_loc)], axis=1)
        v_ctx = jnp.concatenate([v_ctx, jnp.ones_like(v_ctx)], axis=1)
        halves = []
        for half, qh in enumerate(_stream_queries(q, "pair")):
            s_loc = lax.dot_general(qh, k_loc, nt, preferred_element_type=F32) + bias_ref[0, 2 * slot + half]
            s_ctx = lax.dot_general(qh, k_ctx, nt, preferred_element_type=F32)
            m = jnp.maximum(jnp.max(s_loc, axis=-1, keepdims=True), jnp.max(s_ctx, axis=-1, keepdims=True))
            acc = (jnp.dot(jnp.exp2(s_loc - m).astype(BF16), v_loc, preferred_element_type=F32)
                   + jnp.dot(jnp.exp2(s_ctx - m).astype(BF16), v_ctx, preferred_element_type=F32))
            halves.append(acc[:, 0:LANES] / acc[:, LANES:2 * LANES])
        o_ref[0, :, sl] = jnp.where(lane < 64, halves[0], halves[1]).astype(o_ref.dtype)


def _na_bias_table(bias_tab, rows):
    assert rows >= NA_KROWS + NA_QROWS and rows % NA_QROWS == 0
    qc = np.arange(GRID_W)[:, None]
    kc = np.arange(GRID_W)[None, :]
    cstart = np.clip(qc - NA_COLS // 2, 0, GRID_W - NA_COLS)
    in_win = (kc >= cstart) & (kc < cstart + NA_COLS)
    dc_idx = np.clip(kc - qc, 1 - NA_COLS, NA_COLS - 1) + NA_COLS - 1
    onehot = (dc_idx[:, :, None] == np.arange(2 * NA_COLS - 1)).astype(np.float32)
    toep = jnp.einsum("qkd,hrd->hrqk", jnp.asarray(onehot), bias_tab.astype(F32),
                      precision=lax.Precision.HIGHEST) * LOG2E
    toep = jnp.where(in_win, toep, NEG)
    masked = jnp.full((NA_H, GRID_W, GRID_W), NEG, F32)
    variants = []
    for var in (0, NA_ROWS // 2, NA_ROWS):
        q_rows = []
        for j in range(NA_QROWS):
            first = min(max(j + var - NA_ROWS // 2, 0), NA_KROWS - NA_ROWS)
            pieces = [toep[:, i - j - var + NA_ROWS - 1] if first <= i < first + NA_ROWS else masked
                      for i in range(NA_KROWS)]
            q_rows.append(jnp.concatenate(pieces, axis=-1))
        variants.append(jnp.concatenate(q_rows, axis=1))
    return jnp.stack(variants)


def _na_latent(q, k, v, kc, vc, bias):
    b, n, _ = q.shape
    rows = n // GRID_W
    pc = kc.shape[1]
    full = lambda m: pl.BlockSpec((1, m, GW), lambda bi, r: (bi, 0, 0))
    tq = NA_QROWS * GRID_W
    return pl.pallas_call(
        functools.partial(_na_kernel, rows=rows),
        grid=(b, rows // NA_QROWS),
        in_specs=[pl.BlockSpec((1, tq, GW), lambda bi, r: (bi, r, 0)),
                  full(n), full(n), full(pc), full(pc),
                  pl.BlockSpec((1, NA_H, tq, NA_KROWS * GRID_W),
                               lambda bi, r: ((r * NA_QROWS - _na_key_start(r * NA_QROWS, rows))
                                              // (NA_ROWS // 2), 0, 0, 0))],
        out_specs=pl.BlockSpec((1, tq, GW), lambda bi, r: (bi, r, 0)),
        out_shape=jax.ShapeDtypeStruct((b, n, GW), BF16),
        compiler_params=_cp(("parallel", "parallel")),
        name="attn_neighbourhood",
    )(q, k, v, kc, vc, bias)


POOL_HALO = 64


def _pool_kernel(prev_ref, main_ref, next_ref, w_ref, scale_ref, o_ref, *, seq):
    tp = main_ref.shape[1]
    t0 = pl.program_id(1) * tp
    ext = jnp.concatenate([prev_ref[0], main_ref[0], next_ref[0]], axis=0)
    ext_hi = ext.astype(BF16)
    ext_lo = (ext - ext_hi.astype(F32)).astype(BF16)
    t = t0 + lax.broadcasted_iota(jnp.int32, (tp, 1), 0)
    pos = t0 - POOL_HALO + lax.broadcasted_iota(jnp.int32, (1, tp + 2 * POOL_HALO), 1)
    for g, win in enumerate(POOL_WINDOWS):
        sl = slice(g * POOL_C, (g + 1) * POOL_C)
        lo = jnp.clip(t - win // 2, 0, seq)
        hi = jnp.clip(t - win // 2 + win, 0, seq)
        band = jnp.where((pos >= lo) & (pos < hi), 1.0, 0.0).astype(BF16)
        tot = (jnp.dot(band, ext_hi[:, sl], preferred_element_type=F32)
               + jnp.dot(band, ext_lo[:, sl], preferred_element_type=F32))
        mean = tot / (hi - lo).astype(F32)
        d = (mean - main_ref[0, :, sl]).astype(BF16)
        y = jnp.dot(d, w_ref[g], preferred_element_type=F32) * scale_ref[:, sl]
        o_ref[0, :, sl] = y.astype(o_ref.dtype)


def _pool(pu, w_pool, scale, tp=256):
    b, n, _ = pu.shape
    hb = tp // POOL_HALO
    last = n // POOL_HALO - 1
    return pl.pallas_call(
        functools.partial(_pool_kernel, seq=n),
        grid=(b, n // tp),
        in_specs=[pl.BlockSpec((1, POOL_HALO, GW), lambda bi, i: (bi, jnp.maximum(i * hb - 1, 0), 0)),
                  pl.BlockSpec((1, tp, GW), lambda bi, i: (bi, i, 0)),
                  pl.BlockSpec((1, POOL_HALO, GW), lambda bi, i: (bi, jnp.minimum((i + 1) * hb, last), 0)),
                  _const_spec(w_pool.shape), _const_spec(scale.shape)],
        out_specs=pl.BlockSpec((1, tp, GW), lambda bi, i: (bi, i, 0)),
        out_shape=jax.ShapeDtypeStruct((b, n, GW), BF16),
        compiler_params=_cp(("parallel", "parallel")),
        name="pool_mixer",
    )(pu, pu, pu, w_pool, scale)


def _outproj_kernel(oa_ref, ob_ref, oc_ref, od_ref, x_ref, mod_ref, g2_ref, w_ref, o_ref, h_ref):
    mix = jnp.dot(oa_ref[...], w_ref[0:GW, :], preferred_element_type=F32)
    mix += jnp.dot(ob_ref[...], w_ref[GW:2 * GW, :], preferred_element_type=F32)
    mix += jnp.dot(oc_ref[...], w_ref[2 * GW:3 * GW, :], preferred_element_type=F32)
    mix += jnp.dot(od_ref[...], w_ref[3 * GW:4 * GW, :], preferred_element_type=F32)
    mod = mod_ref[0]
    x = x_ref[...] + mod[2:3, :] * mix
    o_ref[...] = x
    h_ref[...] = _modulated(x, g2_ref[...], mod, 3).astype(BF16)


def _outproj(oa, ob, oc, od, x2, mod, mod_row, g2, w_out, tm=256):
    t, d = x2.shape
    tok = lambda w: pl.BlockSpec((tm, w), lambda i: (i, 0))
    return pl.pallas_call(
        _outproj_kernel,
        grid=(t // tm,),
        in_specs=[tok(GW)] * 4 + [tok(d), pl.BlockSpec((1, N_MOD, d), lambda i: (mod_row(i), 0, 0)),
                                   _const_spec((1, d)), _const_spec(w_out.shape)],
        out_specs=[tok(d), tok(d)],
        out_shape=[jax.ShapeDtypeStruct((t, d), F32), jax.ShapeDtypeStruct((t, d), BF16)],
        compiler_params=_cp(("parallel",)),
        name="outproj",
    )(oa, ob, oc, od, x2, mod, g2, w_out)


FFN_HALO = 16


def _ffn_kernel(prev_ref, hm_ref, next_ref, x_ref, mod_ref, wg_ref, wv_ref, cwg_ref, cwv_ref, cbg_ref,
                cbv_ref, wd_ref, o_ref, h_ref, *, seq):
    i = pl.program_id(0)
    j = pl.program_id(1)
    tm = x_ref.shape[0]
    rows = tm + 2 * FFN_HALO
    tiles_per_seq = max(seq // tm, 1)

    @pl.when(j == 0)
    def _():
        first = (i % tiles_per_seq) == 0
        last = (i % tiles_per_seq) == tiles_per_seq - 1
        h_ref[0:FFN_HALO, :] = jnp.where(first, jnp.zeros_like(prev_ref), prev_ref[...])
        h_ref[FFN_HALO:FFN_HALO + tm, :] = hm_ref[...]
        h_ref[FFN_HALO + tm:, :] = jnp.where(last, jnp.zeros_like(next_ref), next_ref[...])
        o_ref[...] = jnp.zeros_like(o_ref)

    h = h_ref[...]
    if seq < tm:
        pos = lax.broadcasted_iota(jnp.int32, (rows, 1), 0) - FFN_HALO
        seq_start = (pos % seq) == 0
        seq_end = (pos % seq) == seq - 1

    def conv(w_ref, cw_ref, cb_ref):
        u = jnp.dot(h, w_ref[...], preferred_element_type=F32)
        cw = cw_ref[...]
        before = pltpu.roll(u, 1, 0)
        after = pltpu.roll(u, rows - 1, 0)
        if seq < tm:
            before = jnp.where(seq_start, 0.0, before)
            after = jnp.where(seq_end, 0.0, after)
        y = before * cw[0:1] + u * cw[1:2] + after * cw[2:3]
        return y[FFN_HALO:FFN_HALO + tm] + cb_ref[...]

    gate = conv(wg_ref, cwg_ref, cbg_ref)
    val = conv(wv_ref, cwv_ref, cbv_ref)
    a = (gate * jax.nn.sigmoid(gate) * val).astype(BF16)
    o_ref[...] += jnp.dot(a, wd_ref[...], preferred_element_type=F32)

    @pl.when(j == pl.num_programs(1) - 1)
    def _():
        o_ref[...] = x_ref[...] + mod_ref[0, 5:6, :] * o_ref[...]


def _ffn(x2, h2, mod, mod_row, w_up, conv_w, conv_b, w_down, seq, tm, tn=512):
    t, d = x2.shape
    dff = w_down.shape[0]
    assert t % tm == 0 and (seq % tm == 0 or tm % seq == 0)
    nj = dff // tn
    hb = tm // FFN_HALO
    nhb = t // FFN_HALO
    return pl.pallas_call(
        functools.partial(_ffn_kernel, seq=seq),
        grid=(t // tm, nj),
        in_specs=[pl.BlockSpec((FFN_HALO, d), lambda i, j: (jnp.maximum(i * hb - 1, 0), 0)),
                  pl.BlockSpec((tm, d), lambda i, j: (i, 0)),
                  pl.BlockSpec((FFN_HALO, d), lambda i, j: (jnp.minimum((i + 1) * hb, nhb - 1), 0)),
                  pl.BlockSpec((tm, d), lambda i, j: (i, 0)),
                  pl.BlockSpec((1, N_MOD, d), lambda i, j: (mod_row(i), 0, 0)),
                  pl.BlockSpec((d, tn), lambda i, j: (0, j)),
                  pl.BlockSpec((d, tn), lambda i, j: (0, nj + j)),
                  pl.BlockSpec((CONV_W, tn), lambda i, j: (0, j)),
                  pl.BlockSpec((CONV_W, tn), lambda i, j: (0, nj + j)),
                  pl.BlockSpec((1, tn), lambda i, j: (0, j)),
                  pl.BlockSpec((1, tn), lambda i, j: (0, nj + j)),
                  pl.BlockSpec((tn, d), lambda i, j: (j, 0))],
        out_specs=pl.BlockSpec((tm, d), lambda i, j: (i, 0)),
        out_shape=jax.ShapeDtypeStruct((t, d), F32),
        scratch_shapes=[pltpu.VMEM((tm + 2 * FFN_HALO, d), BF16)],
        compiler_params=_cp(("parallel", "arbitrary")),
        name="conv_ffn",
    )(h2, h2, h2, x2, mod, w_up, w_up, conv_w, conv_w, conv_b, conv_b, w_down)


def _block_ones(group):
    idx = np.arange(GW) // group
    return jnp.asarray(idx[:, None] == idx[None, :], dtype=BF16)


def _pad_heads(w, heads, width):
    lead = w.shape[:-1]
    w = w.reshape(lead + (heads, width))
    w = jnp.pad(w, [(0, 0)] * len(lead) + [(0, 0), (0, LANES - width)])
    return w.reshape(lead + (heads * LANES,))


def _layer_weights(P, l):
    d = P["w_in"].shape[1]
    w_in = P["w_in"][l]
    o = np.cumsum((0, 512, 512, 512, MLA_Q_LORA, MLA_KV_LORA, MLA_ROPE, 512, 512, 512, 512))
    col = lambda k: w_in[:, o[k]:o[k + 1]]
    w_all = jnp.concatenate(
        [col(0), col(1), col(2), col(6), col(7), col(8), col(9), col(3), col(4), col(5),
         jnp.zeros((d, ML_W - MLA_Q_LORA - MLA_KV_LORA - MLA_ROPE), F32)], axis=1).astype(BF16)
    wukv = P["mla_w_ukv"][l].reshape(MLA_KV_LORA, MLA_H, MLA_NOPE + MLA_V)
    wuk = _pad_heads(wukv[:, :, :MLA_NOPE].reshape(MLA_KV_LORA, MLA_H * MLA_NOPE), MLA_H, MLA_NOPE)
    wuv = wukv[:, :, MLA_NOPE:].reshape(MLA_KV_LORA, MLA_H * MLA_V)
    place = np.zeros((LANES, GW), np.float32)
    for hh in range(MLA_H):
        place[np.arange(MLA_ROPE), hh * LANES + MLA_NOPE + np.arange(MLA_ROPE)] = 1.0
    tile = lambda g, reps: jnp.tile(g.astype(F32), reps)
    padw = lambda g: jnp.pad(g.astype(F32), (0, GW - g.shape[0]))
    gains = jnp.stack([
        tile(P["diff_qn_g"][l], 8) * (DIFF_DH ** -0.5 * LOG2E),
        tile(P["diff_kn_g"][l], 8),
        tile(P["na_qn_g"][l], 8) * (NA_DH ** -0.5 * LOG2E),
        tile(P["na_kn_g"][l], 8),
        _pad_heads(tile(P["mla_qn_g"][l], MLA_H), MLA_H, MLA_QK) * (MLA_QK ** -0.5 * LOG2E),
        _pad_heads(tile(P["mla_kn_g"][l], MLA_H), MLA_H, MLA_QK),
        padw(P["mla_qa_g"][l]),
        padw(P["mla_kva_g"][l]),
    ])
    return dict(
        w_in=w_all,
        wuq=_pad_heads(P["mla_w_uq"][l], MLA_H, MLA_QK).astype(BF16),
        wuk=wuk.astype(BF16), wuv=wuv.astype(BF16),
        place=jnp.asarray(place, dtype=BF16),
        gains=gains, g64=_block_ones(64), g128=_block_ones(LANES),
        norm1=P["norm1_g"][l].reshape(1, d).astype(F32),
        norm2=P["norm2_g"][l].reshape(1, d).astype(F32),
        lam=P["diff_lam"][l].astype(F32),
        subg=P["diff_sub_g"][l].reshape(1, 2 * DIFF_DH).astype(F32),
        pool_w=P["pool_w"][l].astype(BF16),
        pool_scale=P["pool_scale"][l].reshape(1, GW).astype(F32),
        w_out=P["w_out"][l].astype(BF16),
        w_up=P["w_up"][l].astype(BF16),
        conv_w=P["conv_w"][l].astype(F32),
        conv_b=P["conv_b"][l].reshape(1, -1).astype(F32),
        w_down=P["w_down"][l].astype(BF16),
    )


def _rope_tables(n):
    t = jnp.arange(n)
    rows = (t // GRID_W).astype(F32)
    cols = (t % GRID_W).astype(F32)

    def ang(dim):
        quarter = dim // 4
        inv = ROPE_BASE ** (-jnp.arange(quarter, dtype=F32) / quarter)
        return jnp.concatenate([rows[:, None] * inv, cols[:, None] * inv], axis=-1)

    a = ang(DIFF_DH)
    cos_a, sin_a = jnp.cos(a), jnp.sin(a)
    cd = jnp.tile(cos_a, (1, 4))
    sd = jnp.tile(jnp.concatenate([-sin_a, sin_a], axis=1), (1, 2))
    b = ang(MLA_ROPE)
    cos_b, sin_b = jnp.cos(b), jnp.sin(b)
    ones = jnp.ones((n, MLA_NOPE), F32)
    pad = LANES - MLA_NOPE - MLA_ROPE
    cm = jnp.concatenate([ones, cos_b, cos_b, jnp.ones((n, pad), F32)], axis=1)
    sm = jnp.concatenate([0.0 * ones, -sin_b, sin_b, jnp.zeros((n, pad), F32)], axis=1)
    return jnp.stack([cd, sd, cm, sm])


def _layer(x, mod, mod_row_of_batch, W, layer, cache, rope_tab, bias_tab):
    b, n, d = x.shape
    t = b * n
    x2 = x.reshape(t, d)
    lam_init = 0.8 - 0.6 * math.exp(-0.3 * layer)
    tm = min(512, n)
    row_in = lambda i: mod_row_of_batch(i // (n // tm))
    outs = _inproj(x2, mod, row_in, W["norm1"], W, rope_tab, n, states=cache is None, tm=tm)
    qd, kd, vd, qn, kn, vn, pu, qm, km, vm = [a.reshape(b, n, GW) for a in outs[:10]]
    if cache is None:
        oa = _attention("diff", qd, [(kd, vd)], n, n, W["lam"], W["subg"], lam_init)
        ob = _attention("full", qm, [(km, vm)], n, n)
        oc = _attention("pair", qn, [(kn, vn)], n, n)
        state = outs[10:]
    else:
        a_k, a_v, b_ckv, b_kpe, c_k, c_v = cache
        pc = a_k.shape[1]
        ck = a_k.reshape(b, pc, GW).astype(BF16)
        cv = a_v.reshape(b, pc, GW).astype(BF16)
        oa = _attention_latent("diff", qd, ck, cv, kd, vd, W["lam"], W["subg"], lam_init)
        kpe_pad = jnp.pad(b_kpe.reshape(b * pc, MLA_ROPE), ((0, 0), (0, LANES - MLA_ROPE)))
        kmc, vmc = _mla_cache(b_ckv.reshape(b * pc, MLA_KV_LORA), kpe_pad, W)
        ob = _attention_latent("full", qm, kmc.reshape(b, pc, GW), vmc.reshape(b, pc, GW), km, vm)
        nk = c_k.reshape(b, pc, GW).astype(BF16)
        nv = c_v.reshape(b, pc, GW).astype(BF16)
        oc = _na_latent(qn, kn, vn, nk, nv, bias_tab)
        state = ()
    od = _pool(pu, W["pool_w"], W["pool_scale"])
    flat = lambda a: a.reshape(t, GW)
    x2, h2 = _outproj(flat(oa), flat(ob), flat(oc), flat(od), x2, mod, row_in, W["norm2"], W["w_out"], tm=tm)
    tmf = min(512, t) if cache is None else min(512, n)
    row_ffn = lambda i: mod_row_of_batch((i * tmf) // n)
    x2 = _ffn(x2, h2, mod, row_ffn, W["w_up"], W["conv_w"], W["conv_b"], W["w_down"], n, tm=tmf)
    return x2.reshape(b, n, d), state


def kernel(x_prompt, x_sample, cache_diff_k, cache_diff_v, cache_mla_ckv, cache_mla_kpe, cache_na_k, cache_na_v, c, c_ctx, norm1_g, norm2_g, ada_w, ada_b, w_in, diff_qn_g, diff_kn_g, diff_lam, diff_sub_g, mla_qa_g, mla_kva_g, mla_w_uq, mla_w_ukv, mla_qn_g, mla_kn_g, na_qn_g, na_kn_g, na_bias, pool_w, pool_scale, w_out, w_up, conv_w, conv_b, w_down):
    P = dict(norm1_g=norm1_g, norm2_g=norm2_g, w_in=w_in, diff_qn_g=diff_qn_g, diff_kn_g=diff_kn_g,
             diff_lam=diff_lam, diff_sub_g=diff_sub_g, mla_qa_g=mla_qa_g, mla_kva_g=mla_kva_g,
             mla_w_uq=mla_w_uq, mla_w_ukv=mla_w_ukv, mla_qn_g=mla_qn_g, mla_kn_g=mla_kn_g,
             na_qn_g=na_qn_g, na_kn_g=na_kn_g, pool_w=pool_w, pool_scale=pool_scale, w_out=w_out,
             w_up=w_up, conv_w=conv_w, conv_b=conv_b, w_down=w_down)
    depth = w_in.shape[0]
    d = x_prompt.shape[-1]
    bd, nd = x_sample.shape[:2]
    n_rows = -(-(bd + 1) // 8) * 8
    cv = jnp.concatenate([c, c_ctx[None, :], jnp.zeros((n_rows - bd - 1, d), F32)], axis=0)
    mod_all = _ada(cv, ada_w, ada_b).reshape(depth, n_rows, N_MOD, d)
    rope_tab = _rope_tables(nd)
    Ws = [_layer_weights(P, l) for l in range(depth)]

    xp = x_prompt
    states = []
    for l in range(depth):
        xp, st = _layer(xp, mod_all[l], lambda bi: bd, Ws[l], l, None, None, None)
        states.append(st)
    xs = x_sample
    for l in range(depth):
        cache_l = (cache_diff_k[:, l], cache_diff_v[:, l], cache_mla_ckv[:, l], cache_mla_kpe[:, l],
                   cache_na_k[:, l], cache_na_v[:, l])
        bias_tab = _na_bias_table(na_bias[l], nd // GRID_W)
        xs, _ = _layer(xs, mod_all[l], lambda bi: bi, Ws[l], l, cache_l, rope_tab, bias_tab)

    bp, npr = x_prompt.shape[:2]

    def stack(k, shape):
        return jnp.stack([s[k].reshape((bp, npr) + shape) for s in states], axis=1)

    return (xp, xs,
            stack(0, (DIFF_H, 2, DIFF_DH)), stack(1, (DIFF_H, 2 * DIFF_DH)),
            stack(2, (MLA_KV_LORA,)), stack(3, (MLA_ROPE,)),
            stack(4, (NA_H, NA_DH)), stack(5, (NA_H, NA_DH)))
```

```python
import functools
import math

import numpy as np
import jax
import jax.numpy as jnp
from jax import lax
from jax.experimental import pallas as pl
from jax.experimental.pallas import tpu as pltpu

F32 = jnp.float32
BF16 = jnp.bfloat16

GRID_W = 64
ROPE_BASE = 10000.0
EPS = 1e-6
N_MOD = 6
DIFF_H = 4
DIFF_DH = 64
MLA_H = 4
MLA_NOPE = 64
MLA_ROPE = 32
MLA_V = 128
MLA_Q_LORA = 384
MLA_KV_LORA = 128
NA_H = 8
NA_DH = 64
NA_ROWS = 8
NA_COLS = 16
POOL_WINDOWS = (2, 4, 8, 16)
POOL_G = 4
POOL_C = 128
CONV_W = 3

GW = 512
LANES = 128
NSLOT = GW // LANES
MXU_DIM = 256
MLA_QK = MLA_NOPE + MLA_ROPE
V7X_VMEM_LIMIT = 56 * 1024 * 1024
NEG = -1e30
LOG2E = math.log2(math.e)

C_DQ, C_DK, C_DV, C_NQ, C_NK, C_NV, C_PU, C_ML = 0, 512, 1024, 1536, 2048, 2560, 3072, 3584
ML_W = 640
IN_WP = C_ML + ML_W


def _cp(sem, vmem=V7X_VMEM_LIMIT):
    return pltpu.CompilerParams(dimension_semantics=sem, vmem_limit_bytes=vmem)


def _const_spec(shape):
    nd = len(shape)
    return pl.BlockSpec(shape, lambda *_: (0,) * nd, pipeline_mode=pl.Buffered(1))


def _ada_kernel(c_ref, w_ref, b_ref, o_ref):
    c = c_ref[...]
    s = c * jax.nn.sigmoid(c)
    o_ref[0] = jnp.dot(s.astype(BF16), w_ref[0].astype(BF16), preferred_element_type=F32) + b_ref[0]


def _ada(cv, ada_w, ada_b, tn=1024):
    depth, d, n = ada_w.shape
    r = cv.shape[0]
    return pl.pallas_call(
        _ada_kernel,
        grid=(depth, n // tn),
        in_specs=[pl.BlockSpec((r, d), lambda l, j: (0, 0)),
                  pl.BlockSpec((1, d, tn), lambda l, j: (l, 0, j)),
                  pl.BlockSpec((1, 1, tn), lambda l, j: (l, 0, j))],
        out_specs=pl.BlockSpec((1, r, tn), lambda l, j: (l, 0, j)),
        out_shape=jax.ShapeDtypeStruct((depth, r, n), F32),
        compiler_params=_cp(("parallel", "parallel")),
        name="adaln",
    )(cv, ada_w, ada_b.reshape(depth, 1, n))


def _modulated(x, g, mod, k):
    ms = jnp.mean(x * x, axis=-1, keepdims=True)
    y = x * lax.rsqrt(ms + EPS) * g
    return y * (1.0 + mod[k + 1:k + 2, :]) + mod[k:k + 1, :]


def _group_rms(z, gsum_ref, inv_n, gain):
    sq = (z * z).astype(BF16)
    ss = jnp.concatenate(
        [jnp.dot(sq[:, c:c + MXU_DIM], gsum_ref[c:c + MXU_DIM, c:c + MXU_DIM], preferred_element_type=F32)
         for c in range(0, z.shape[1], MXU_DIM)], axis=1)
    return z * lax.rsqrt(ss * inv_n + EPS) * gain


def _rope_chunks(z, cos_ref, sin_ref, lo_pred, shift_lo, shift_hi):
    c = cos_ref[...]
    s = sin_ref[...]
    outs = []
    for j in range(z.shape[1] // LANES):
        zc = z[:, j * LANES:(j + 1) * LANES]
        partner = jnp.where(lo_pred, pltpu.roll(zc, shift_lo, 1), pltpu.roll(zc, shift_hi, 1))
        outs.append(zc * c + partner * s)
    return jnp.concatenate(outs, axis=1)


def _mla_kv(ckv_b, kp, wuk_ref, wuv_ref, g128_ref, kgain):
    kp_slot = pltpu.roll(kp, MLA_NOPE, 1)
    kh = jnp.dot(ckv_b, wuk_ref[...], preferred_element_type=F32) + jnp.concatenate([kp_slot] * MLA_H, axis=1)
    k = _group_rms(kh, g128_ref, 1.0 / MLA_QK, kgain)
    v = jnp.dot(ckv_b, wuv_ref[...], preferred_element_type=F32)
    return k, v


def _inproj_kernel(*refs, rope, states):
    it = iter(refs)
    x_ref, mod_ref, g_ref, w_ref, wuq_ref, wuk_ref, wuv_ref = (next(it) for _ in range(7))
    gains_ref, g64_ref, g128_ref = (next(it) for _ in range(3))
    if rope:
        rope_ref = next(it)
    qd_ref, kd_ref, vd_ref, qn_ref, kn_ref, vn_ref, pu_ref, qm_ref, km_ref, vm_ref = (
        next(it) for _ in range(10))
    if states:
        ska_ref, sva_ref, sckv_ref, skpe_ref, skc_ref, svc_ref = (next(it) for _ in range(6))

    h = _modulated(x_ref[...], g_ref[...], mod_ref[0], 0).astype(BF16)
    gains = gains_ref[...]
    lane = lax.broadcasted_iota(jnp.int32, (1, LANES), 1)

    def seg(c0, w=GW):
        return jnp.dot(h, w_ref[:, c0:c0 + w], preferred_element_type=F32)

    def rope_d(z):
        if not rope:
            return z
        return _rope_chunks(z, rope_ref.at[0], rope_ref.at[1], (lane % 64) < 32, LANES - 32, 32)

    def rope_m(z):
        if not rope:
            return z
        return _rope_chunks(z, rope_ref.at[2], rope_ref.at[3], lane < MLA_NOPE + MLA_ROPE // 2,
                            LANES - MLA_ROPE // 2, MLA_ROPE // 2)

    qd_ref[...] = rope_d(_group_rms(seg(C_DQ), g64_ref, 1.0 / DIFF_DH, gains[0:1])).astype(BF16)
    kd = _group_rms(seg(C_DK), g64_ref, 1.0 / DIFF_DH, gains[1:2])
    kd_ref[...] = rope_d(kd).astype(BF16)
    vd = seg(C_DV)
    vd_ref[...] = vd.astype(BF16)
    qn_ref[...] = _group_rms(seg(C_NQ), g64_ref, 1.0 / NA_DH, gains[2:3]).astype(BF16)
    kn = _group_rms(seg(C_NK), g64_ref, 1.0 / NA_DH, gains[3:4])
    kn_ref[...] = kn.astype(BF16)
    vn = seg(C_NV)
    vn_ref[...] = vn.astype(BF16)
    pu_ref[...] = seg(C_PU)
    zm = seg(C_ML, ML_W)
    cq = zm[:, :MLA_Q_LORA]
    cq = cq * lax.rsqrt(jnp.mean(cq * cq, axis=-1, keepdims=True) + EPS) * gains[6:7, :MLA_Q_LORA]
    qh = jnp.dot(cq.astype(BF16), wuq_ref[...], preferred_element_type=F32)
    qm_ref[...] = rope_m(_group_rms(qh, g128_ref, 1.0 / MLA_QK, gains[4:5])).astype(BF16)
    ck = zm[:, MLA_Q_LORA:MLA_Q_LORA + MLA_KV_LORA]
    ckv = ck * lax.rsqrt(jnp.mean(ck * ck, axis=-1, keepdims=True) + EPS) * gains[7:8, :MLA_KV_LORA]
    kp = zm[:, GW:GW + LANES]
    km, vm = _mla_kv(ckv.astype(BF16), kp, wuk_ref, wuv_ref, g128_ref, gains[5:6])
    km_ref[...] = rope_m(km).astype(BF16)
    vm_ref[...] = vm.astype(BF16)
    if states:
        ska_ref[...] = kd
        sva_ref[...] = vd
        sckv_ref[...] = ckv
        skpe_ref[...] = kp[:, :MLA_ROPE]
        skc_ref[...] = kn
        svc_ref[...] = vn


def _inproj(x2, mod, mod_row, g1, W, rope_tab, seq, states, tm=256):
    t, d = x2.shape
    rope = rope_tab is not None
    tiles_per_seq = seq // tm
    tok = lambda w: pl.BlockSpec((tm, w), lambda i: (i, 0))
    in_specs = [tok(d),
                pl.BlockSpec((1, N_MOD, d), lambda i: (mod_row(i), 0, 0)),
                _const_spec((1, d)),
                _const_spec(W["w_in"].shape), _const_spec(W["wuq"].shape),
                _const_spec(W["wuk"].shape), _const_spec(W["wuv"].shape),
                _const_spec(W["gains"].shape), _const_spec(W["g64"].shape),
                _const_spec(W["g128"].shape)]
    args = [x2, mod, g1, W["w_in"], W["wuq"], W["wuk"], W["wuv"], W["gains"], W["g64"], W["g128"]]
    if rope:
        in_specs.append(pl.BlockSpec((4, tm, LANES), lambda i: (0, i % tiles_per_seq, 0)))
        args.append(rope_tab)
    out_specs = [tok(GW)] * 10
    out_shape = [jax.ShapeDtypeStruct((t, GW), BF16)] * 6 + [jax.ShapeDtypeStruct((t, GW), F32)] + \
                [jax.ShapeDtypeStruct((t, GW), BF16)] * 3
    if states:
        widths = (GW, GW, MLA_KV_LORA, MLA_ROPE, GW, GW)
        out_specs += [tok(w) for w in widths]
        out_shape += [jax.ShapeDtypeStruct((t, w), F32) for w in widths]
    return pl.pallas_call(
        functools.partial(_inproj_kernel, rope=rope, states=states),
        grid=(t // tm,),
        in_specs=in_specs, out_specs=out_specs, out_shape=out_shape,
        compiler_params=_cp(("parallel",)),
        name="inproj",
    )(*args)


def _mla_cache_kernel(ckv_ref, kp_ref, wuk_ref, wuv_ref, g128_ref, gains_ref, k_ref, v_ref):
    k, v = _mla_kv(ckv_ref[...].astype(BF16), kp_ref[...], wuk_ref, wuv_ref, g128_ref, gains_ref[5:6, :])
    k_ref[...] = k.astype(BF16)
    v_ref[...] = v.astype(BF16)


def _mla_cache(ckv2, kp2, W, tm=256):
    t = ckv2.shape[0]
    tok = lambda w: pl.BlockSpec((tm, w), lambda i: (i, 0))
    return pl.pallas_call(
        _mla_cache_kernel,
        grid=(t // tm,),
        in_specs=[tok(MLA_KV_LORA), tok(LANES), _const_spec(W["wuk"].shape), _const_spec(W["wuv"].shape),
                  _const_spec(W["g128"].shape), _const_spec(W["gains"].shape)],
        out_specs=[tok(GW), tok(GW)],
        out_shape=[jax.ShapeDtypeStruct((t, GW), BF16)] * 2,
        compiler_params=_cp(("parallel",)),
        name="mla_cache_kv",
    )(ckv2, kp2, W["wuk"], W["wuv"], W["g128"], W["gains"])


def _diff_lambda(lam_ref, lam_init):
    lf = lam_ref[...]
    return (jnp.exp(jnp.sum(lf[0:1] * lf[1:2], axis=-1, keepdims=True))
            - jnp.exp(jnp.sum(lf[2:3] * lf[3:4], axis=-1, keepdims=True)) + lam_init)


def _stream_queries(q, mode):
    if mode == "full":
        return [q]
    lane = lax.broadcasted_iota(jnp.int32, (1, LANES), 1)
    zero = jnp.zeros_like(q)
    return [jnp.where(lane < 64, q, zero), jnp.where(lane < 64, zero, q)]


def _combine_streams(outs, mode, lam, subg_ref, lam_init):
    if mode == "full":
        return outs[0]
    if mode == "pair":
        lane = lax.broadcasted_iota(jnp.int32, (1, LANES), 1)
        return jnp.where(lane < 64, outs[0], outs[1])
    o = outs[0] - lam * outs[1]
    o = o * lax.rsqrt(jnp.mean(o * o, axis=-1, keepdims=True) + EPS) * subg_ref[...]
    return o * (1.0 - lam_init)


def _online_attend(q_ref, srcs, chunks, mode, lam, subg_ref, lam_init, o_ref):
    tq = q_ref.shape[1]
    nstream = 1 if mode == "full" else 2
    for slot in range(NSLOT):
        sl = slice(slot * LANES, (slot + 1) * LANES)
        qs = _stream_queries(q_ref[0, :, sl], mode)

        def step(k_c, v_cs, state):
            new = []
            for si in range(nstream):
                m, l, acc = state[3 * si:3 * si + 3]
                s = lax.dot_general(qs[si], k_c, (((1,), (1,)), ((), ())), preferred_element_type=F32)
                m_new = jnp.maximum(m, jnp.max(s, axis=-1, keepdims=True))
                alpha = jnp.exp2(m - m_new)
                p = jnp.exp2(s - m_new)
                l = alpha * l + jnp.sum(p, axis=-1, keepdims=True)
                acc = alpha * acc + jnp.dot(p.astype(BF16), v_cs[si % len(v_cs)],
                                            preferred_element_type=F32)
                new += [m_new, l, acc]
            return tuple(new)

        state = ()
        for _ in range(nstream):
            state += (jnp.full((tq, 1), NEG, F32), jnp.zeros((tq, 1), F32), jnp.zeros((tq, LANES), F32))
        for (k_ref, v_refs), (n_chunk, tk) in zip(srcs, chunks):
            if n_chunk == 1:
                state = step(k_ref[0, :, sl], [v[0, :, sl] for v in v_refs], state)
            else:
                def body(c, st, k_ref=k_ref, v_refs=v_refs, tk=tk):
                    rows = pl.ds(pl.multiple_of(c * tk, tk), tk)
                    return step(k_ref[0, rows, sl], [v[0, rows, sl] for v in v_refs], st)
                state = lax.fori_loop(0, n_chunk, body, state)

        outs = [state[3 * si + 2] / state[3 * si + 1] for si in range(nstream)]
        o_ref[0, :, sl] = _combine_streams(outs, mode, lam, subg_ref, lam_init).astype(o_ref.dtype)


def _attn_kernel(*refs, mode, chunks, lam_init):
    it = iter(refs)
    q_ref = next(it)
    srcs = []
    for _ in chunks:
        k_ref = next(it)
        srcs.append((k_ref, (next(it),)))
    lam, subg_ref = None, None
    if mode == "diff":
        lam = _diff_lambda(next(it), lam_init)
        subg_ref = next(it)
    o_ref = next(it)
    _online_attend(q_ref, srcs, chunks, mode, lam, subg_ref, lam_init, o_ref)


AUG = 2 * LANES
SHIFT_MARGIN = 1.0 + 2.0 ** -6
SHIFT_LIMIT = 48.0


def _attn_fast_kernel(*refs, mode, chunks, lam_init, head_dim):
    it = iter(refs)
    q_ref, kc_ref, vc_ref, k_ref, v_ref, qgain_ref = (next(it) for _ in range(6))
    lam, subg_ref = None, None
    if mode == "diff":
        lam = _diff_lambda(next(it), lam_init)
        subg_ref = next(it)
    o_ref, kaug_ref, vext_ref, shift_ref = (next(it) for _ in range(4))

    tq = q_ref.shape[1]
    pc, n = kc_ref.shape[1], k_ref.shape[1]
    kt = pc + n
    nstream = 1 if mode == "full" else 2
    lane = lax.broadcasted_iota(jnp.int32, (1, LANES), 1)

    @pl.when(pl.program_id(1) == 0)
    def _():
        q_bound = math.sqrt(head_dim) * jnp.max(jnp.abs(qgain_ref[...]), axis=-1, keepdims=True)
        shift_ref[...] = jnp.zeros_like(shift_ref)
        for slot in range(NSLOT):
            sl = slice(slot * LANES, (slot + 1) * LANES)
            a0 = slot * AUG
            kaug_ref[0:pc, a0:a0 + LANES] = kc_ref[0, :, sl]
            kaug_ref[pc:kt, a0:a0 + LANES] = k_ref[0, :, sl]
            vext_ref[0:pc, a0:a0 + LANES] = vc_ref[0, :, sl]
            vext_ref[pc:kt, a0:a0 + LANES] = v_ref[0, :, sl]
            vext_ref[:, a0 + LANES:a0 + AUG] = jnp.ones((kt, LANES), BF16)
            shift_lanes = jnp.zeros((1, LANES), F32)
            for si in range(nstream):
                best = None
                for src in (kc_ref, k_ref):
                    kk = _stream_queries(src[0, :, sl], mode)[si].astype(F32)
                    nrm = jnp.max(jnp.sum(kk * kk, axis=-1, keepdims=True), axis=0, keepdims=True)
                    best = nrm if best is None else jnp.maximum(best, nrm)
                shift = q_bound * jnp.sqrt(best) * SHIFT_MARGIN
                row = slot * nstream + si
                shift_ref[row:row + 1, :] = jnp.broadcast_to(shift, (1, LANES))
                shift_lanes = jnp.where(lane == si, -shift, shift_lanes)
            kaug_ref[:, a0 + LANES:a0 + AUG] = jnp.broadcast_to(shift_lanes.astype(BF16), (kt, LANES))

    safe = jnp.max(shift_ref[...]) <= SHIFT_LIMIT

    @pl.when(safe)
    def _():
        for slot in range(NSLOT):
            sl = slice(slot * LANES, (slot + 1) * LANES)
            a0 = slot * AUG
            outs = []
            for si, qm in enumerate(_stream_queries(q_ref[0, :, sl], mode)):
                pick = jnp.broadcast_to(jnp.where(lane == si, 1.0, 0.0).astype(BF16), (tq, LANES))
                qa = jnp.concatenate([qm, pick], axis=1)
                s = lax.dot_general(qa, kaug_ref[:, a0:a0 + AUG], (((1,), (1,)), ((), ())),
                                    preferred_element_type=F32)
                acc = jnp.dot(jnp.exp2(s).astype(BF16), vext_ref[:, a0:a0 + AUG],
                              preferred_element_type=F32)
                outs.append(acc[:, 0:LANES] / acc[:, LANES:AUG])
            o = _combine_streams(outs, mode, lam, subg_ref, lam_init)
            o_ref[0, :, sl] = o.astype(o_ref.dtype)

    @pl.when(jnp.logical_not(safe))
    def _():
        _online_attend(q_ref, [(kc_ref, (vc_ref,)), (k_ref, (v_ref,))], chunks, mode, lam, subg_ref,
                       lam_init, o_ref)


def _attention_latent(mode, q, kc, vc, k, v, qgain, head_dim, lam=None, subg=None, lam_init=0.0,
                      tq=256, tk=512):
    b, n, _ = q.shape
    pc = kc.shape[1]
    kt = pc + n
    res = lambda m: pl.BlockSpec((1, m, GW), lambda bi, qi: (bi, 0, 0))
    in_specs = [pl.BlockSpec((1, tq, GW), lambda bi, qi: (bi, qi, 0)), res(pc), res(pc), res(n), res(n),
                _const_spec(qgain.shape)]
    args = [q, kc, vc, k, v, qgain]
    if mode == "diff":
        in_specs += [_const_spec(lam.shape), _const_spec(subg.shape)]
        args += [lam, subg]
    chunks = ((1, pc), (n // tk, tk))
    return pl.pallas_call(
        functools.partial(_attn_fast_kernel, mode=mode, chunks=chunks, lam_init=lam_init, head_dim=head_dim),
        grid=(b, n // tq),
        in_specs=in_specs,
        out_specs=pl.BlockSpec((1, tq, GW), lambda bi, qi: (bi, qi, 0)),
        out_shape=jax.ShapeDtypeStruct((b, n, GW), BF16),
        scratch_shapes=[pltpu.VMEM((kt, NSLOT * AUG), BF16), pltpu.VMEM((kt, NSLOT * AUG), BF16),
                        pltpu.VMEM((8, LANES), F32)],
        compiler_params=_cp(("parallel", "arbitrary")),
        name="attn_latent_" + mode,
    )(*args)


def _attention(mode, q, srcs, tq, tk, lam=None, subg=None, lam_init=0.0):
    b, n, _ = q.shape
    in_specs = [pl.BlockSpec((1, tq, GW), lambda bi, qi: (bi, qi, 0))]
    args = [q]
    chunks = []
    for src in srcs:
        ks = src[0].shape[1]
        step = min(tk, ks)
        chunks.append((ks // step, step))
        for a in src:
            in_specs.append(pl.BlockSpec((1, ks, GW), lambda bi, qi: (bi, 0, 0)))
            args.append(a)
    if mode == "diff":
        in_specs += [_const_spec(lam.shape), _const_spec(subg.shape)]
        args += [lam, subg]
    return pl.pallas_call(
        functools.partial(_attn_kernel, mode=mode, chunks=tuple(chunks), lam_init=lam_init),
        grid=(b, n // tq),
        in_specs=in_specs,
        out_specs=pl.BlockSpec((1, tq, GW), lambda bi, qi: (bi, qi, 0)),
        out_shape=jax.ShapeDtypeStruct((b, n, GW), BF16),
        compiler_params=_cp(("parallel", "parallel")),
        name="attn_" + mode,
    )(*args)


NA_QROWS = 4
NA_KROWS = NA_QROWS + NA_ROWS


def _na_key_start(r0, rows):
    return jnp.clip(r0 - NA_ROWS // 2, 0, rows - NA_KROWS)


def _na_kernel(q_ref, k_ref, v_ref, kc_ref, vc_ref, bias_ref, o_ref, *, rows):
    r0 = pl.program_id(1) * NA_QROWS
    ks = _na_key_start(r0, rows)
    loc = pl.ds(pl.multiple_of(ks * GRID_W, GRID_W), NA_KROWS * GRID_W)
    lane = lax.broadcasted_iota(jnp.int32, (1, LANES), 1)
    nt = (((1,), (1,)), ((), ()))
    for slot in range(NSLOT):
        sl = slice(slot * LANES, (slot + 1) * LANES)
        q = q_ref[0, :, sl]
        k_loc = k_ref[0, loc, sl]
        k_ctx = kc_ref[0, :, sl]
        v_loc = v_ref[0, loc, sl]
        v_ctx = vc_ref[0, :, sl]
        v_loc = jnp.concatenate([v_loc, jnp.ones_like(v_loc)], axis=1)
        v_ctx = jnp.concatenate([v_ctx, jnp.ones_like(v_ctx)], axis=1)
        halves = []
        for half, qh in enumerate(_stream_queries(q, "pair")):
            s_loc = lax.dot_general(qh, k_loc, nt, preferred_element_type=F32) + bias_ref[0, 2 * slot + half]
            s_ctx = lax.dot_general(qh, k_ctx, nt, preferred_element_type=F32)
            m = jnp.maximum(jnp.max(s_loc, axis=-1, keepdims=True), jnp.max(s_ctx, axis=-1, keepdims=True))
            acc = (jnp.dot(jnp.exp2(s_loc - m).astype(BF16), v_loc, preferred_element_type=F32)
                   + jnp.dot(jnp.exp2(s_ctx - m).astype(BF16), v_ctx, preferred_element_type=F32))
            halves.append(acc[:, 0:LANES] / acc[:, LANES:2 * LANES])
        o_ref[0, :, sl] = jnp.where(lane < 64, halves[0], halves[1]).astype(o_ref.dtype)


def _na_bias_table(bias_tab, rows):
    assert rows >= NA_KROWS + NA_QROWS and rows % NA_QROWS == 0
    qc = np.arange(GRID_W)[:, None]
    kc = np.arange(GRID_W)[None, :]
    cstart = np.clip(qc - NA_COLS // 2, 0, GRID_W - NA_COLS)
    in_win = (kc >= cstart) & (kc < cstart + NA_COLS)
    dc_idx = np.clip(kc - qc, 1 - NA_COLS, NA_COLS - 1) + NA_COLS - 1
    onehot = (dc_idx[:, :, None] == np.arange(2 * NA_COLS - 1)).astype(np.float32)
    toep = jnp.einsum("qkd,hrd->hrqk", jnp.asarray(onehot), bias_tab.astype(F32),
                      precision=lax.Precision.HIGHEST) * LOG2E
    toep = jnp.where(in_win, toep, NEG)
    masked = jnp.full((NA_H, GRID_W, GRID_W), NEG, F32)
    variants = []
    for var in (0, NA_ROWS // 2, NA_ROWS):
        q_rows = []
        for j in range(NA_QROWS):
            first = min(max(j + var - NA_ROWS // 2, 0), NA_KROWS - NA_ROWS)
            pieces = [toep[:, i - j - var + NA_ROWS - 1] if first <= i < first + NA_ROWS else masked
                      for i in range(NA_KROWS)]
            q_rows.append(jnp.concatenate(pieces, axis=-1))
        variants.append(jnp.concatenate(q_rows, axis=1))
    return jnp.stack(variants)


def _na_latent(q, k, v, kc, vc, bias):
    b, n, _ = q.shape
    rows = n // GRID_W
    pc = kc.shape[1]
    full = lambda m: pl.BlockSpec((1, m, GW), lambda bi, r: (bi, 0, 0))
    tq = NA_QROWS * GRID_W
    return pl.pallas_call(
        functools.partial(_na_kernel, rows=rows),
        grid=(b, rows // NA_QROWS),
        in_specs=[pl.BlockSpec((1, tq, GW), lambda bi, r: (bi, r, 0)),
                  full(n), full(n), full(pc), full(pc),
                  pl.BlockSpec((1, NA_H, tq, NA_KROWS * GRID_W),
                               lambda bi, r: ((r * NA_QROWS - _na_key_start(r * NA_QROWS, rows))
                                              // (NA_ROWS // 2), 0, 0, 0))],
        out_specs=pl.BlockSpec((1, tq, GW), lambda bi, r: (bi, r, 0)),
        out_shape=jax.ShapeDtypeStruct((b, n, GW), BF16),
        compiler_params=_cp(("parallel", "parallel")),
        name="attn_neighbourhood",
    )(q, k, v, kc, vc, bias)


POOL_HALO = 64


def _pool_kernel(prev_ref, main_ref, next_ref, w_ref, scale_ref, o_ref, *, seq):
    tp = main_ref.shape[1]
    t0 = pl.program_id(1) * tp
    ext = jnp.concatenate([prev_ref[0], main_ref[0], next_ref[0]], axis=0)
    ext_hi = ext.astype(BF16)
    ext_lo = (ext - ext_hi.astype(F32)).astype(BF16)
    t = t0 + lax.broadcasted_iota(jnp.int32, (tp, 1), 0)
    pos = t0 - POOL_HALO + lax.broadcasted_iota(jnp.int32, (1, tp + 2 * POOL_HALO), 1)
    for g, win in enumerate(POOL_WINDOWS):
        sl = slice(g * POOL_C, (g + 1) * POOL_C)
        lo = jnp.clip(t - win // 2, 0, seq)
        hi = jnp.clip(t - win // 2 + win, 0, seq)
        band = jnp.where((pos >= lo) & (pos < hi), 1.0, 0.0).astype(BF16)
        tot = (jnp.dot(band, ext_hi[:, sl], preferred_element_type=F32)
               + jnp.dot(band, ext_lo[:, sl], preferred_element_type=F32))
        mean = tot / (hi - lo).astype(F32)
        d = (mean - main_ref[0, :, sl]).astype(BF16)
        y = jnp.dot(d, w_ref[g], preferred_element_type=F32) * scale_ref[:, sl]
        o_ref[0, :, sl] = y.astype(o_ref.dtype)


def _pool(pu, w_pool, scale, tp=256):
    b, n, _ = pu.shape
    hb = tp // POOL_HALO
    last = n // POOL_HALO - 1
    return pl.pallas_call(
        functools.partial(_pool_kernel, seq=n),
        grid=(b, n // tp),
        in_specs=[pl.BlockSpec((1, POOL_HALO, GW), lambda bi, i: (bi, jnp.maximum(i * hb - 1, 0), 0)),
                  pl.BlockSpec((1, tp, GW), lambda bi, i: (bi, i, 0)),
                  pl.BlockSpec((1, POOL_HALO, GW), lambda bi, i: (bi, jnp.minimum((i + 1) * hb, last), 0)),
                  _const_spec(w_pool.shape), _const_spec(scale.shape)],
        out_specs=pl.BlockSpec((1, tp, GW), lambda bi, i: (bi, i, 0)),
        out_shape=jax.ShapeDtypeStruct((b, n, GW), BF16),
        compiler_params=_cp(("parallel", "parallel")),
        name="pool_mixer",
    )(pu, pu, pu, w_pool, scale)


def _outproj_kernel(oa_ref, ob_ref, oc_ref, od_ref, x_ref, mod_ref, g2_ref, w_ref, o_ref, h_ref):
    mix = jnp.dot(oa_ref[...], w_ref[0:GW, :], preferred_element_type=F32)
    mix += jnp.dot(ob_ref[...], w_ref[GW:2 * GW, :], preferred_element_type=F32)
    mix += jnp.dot(oc_ref[...], w_ref[2 * GW:3 * GW, :], preferred_element_type=F32)
    mix += jnp.dot(od_ref[...], w_ref[3 * GW:4 * GW, :], preferred_element_type=F32)
    mod = mod_ref[0]
    x = x_ref[...] + mod[2:3, :] * mix
    o_ref[...] = x
    h_ref[...] = _modulated(x, g2_ref[...], mod, 3).astype(BF16)


def _outproj(oa, ob, oc, od, x2, mod, mod_row, g2, w_out, tm=256):
    t, d = x2.shape
    tok = lambda w: pl.BlockSpec((tm, w), lambda i: (i, 0))
    return pl.pallas_call(
        _outproj_kernel,
        grid=(t // tm,),
        in_specs=[tok(GW)] * 4 + [tok(d), pl.BlockSpec((1, N_MOD, d), lambda i: (mod_row(i), 0, 0)),
                                   _const_spec((1, d)), _const_spec(w_out.shape)],
        out_specs=[tok(d), tok(d)],
        out_shape=[jax.ShapeDtypeStruct((t, d), F32), jax.ShapeDtypeStruct((t, d), BF16)],
        compiler_params=_cp(("parallel",)),
        name="outproj",
    )(oa, ob, oc, od, x2, mod, g2, w_out)


FFN_HALO = 16


def _ffn_kernel(prev_ref, hm_ref, next_ref, x_ref, mod_ref, wg_ref, wv_ref, cwg_ref, cwv_ref, cbg_ref,
                cbv_ref, wd_ref, o_ref, h_ref, *, seq):
    i = pl.program_id(0)
    j = pl.program_id(1)
    tm = x_ref.shape[0]
    rows = tm + 2 * FFN_HALO
    tiles_per_seq = max(seq // tm, 1)

    @pl.when(j == 0)
    def _():
        first = (i % tiles_per_seq) == 0
        last = (i % tiles_per_seq) == tiles_per_seq - 1
        h_ref[0:FFN_HALO, :] = jnp.where(first, jnp.zeros_like(prev_ref), prev_ref[...])
        h_ref[FFN_HALO:FFN_HALO + tm, :] = hm_ref[...]
        h_ref[FFN_HALO + tm:, :] = jnp.where(last, jnp.zeros_like(next_ref), next_ref[...])
        o_ref[...] = jnp.zeros_like(o_ref)

    h = h_ref[...]
    if seq < tm:
        pos = lax.broadcasted_iota(jnp.int32, (rows, 1), 0) - FFN_HALO
        seq_start = (pos % seq) == 0
        seq_end = (pos % seq) == seq - 1

    def conv(w_ref, cw_ref, cb_ref):
        u = jnp.dot(h, w_ref[...], preferred_element_type=F32)
        cw = cw_ref[...]
        before = pltpu.roll(u, 1, 0)
        after = pltpu.roll(u, rows - 1, 0)
        if seq < tm:
            before = jnp.where(seq_start, 0.0, before)
            after = jnp.where(seq_end, 0.0, after)
        y = before * cw[0:1] + u * cw[1:2] + after * cw[2:3]
        return y[FFN_HALO:FFN_HALO + tm] + cb_ref[...]

    gate = conv(wg_ref, cwg_ref, cbg_ref)
    val = conv(wv_ref, cwv_ref, cbv_ref)
    a = (gate * jax.nn.sigmoid(gate) * val).astype(BF16)
    o_ref[...] += jnp.dot(a, wd_ref[...], preferred_element_type=F32)

    @pl.when(j == pl.num_programs(1) - 1)
    def _():
        o_ref[...] = x_ref[...] + mod_ref[0, 5:6, :] * o_ref[...]


def _ffn(x2, h2, mod, mod_row, w_up, conv_w, conv_b, w_down, seq, tm, tn=512):
    t, d = x2.shape
    dff = w_down.shape[0]
    assert t % tm == 0 and (seq % tm == 0 or tm % seq == 0)
    nj = dff // tn
    hb = tm // FFN_HALO
    nhb = t // FFN_HALO
    return pl.pallas_call(
        functools.partial(_ffn_kernel, seq=seq),
        grid=(t // tm, nj),
        in_specs=[pl.BlockSpec((FFN_HALO, d), lambda i, j: (jnp.maximum(i * hb - 1, 0), 0)),
                  pl.BlockSpec((tm, d), lambda i, j: (i, 0)),
                  pl.BlockSpec((FFN_HALO, d), lambda i, j: (jnp.minimum((i + 1) * hb, nhb - 1), 0)),
                  pl.BlockSpec((tm, d), lambda i, j: (i, 0)),
                  pl.BlockSpec((1, N_MOD, d), lambda i, j: (mod_row(i), 0, 0)),
                  pl.BlockSpec((d, tn), lambda i, j: (0, j)),
                  pl.BlockSpec((d, tn), lambda i, j: (0, nj + j)),
                  pl.BlockSpec((CONV_W, tn), lambda i, j: (0, j)),
                  pl.BlockSpec((CONV_W, tn), lambda i, j: (0, nj + j)),
                  pl.BlockSpec((1, tn), lambda i, j: (0, j)),
                  pl.BlockSpec((1, tn), lambda i, j: (0, nj + j)),
                  pl.BlockSpec((tn, d), lambda i, j: (j, 0))],
        out_specs=pl.BlockSpec((tm, d), lambda i, j: (i, 0)),
        out_shape=jax.ShapeDtypeStruct((t, d), F32),
        scratch_shapes=[pltpu.VMEM((tm + 2 * FFN_HALO, d), BF16)],
        compiler_params=_cp(("parallel", "arbitrary")),
        name="conv_ffn",
    )(h2, h2, h2, x2, mod, w_up, w_up, conv_w, conv_w, conv_b, conv_b, w_down)


def _block_ones(group):
    idx = np.arange(GW) // group
    return jnp.asarray(idx[:, None] == idx[None, :], dtype=BF16)


def _pad_heads(w, heads, width):
    lead = w.shape[:-1]
    w = w.reshape(lead + (heads, width))
    w = jnp.pad(w, [(0, 0)] * len(lead) + [(0, 0), (0, LANES - width)])
    return w.reshape(lead + (heads * LANES,))


def _layer_weights(P, l):
    d = P["w_in"].shape[1]
    w_in = P["w_in"][l]
    o = np.cumsum((0, 512, 512, 512, MLA_Q_LORA, MLA_KV_LORA, MLA_ROPE, 512, 512, 512, 512))
    col = lambda k: w_in[:, o[k]:o[k + 1]]
    w_all = jnp.concatenate(
        [col(0), col(1), col(2), col(6), col(7), col(8), col(9), col(3), col(4), col(5),
         jnp.zeros((d, ML_W - MLA_Q_LORA - MLA_KV_LORA - MLA_ROPE), F32)], axis=1).astype(BF16)
    wukv = P["mla_w_ukv"][l].reshape(MLA_KV_LORA, MLA_H, MLA_NOPE + MLA_V)
    wuk = _pad_heads(wukv[:, :, :MLA_NOPE].reshape(MLA_KV_LORA, MLA_H * MLA_NOPE), MLA_H, MLA_NOPE)
    wuv = wukv[:, :, MLA_NOPE:].reshape(MLA_KV_LORA, MLA_H * MLA_V)
    tile = lambda g, reps: jnp.tile(g.astype(F32), reps)
    padw = lambda g: jnp.pad(g.astype(F32), (0, GW - g.shape[0]))
    gains = jnp.stack([
        tile(P["diff_qn_g"][l], 8) * (DIFF_DH ** -0.5 * LOG2E),
        tile(P["diff_kn_g"][l], 8),
        tile(P["na_qn_g"][l], 8) * (NA_DH ** -0.5 * LOG2E),
        tile(P["na_kn_g"][l], 8),
        _pad_heads(tile(P["mla_qn_g"][l], MLA_H), MLA_H, MLA_QK) * (MLA_QK ** -0.5 * LOG2E),
        _pad_heads(tile(P["mla_kn_g"][l], MLA_H), MLA_H, MLA_QK),
        padw(P["mla_qa_g"][l]),
        padw(P["mla_kva_g"][l]),
    ])
    return dict(
        w_in=w_all,
        wuq=_pad_heads(P["mla_w_uq"][l], MLA_H, MLA_QK).astype(BF16),
        wuk=wuk.astype(BF16), wuv=wuv.astype(BF16),
        gains=gains, g64=_block_ones(64), g128=_block_ones(LANES),
        norm1=P["norm1_g"][l].reshape(1, d).astype(F32),
        norm2=P["norm2_g"][l].reshape(1, d).astype(F32),
        lam=P["diff_lam"][l].astype(F32),
        subg=P["diff_sub_g"][l].reshape(1, 2 * DIFF_DH).astype(F32),
        pool_w=P["pool_w"][l].astype(BF16),
        pool_scale=P["pool_scale"][l].reshape(1, GW).astype(F32),
        w_out=P["w_out"][l].astype(BF16),
        w_up=P["w_up"][l].astype(BF16),
        conv_w=P["conv_w"][l].astype(F32),
        conv_b=P["conv_b"][l].reshape(1, -1).astype(F32),
        w_down=P["w_down"][l].astype(BF16),
    )


def _rope_tables(n):
    t = jnp.arange(n)
    rows = (t // GRID_W).astype(F32)
    cols = (t % GRID_W).astype(F32)

    def ang(dim):
        quarter = dim // 4
        inv = ROPE_BASE ** (-jnp.arange(quarter, dtype=F32) / quarter)
        return jnp.concatenate([rows[:, None] * inv, cols[:, None] * inv], axis=-1)

    a = ang(DIFF_DH)
    cos_a, sin_a = jnp.cos(a), jnp.sin(a)
    cd = jnp.tile(cos_a, (1, 4))
    sd = jnp.tile(jnp.concatenate([-sin_a, sin_a], axis=1), (1, 2))
    b = ang(MLA_ROPE)
    cos_b, sin_b = jnp.cos(b), jnp.sin(b)
    ones = jnp.ones((n, MLA_NOPE), F32)
    pad = LANES - MLA_NOPE - MLA_ROPE
    cm = jnp.concatenate([ones, cos_b, cos_b, jnp.ones((n, pad), F32)], axis=1)
    sm = jnp.concatenate([0.0 * ones, -sin_b, sin_b, jnp.zeros((n, pad), F32)], axis=1)
    return jnp.stack([cd, sd, cm, sm])


def _layer(x, mod, mod_row_of_batch, W, layer, cache, rope_tab, bias_tab):
    b, n, d = x.shape
    t = b * n
    x2 = x.reshape(t, d)
    lam_init = 0.8 - 0.6 * math.exp(-0.3 * layer)
    tm = min(512, n)
    row_in = lambda i: mod_row_of_batch(i // (n // tm))
    outs = _inproj(x2, mod, row_in, W["norm1"], W, rope_tab, n, states=cache is None, tm=tm)
    qd, kd, vd, qn, kn, vn, pu, qm, km, vm = [a.reshape(b, n, GW) for a in outs[:10]]
    if cache is None:
        oa = _attention("diff", qd, [(kd, vd)], n, n, W["lam"], W["subg"], lam_init)
        ob = _attention("full", qm, [(km, vm)], n, n)
        oc = _attention("pair", qn, [(kn, vn)], n, n)
        state = outs[10:]
    else:
        a_k, a_v, b_ckv, b_kpe, c_k, c_v = cache
        pc = a_k.shape[1]
        ck = a_k.reshape(b, pc, GW).astype(BF16)
        cv = a_v.reshape(b, pc, GW).astype(BF16)
        oa = _attention_latent("diff", qd, ck, cv, kd, vd, W["gains"][0:1], DIFF_DH, W["lam"], W["subg"],
                               lam_init)
        kpe_pad = jnp.pad(b_kpe.reshape(b * pc, MLA_ROPE), ((0, 0), (0, LANES - MLA_ROPE)))
        kmc, vmc = _mla_cache(b_ckv.reshape(b * pc, MLA_KV_LORA), kpe_pad, W)
        ob = _attention_latent("full", qm, kmc.reshape(b, pc, GW), vmc.reshape(b, pc, GW), km, vm,
                               W["gains"][4:5], MLA_QK)
        nk = c_k.reshape(b, pc, GW).astype(BF16)
        nv = c_v.reshape(b, pc, GW).astype(BF16)
        oc = _na_latent(qn, kn, vn, nk, nv, bias_tab)
        state = ()
    od = _pool(pu, W["pool_w"], W["pool_scale"])
    flat = lambda a: a.reshape(t, GW)
    x2, h2 = _outproj(flat(oa), flat(ob), flat(oc), flat(od), x2, mod, row_in, W["norm2"], W["w_out"], tm=tm)
    tmf = min(512, t) if cache is None else min(512, n)
    row_ffn = lambda i: mod_row_of_batch((i * tmf) // n)
    x2 = _ffn(x2, h2, mod, row_ffn, W["w_up"], W["conv_w"], W["conv_b"], W["w_down"], n, tm=tmf)
    return x2.reshape(b, n, d), state


def kernel(x_prompt, x_sample, cache_diff_k, cache_diff_v, cache_mla_ckv, cache_mla_kpe, cache_na_k, cache_na_v, c, c_ctx, norm1_g, norm2_g, ada_w, ada_b, w_in, diff_qn_g, diff_kn_g, diff_lam, diff_sub_g, mla_qa_g, mla_kva_g, mla_w_uq, mla_w_ukv, mla_qn_g, mla_kn_g, na_qn_g, na_kn_g, na_bias, pool_w, pool_scale, w_out, w_up, conv_w, conv_b, w_down):
    P = dict(norm1_g=norm1_g, norm2_g=norm2_g, w_in=w_in, diff_qn_g=diff_qn_g, diff_kn_g=diff_kn_g,
             diff_lam=diff_lam, diff_sub_g=diff_sub_g, mla_qa_g=mla_qa_g, mla_kva_g=mla_kva_g,
             mla_w_uq=mla_w_uq, mla_w_ukv=mla_w_ukv, mla_qn_g=mla_qn_g, mla_kn_g=mla_kn_g,
             na_qn_g=na_qn_g, na_kn_g=na_kn_g, pool_w=pool_w, pool_scale=pool_scale, w_out=w_out,
             w_up=w_up, conv_w=conv_w, conv_b=conv_b, w_down=w_down)
    depth = w_in.shape[0]
    d = x_prompt.shape[-1]
    bd, nd = x_sample.shape[:2]
    n_rows = -(-(bd + 1) // 8) * 8
    cv = jnp.concatenate([c, c_ctx[None, :], jnp.zeros((n_rows - bd - 1, d), F32)], axis=0)
    mod_all = _ada(cv, ada_w, ada_b).reshape(depth, n_rows, N_MOD, d)
    rope_tab = _rope_tables(nd)
    Ws = [_layer_weights(P, l) for l in range(depth)]

    xp = x_prompt
    states = []
    for l in range(depth):
        xp, st = _layer(xp, mod_all[l], lambda bi: bd, Ws[l], l, None, None, None)
        states.append(st)
    xs = x_sample
    for l in range(depth):
        cache_l = (cache_diff_k[:, l], cache_diff_v[:, l], cache_mla_ckv[:, l], cache_mla_kpe[:, l],
                   cache_na_k[:, l], cache_na_v[:, l])
        bias_tab = _na_bias_table(na_bias[l], nd // GRID_W)
        xs, _ = _layer(xs, mod_all[l], lambda bi: bi, Ws[l], l, cache_l, rope_tab, bias_tab)

    bp, npr = x_prompt.shape[:2]

    def stack(k, shape):
        return jnp.stack([s[k].reshape((bp, npr) + shape) for s in states], axis=1)

    return (xp, xs,
            stack(0, (DIFF_H, 2, DIFF_DH)), stack(1, (DIFF_H, 2 * DIFF_DH)),
            stack(2, (MLA_KV_LORA,)), stack(3, (MLA_ROPE,)),
            stack(4, (NA_H, NA_DH)), stack(5, (NA_H, NA_DH)))
```

```python
import functools
import math

import numpy as np
import jax
import jax.numpy as jnp
from jax import lax
from jax.experimental import pallas as pl
from jax.experimental.pallas import tpu as pltpu

F32 = jnp.float32
BF16 = jnp.bfloat16

GRID_W = 64
ROPE_BASE = 10000.0
EPS = 1e-6
N_MOD = 6
DIFF_H = 4
DIFF_DH = 64
MLA_H = 4
MLA_NOPE = 64
MLA_ROPE = 32
MLA_V = 128
MLA_Q_LORA = 384
MLA_KV_LORA = 128
NA_H = 8
NA_DH = 64
NA_ROWS = 8
NA_COLS = 16
POOL_WINDOWS = (2, 4, 8, 16)
POOL_G = 4
POOL_C = 128
CONV_W = 3

GW = 512
LANES = 128
NSLOT = GW // LANES
MXU_DIM = 256
MLA_QK = MLA_NOPE + MLA_ROPE
V7X_VMEM_LIMIT = 56 * 1024 * 1024
NEG = -1e30
LOG2E = math.log2(math.e)

C_DQ, C_DK, C_DV, C_NQ, C_NK, C_NV, C_PU, C_ML = 0, 512, 1024, 1536, 2048, 2560, 3072, 3584
ML_W = 640
IN_WP = C_ML + ML_W


def _cp(sem, vmem=V7X_VMEM_LIMIT):
    return pltpu.CompilerParams(dimension_semantics=sem, vmem_limit_bytes=vmem)


def _const_spec(shape):
    nd = len(shape)
    return pl.BlockSpec(shape, lambda *_: (0,) * nd, pipeline_mode=pl.Buffered(1))


def _ada_kernel(c_ref, w_ref, b_ref, o_ref):
    c = c_ref[...]
    s = c * jax.nn.sigmoid(c)
    o_ref[0] = jnp.dot(s.astype(BF16), w_ref[0].astype(BF16), preferred_element_type=F32) + b_ref[0]


def _ada(cv, ada_w, ada_b, tn=1024):
    depth, d, n = ada_w.shape
    r = cv.shape[0]
    return pl.pallas_call(
        _ada_kernel,
        grid=(depth, n // tn),
        in_specs=[pl.BlockSpec((r, d), lambda l, j: (0, 0)),
                  pl.BlockSpec((1, d, tn), lambda l, j: (l, 0, j)),
                  pl.BlockSpec((1, 1, tn), lambda l, j: (l, 0, j))],
        out_specs=pl.BlockSpec((1, r, tn), lambda l, j: (l, 0, j)),
        out_shape=jax.ShapeDtypeStruct((depth, r, n), F32),
        compiler_params=_cp(("parallel", "parallel")),
        name="adaln",
    )(cv, ada_w, ada_b.reshape(depth, 1, n))


def _modulated(x, g, mod, k):
    ms = jnp.mean(x * x, axis=-1, keepdims=True)
    y = x * lax.rsqrt(ms + EPS) * g
    return y * (1.0 + mod[k + 1:k + 2, :]) + mod[k:k + 1, :]


def _group_rms(z, gsum_ref, inv_n, gain):
    sq = (z * z).astype(BF16)
    ss = jnp.concatenate(
        [jnp.dot(sq[:, c:c + MXU_DIM], gsum_ref[c:c + MXU_DIM, c:c + MXU_DIM], preferred_element_type=F32)
         for c in range(0, z.shape[1], MXU_DIM)], axis=1)
    return z * lax.rsqrt(ss * inv_n + EPS) * gain


def _rope_chunks(z, cos_ref, sin_ref, lo_pred, shift_lo, shift_hi):
    c = cos_ref[...]
    s = sin_ref[...]
    outs = []
    for j in range(z.shape[1] // LANES):
        zc = z[:, j * LANES:(j + 1) * LANES]
        partner = jnp.where(lo_pred, pltpu.roll(zc, shift_lo, 1), pltpu.roll(zc, shift_hi, 1))
        outs.append(zc * c + partner * s)
    return jnp.concatenate(outs, axis=1)


def _mla_kv(ckv_b, kp, wuk_ref, wuv_ref, g128_ref, kgain):
    kp_slot = pltpu.roll(kp, MLA_NOPE, 1)
    kh = jnp.dot(ckv_b, wuk_ref[...], preferred_element_type=F32) + jnp.concatenate([kp_slot] * MLA_H, axis=1)
    k = _group_rms(kh, g128_ref, 1.0 / MLA_QK, kgain)
    v = jnp.dot(ckv_b, wuv_ref[...], preferred_element_type=F32)
    return k, v


def _inproj_kernel(*refs, rope, states):
    it = iter(refs)
    x_ref, mod_ref, g_ref, w_ref, wuq_ref, wuk_ref, wuv_ref = (next(it) for _ in range(7))
    gains_ref, g64_ref, g128_ref = (next(it) for _ in range(3))
    if rope:
        rope_ref = next(it)
    qd_ref, kd_ref, vd_ref, qn_ref, kn_ref, vn_ref, pu_ref, qm_ref, km_ref, vm_ref = (
        next(it) for _ in range(10))
    if states:
        ska_ref, sva_ref, sckv_ref, skpe_ref, skc_ref, svc_ref = (next(it) for _ in range(6))

    h = _modulated(x_ref[...], g_ref[...], mod_ref[0], 0).astype(BF16)
    gains = gains_ref[...]
    lane = lax.broadcasted_iota(jnp.int32, (1, LANES), 1)

    def seg(c0, w=GW):
        return jnp.dot(h, w_ref[:, c0:c0 + w], preferred_element_type=F32)

    def rope_d(z):
        if not rope:
            return z
        return _rope_chunks(z, rope_ref.at[0], rope_ref.at[1], (lane % 64) < 32, LANES - 32, 32)

    def rope_m(z):
        if not rope:
            return z
        return _rope_chunks(z, rope_ref.at[2], rope_ref.at[3], lane < MLA_NOPE + MLA_ROPE // 2,
                            LANES - MLA_ROPE // 2, MLA_ROPE // 2)

    qd_ref[...] = rope_d(_group_rms(seg(C_DQ), g64_ref, 1.0 / DIFF_DH, gains[0:1])).astype(BF16)
    kd = _group_rms(seg(C_DK), g64_ref, 1.0 / DIFF_DH, gains[1:2])
    kd_ref[...] = rope_d(kd).astype(BF16)
    vd = seg(C_DV)
    vd_ref[...] = vd.astype(BF16)
    qn_ref[...] = _group_rms(seg(C_NQ), g64_ref, 1.0 / NA_DH, gains[2:3]).astype(BF16)
    kn = _group_rms(seg(C_NK), g64_ref, 1.0 / NA_DH, gains[3:4])
    kn_ref[...] = kn.astype(BF16)
    vn = seg(C_NV)
    vn_ref[...] = vn.astype(BF16)
    pu_ref[...] = seg(C_PU)
    zm = seg(C_ML, ML_W)
    cq = zm[:, :MLA_Q_LORA]
    cq = cq * lax.rsqrt(jnp.mean(cq * cq, axis=-1, keepdims=True) + EPS) * gains[6:7, :MLA_Q_LORA]
    qh = jnp.dot(cq.astype(BF16), wuq_ref[...], preferred_element_type=F32)
    qm_ref[...] = rope_m(_group_rms(qh, g128_ref, 1.0 / MLA_QK, gains[4:5])).astype(BF16)
    ck = zm[:, MLA_Q_LORA:MLA_Q_LORA + MLA_KV_LORA]
    ckv = ck * lax.rsqrt(jnp.mean(ck * ck, axis=-1, keepdims=True) + EPS) * gains[7:8, :MLA_KV_LORA]
    kp = zm[:, GW:GW + LANES]
    km, vm = _mla_kv(ckv.astype(BF16), kp, wuk_ref, wuv_ref, g128_ref, gains[5:6])
    km_ref[...] = rope_m(km).astype(BF16)
    vm_ref[...] = vm.astype(BF16)
    if states:
        ska_ref[...] = kd
        sva_ref[...] = vd
        sckv_ref[...] = ckv
        skpe_ref[...] = kp[:, :MLA_ROPE]
        skc_ref[...] = kn
        svc_ref[...] = vn


def _inproj(x2, mod, mod_row, g1, W, rope_tab, seq, states, tm=256):
    t, d = x2.shape
    rope = rope_tab is not None
    tiles_per_seq = seq // tm
    tok = lambda w: pl.BlockSpec((tm, w), lambda i: (i, 0))
    in_specs = [tok(d),
                pl.BlockSpec((1, N_MOD, d), lambda i: (mod_row(i), 0, 0)),
                _const_spec((1, d)),
                _const_spec(W["w_in"].shape), _const_spec(W["wuq"].shape),
                _const_spec(W["wuk"].shape), _const_spec(W["wuv"].shape),
                _const_spec(W["gains"].shape), _const_spec(W["g64"].shape),
                _const_spec(W["g128"].shape)]
    args = [x2, mod, g1, W["w_in"], W["wuq"], W["wuk"], W["wuv"], W["gains"], W["g64"], W["g128"]]
    if rope:
        in_specs.append(pl.BlockSpec((4, tm, LANES), lambda i: (0, i % tiles_per_seq, 0)))
        args.append(rope_tab)
    out_specs = [tok(GW)] * 10
    out_shape = [jax.ShapeDtypeStruct((t, GW), BF16)] * 6 + [jax.ShapeDtypeStruct((t, GW), F32)] + \
                [jax.ShapeDtypeStruct((t, GW), BF16)] * 3
    if states:
        widths = (GW, GW, MLA_KV_LORA, MLA_ROPE, GW, GW)
        out_specs += [tok(w) for w in widths]
        out_shape += [jax.ShapeDtypeStruct((t, w), F32) for w in widths]
    return pl.pallas_call(
        functools.partial(_inproj_kernel, rope=rope, states=states),
        grid=(t // tm,),
        in_specs=in_specs, out_specs=out_specs, out_shape=out_shape,
        compiler_params=_cp(("parallel",)),
        name="inproj",
    )(*args)


def _mla_cache_kernel(ckv_ref, kp_ref, wuk_ref, wuv_ref, g128_ref, gains_ref, k_ref, v_ref):
    k, v = _mla_kv(ckv_ref[...].astype(BF16), kp_ref[...], wuk_ref, wuv_ref, g128_ref, gains_ref[5:6, :])
    k_ref[...] = k.astype(BF16)
    v_ref[...] = v.astype(BF16)


def _mla_cache(ckv2, kp2, W, tm=256):
    t = ckv2.shape[0]
    tok = lambda w: pl.BlockSpec((tm, w), lambda i: (i, 0))
    return pl.pallas_call(
        _mla_cache_kernel,
        grid=(t // tm,),
        in_specs=[tok(MLA_KV_LORA), tok(LANES), _const_spec(W["wuk"].shape), _const_spec(W["wuv"].shape),
                  _const_spec(W["g128"].shape), _const_spec(W["gains"].shape)],
        out_specs=[tok(GW), tok(GW)],
        out_shape=[jax.ShapeDtypeStruct((t, GW), BF16)] * 2,
        compiler_params=_cp(("parallel",)),
        name="mla_cache_kv",
    )(ckv2, kp2, W["wuk"], W["wuv"], W["g128"], W["gains"])


def _diff_lambda(lam_ref, lam_init):
    lf = lam_ref[...]
    return (jnp.exp(jnp.sum(lf[0:1] * lf[1:2], axis=-1, keepdims=True))
            - jnp.exp(jnp.sum(lf[2:3] * lf[3:4], axis=-1, keepdims=True)) + lam_init)


def _stream_queries(q, mode):
    if mode == "full":
        return [q]
    lane = lax.broadcasted_iota(jnp.int32, (1, LANES), 1)
    zero = jnp.zeros_like(q)
    return [jnp.where(lane < 64, q, zero), jnp.where(lane < 64, zero, q)]


def _combine_streams(outs, mode, lam, subg_ref, lam_init):
    if mode == "full":
        return outs[0]
    if mode == "pair":
        lane = lax.broadcasted_iota(jnp.int32, (1, LANES), 1)
        return jnp.where(lane < 64, outs[0], outs[1])
    o = outs[0] - lam * outs[1]
    o = o * lax.rsqrt(jnp.mean(o * o, axis=-1, keepdims=True) + EPS) * subg_ref[...]
    return o * (1.0 - lam_init)


def _online_attend(q_ref, srcs, chunks, mode, lam, subg_ref, lam_init, o_ref):
    tq = q_ref.shape[1]
    nstream = 1 if mode == "full" else 2
    for slot in range(NSLOT):
        sl = slice(slot * LANES, (slot + 1) * LANES)
        qs = _stream_queries(q_ref[0, :, sl], mode)

        def step(k_c, v_cs, state):
            new = []
            for si in range(nstream):
                m, l, acc = state[3 * si:3 * si + 3]
                s = lax.dot_general(qs[si], k_c, (((1,), (1,)), ((), ())), preferred_element_type=F32)
                m_new = jnp.maximum(m, jnp.max(s, axis=-1, keepdims=True))
                alpha = jnp.exp2(m - m_new)
                p = jnp.exp2(s - m_new)
                l = alpha * l + jnp.sum(p, axis=-1, keepdims=True)
                acc = alpha * acc + jnp.dot(p.astype(BF16), v_cs[si % len(v_cs)],
                                            preferred_element_type=F32)
                new += [m_new, l, acc]
            return tuple(new)

        state = ()
        for _ in range(nstream):
            state += (jnp.full((tq, 1), NEG, F32), jnp.zeros((tq, 1), F32), jnp.zeros((tq, LANES), F32))
        for (k_ref, v_refs), (n_chunk, tk) in zip(srcs, chunks):
            if n_chunk == 1:
                state = step(k_ref[0, :, sl], [v[0, :, sl] for v in v_refs], state)
            else:
                def body(c, st, k_ref=k_ref, v_refs=v_refs, tk=tk):
                    rows = pl.ds(pl.multiple_of(c * tk, tk), tk)
                    return step(k_ref[0, rows, sl], [v[0, rows, sl] for v in v_refs], st)
                state = lax.fori_loop(0, n_chunk, body, state)

        outs = [state[3 * si + 2] / state[3 * si + 1] for si in range(nstream)]
        o_ref[0, :, sl] = _combine_streams(outs, mode, lam, subg_ref, lam_init).astype(o_ref.dtype)


def _attn_kernel(*refs, mode, chunks, lam_init):
    it = iter(refs)
    q_ref = next(it)
    srcs = []
    for _ in chunks:
        k_ref = next(it)
        srcs.append((k_ref, (next(it),)))
    lam, subg_ref = None, None
    if mode == "diff":
        lam = _diff_lambda(next(it), lam_init)
        subg_ref = next(it)
    o_ref = next(it)
    _online_attend(q_ref, srcs, chunks, mode, lam, subg_ref, lam_init, o_ref)


AUG = 2 * LANES
SHIFT_MARGIN = 1.0 + 2.0 ** -6
SHIFT_LIMIT = 48.0


def _attn_fast_kernel(*refs, mode, chunks, lam_init, head_dim):
    it = iter(refs)
    q_ref, kc_ref, vc_ref, k_ref, v_ref, qgain_ref = (next(it) for _ in range(6))
    lam, subg_ref = None, None
    if mode == "diff":
        lam = _diff_lambda(next(it), lam_init)
        subg_ref = next(it)
    o_ref, kaug_ref, vext_ref, shift_ref = (next(it) for _ in range(4))

    tq = q_ref.shape[1]
    pc, n = kc_ref.shape[1], k_ref.shape[1]
    kt = pc + n
    nstream = 1 if mode == "full" else 2
    lane = lax.broadcasted_iota(jnp.int32, (1, LANES), 1)

    @pl.when(pl.program_id(1) == 0)
    def _():
        q_bound = math.sqrt(head_dim) * jnp.max(jnp.abs(qgain_ref[...]), axis=-1, keepdims=True)
        shift_ref[...] = jnp.zeros_like(shift_ref)
        for slot in range(NSLOT):
            sl = slice(slot * LANES, (slot + 1) * LANES)
            a0 = slot * AUG
            kaug_ref[0:pc, a0:a0 + LANES] = kc_ref[0, :, sl]
            kaug_ref[pc:kt, a0:a0 + LANES] = k_ref[0, :, sl]
            vext_ref[0:pc, a0:a0 + LANES] = vc_ref[0, :, sl]
            vext_ref[pc:kt, a0:a0 + LANES] = v_ref[0, :, sl]
            vext_ref[:, a0 + LANES:a0 + AUG] = jnp.ones((kt, LANES), BF16)
            shift_lanes = jnp.zeros((1, LANES), F32)
            for si in range(nstream):
                best = None
                for src in (kc_ref, k_ref):
                    kk = _stream_queries(src[0, :, sl], mode)[si].astype(F32)
                    nrm = jnp.max(jnp.sum(kk * kk, axis=-1, keepdims=True), axis=0, keepdims=True)
                    best = nrm if best is None else jnp.maximum(best, nrm)
                shift = q_bound * jnp.sqrt(best) * SHIFT_MARGIN
                row = slot * nstream + si
                shift_ref[row:row + 1, :] = jnp.broadcast_to(shift, (1, LANES))
                shift_lanes = jnp.where(lane == si, -shift, shift_lanes)
            kaug_ref[:, a0 + LANES:a0 + AUG] = jnp.broadcast_to(shift_lanes.astype(BF16), (kt, LANES))

    safe = jnp.max(shift_ref[...]) <= SHIFT_LIMIT

    @pl.when(safe)
    def _():
        for slot in range(NSLOT):
            sl = slice(slot * LANES, (slot + 1) * LANES)
            a0 = slot * AUG
            outs = []
            for si, qm in enumerate(_stream_queries(q_ref[0, :, sl], mode)):
                pick = jnp.broadcast_to(jnp.where(lane == si, 1.0, 0.0).astype(BF16), (tq, LANES))
                qa = jnp.concatenate([qm, pick], axis=1)
                s = lax.dot_general(qa, kaug_ref[:, a0:a0 + AUG], (((1,), (1,)), ((), ())),
                                    preferred_element_type=F32)
                acc = jnp.dot(jnp.exp2(s).astype(BF16), vext_ref[:, a0:a0 + AUG],
                              preferred_element_type=F32)
                outs.append(acc[:, 0:LANES] / acc[:, LANES:AUG])
            o = _combine_streams(outs, mode, lam, subg_ref, lam_init)
            o_ref[0, :, sl] = o.astype(o_ref.dtype)

    @pl.when(jnp.logical_not(safe))
    def _():
        _online_attend(q_ref, [(kc_ref, (vc_ref,)), (k_ref, (v_ref,))], chunks, mode, lam, subg_ref,
                       lam_init, o_ref)


def _attention_latent(mode, q, kc, vc, k, v, qgain, head_dim, lam=None, subg=None, lam_init=0.0,
                      tq=256, tk=512):
    b, n, _ = q.shape
    pc = kc.shape[1]
    kt = pc + n
    res = lambda m: pl.BlockSpec((1, m, GW), lambda bi, qi: (bi, 0, 0))
    in_specs = [pl.BlockSpec((1, tq, GW), lambda bi, qi: (bi, qi, 0)), res(pc), res(pc), res(n), res(n),
                _const_spec(qgain.shape)]
    args = [q, kc, vc, k, v, qgain]
    if mode == "diff":
        in_specs += [_const_spec(lam.shape), _const_spec(subg.shape)]
        args += [lam, subg]
    chunks = ((1, pc), (n // tk, tk))
    return pl.pallas_call(
        functools.partial(_attn_fast_kernel, mode=mode, chunks=chunks, lam_init=lam_init, head_dim=head_dim),
        grid=(b, n // tq),
        in_specs=in_specs,
        out_specs=pl.BlockSpec((1, tq, GW), lambda bi, qi: (bi, qi, 0)),
        out_shape=jax.ShapeDtypeStruct((b, n, GW), BF16),
        scratch_shapes=[pltpu.VMEM((kt, NSLOT * AUG), BF16), pltpu.VMEM((kt, NSLOT * AUG), BF16),
                        pltpu.VMEM((8, LANES), F32)],
        compiler_params=_cp(("parallel", "arbitrary")),
        name="attn_latent_" + mode,
    )(*args)


def _attention(mode, q, srcs, tq, tk, lam=None, subg=None, lam_init=0.0):
    b, n, _ = q.shape
    in_specs = [pl.BlockSpec((1, tq, GW), lambda bi, qi: (bi, qi, 0))]
    args = [q]
    chunks = []
    for src in srcs:
        ks = src[0].shape[1]
        step = min(tk, ks)
        chunks.append((ks // step, step))
        for a in src:
            in_specs.append(pl.BlockSpec((1, ks, GW), lambda bi, qi: (bi, 0, 0)))
            args.append(a)
    if mode == "diff":
        in_specs += [_const_spec(lam.shape), _const_spec(subg.shape)]
        args += [lam, subg]
    return pl.pallas_call(
        functools.partial(_attn_kernel, mode=mode, chunks=tuple(chunks), lam_init=lam_init),
        grid=(b, n // tq),
        in_specs=in_specs,
        out_specs=pl.BlockSpec((1, tq, GW), lambda bi, qi: (bi, qi, 0)),
        out_shape=jax.ShapeDtypeStruct((b, n, GW), BF16),
        compiler_params=_cp(("parallel", "parallel")),
        name="attn_" + mode,
    )(*args)


NA_QROWS = 4
NA_KROWS = NA_QROWS + NA_ROWS


def _na_key_start(r0, rows):
    return jnp.clip(r0 - NA_ROWS // 2, 0, rows - NA_KROWS)


def _na_kernel(q_ref, k_ref, v_ref, kc_ref, vc_ref, bias_ref, o_ref, *, rows):
    r0 = pl.program_id(1) * NA_QROWS
    ks = _na_key_start(r0, rows)
    loc = pl.ds(pl.multiple_of(ks * GRID_W, GRID_W), NA_KROWS * GRID_W)
    lane = lax.broadcasted_iota(jnp.int32, (1, LANES), 1)
    nt = (((1,), (1,)), ((), ()))
    for slot in range(NSLOT):
        sl = slice(slot * LANES, (slot + 1) * LANES)
        q = q_ref[0, :, sl]
        k_loc = k_ref[0, loc, sl]
        k_ctx = kc_ref[0, :, sl]
        v_loc = v_ref[0, loc, sl]
        v_ctx = vc_ref[0, :, sl]
        v_loc = jnp.concatenate([v_loc, jnp.ones_like(v_loc)], axis=1)
        v_ctx = jnp.concatenate([v_ctx, jnp.ones_like(v_ctx)], axis=1)
        halves = []
        for half, qh in enumerate(_stream_queries(q, "pair")):
            s_loc = lax.dot_general(qh, k_loc, nt, preferred_element_type=F32) + bias_ref[0, 2 * slot + half]
            s_ctx = lax.dot_general(qh, k_ctx, nt, preferred_element_type=F32)
            m = jnp.maximum(jnp.max(s_loc, axis=-1, keepdims=True), jnp.max(s_ctx, axis=-1, keepdims=True))
            acc = (jnp.dot(jnp.exp2(s_loc - m).astype(BF16), v_loc, preferred_element_type=F32)
                   + jnp.dot(jnp.exp2(s_ctx - m).astype(BF16), v_ctx, preferred_element_type=F32))
            halves.append(acc[:, 0:LANES] / acc[:, LANES:2 * LANES])
        o_ref[0, :, sl] = jnp.where(lane < 64, halves[0], halves[1]).astype(o_ref.dtype)


def _na_bias_table(bias_tab, rows):
    assert rows >= NA_KROWS + NA_QROWS and rows % NA_QROWS == 0
    qc = np.arange(GRID_W)[:, None]
    kc = np.arange(GRID_W)[None, :]
    cstart = np.clip(qc - NA_COLS // 2, 0, GRID_W - NA_COLS)
    in_win = (kc >= cstart) & (kc < cstart + NA_COLS)
    dc_idx = np.clip(kc - qc, 1 - NA_COLS, NA_COLS - 1) + NA_COLS - 1
    col_pick = (dc_idx[:, :, None] == np.arange(2 * NA_COLS - 1)).astype(np.float32)
    variants = (0, NA_ROWS // 2, NA_ROWS)
    row_pick = np.zeros((len(variants), NA_QROWS, NA_KROWS, 2 * NA_ROWS - 1), np.float32)
    for vi, var in enumerate(variants):
        for j in range(NA_QROWS):
            first = min(max(j + var - NA_ROWS // 2, 0), NA_KROWS - NA_ROWS)
            for i in range(first, first + NA_ROWS):
                row_pick[vi, j, i, i - j - var + NA_ROWS - 1] = 1.0
    valid = (row_pick.sum(-1) > 0)[:, None, :, None, :, None] & in_win[None, None, None, :, None, :]
    toep = jnp.einsum("qkd,hrd->hrqk", jnp.asarray(col_pick), bias_tab.astype(F32),
                      precision=lax.Precision.HIGHEST)
    tab = jnp.einsum("vjir,hrqk->vhjqik", jnp.asarray(row_pick), toep, precision=lax.Precision.HIGHEST)
    tab = jnp.where(jnp.asarray(valid), tab * LOG2E, NEG)
    return tab.reshape(len(variants), NA_H, NA_QROWS * GRID_W, NA_KROWS * GRID_W)


def _na_latent(q, k, v, kc, vc, bias):
    b, n, _ = q.shape
    rows = n // GRID_W
    pc = kc.shape[1]
    full = lambda m: pl.BlockSpec((1, m, GW), lambda bi, r: (bi, 0, 0))
    tq = NA_QROWS * GRID_W
    return pl.pallas_call(
        functools.partial(_na_kernel, rows=rows),
        grid=(b, rows // NA_QROWS),
        in_specs=[pl.BlockSpec((1, tq, GW), lambda bi, r: (bi, r, 0)),
                  full(n), full(n), full(pc), full(pc),
                  pl.BlockSpec((1, NA_H, tq, NA_KROWS * GRID_W),
                               lambda bi, r: ((r * NA_QROWS - _na_key_start(r * NA_QROWS, rows))
                                              // (NA_ROWS // 2), 0, 0, 0))],
        out_specs=pl.BlockSpec((1, tq, GW), lambda bi, r: (bi, r, 0)),
        out_shape=jax.ShapeDtypeStruct((b, n, GW), BF16),
        compiler_params=_cp(("parallel", "parallel")),
        name="attn_neighbourhood",
    )(q, k, v, kc, vc, bias)


POOL_HALO = 64


def _pool_kernel(prev_ref, main_ref, next_ref, w_ref, scale_ref, o_ref, *, seq):
    tp = main_ref.shape[1]
    t0 = pl.program_id(1) * tp
    ext = jnp.concatenate([prev_ref[0], main_ref[0], next_ref[0]], axis=0)
    ext_hi = ext.astype(BF16)
    ext_lo = (ext - ext_hi.astype(F32)).astype(BF16)
    t = t0 + lax.broadcasted_iota(jnp.int32, (tp, 1), 0)
    pos = t0 - POOL_HALO + lax.broadcasted_iota(jnp.int32, (1, tp + 2 * POOL_HALO), 1)
    for g, win in enumerate(POOL_WINDOWS):
        sl = slice(g * POOL_C, (g + 1) * POOL_C)
        lo = jnp.clip(t - win // 2, 0, seq)
        hi = jnp.clip(t - win // 2 + win, 0, seq)
        band = jnp.where((pos >= lo) & (pos < hi), 1.0, 0.0).astype(BF16)
        tot = (jnp.dot(band, ext_hi[:, sl], preferred_element_type=F32)
               + jnp.dot(band, ext_lo[:, sl], preferred_element_type=F32))
        mean = tot / (hi - lo).astype(F32)
        d = (mean - main_ref[0, :, sl]).astype(BF16)
        y = jnp.dot(d, w_ref[g], preferred_element_type=F32) * scale_ref[:, sl]
        o_ref[0, :, sl] = y.astype(o_ref.dtype)


def _pool(pu, w_pool, scale, tp=256):
    b, n, _ = pu.shape
    hb = tp // POOL_HALO
    last = n // POOL_HALO - 1
    return pl.pallas_call(
        functools.partial(_pool_kernel, seq=n),
        grid=(b, n // tp),
        in_specs=[pl.BlockSpec((1, POOL_HALO, GW), lambda bi, i: (bi, jnp.maximum(i * hb - 1, 0), 0)),
                  pl.BlockSpec((1, tp, GW), lambda bi, i: (bi, i, 0)),
                  pl.BlockSpec((1, POOL_HALO, GW), lambda bi, i: (bi, jnp.minimum((i + 1) * hb, last), 0)),
                  _const_spec(w_pool.shape), _const_spec(scale.shape)],
        out_specs=pl.BlockSpec((1, tp, GW), lambda bi, i: (bi, i, 0)),
        out_shape=jax.ShapeDtypeStruct((b, n, GW), BF16),
        compiler_params=_cp(("parallel", "parallel")),
        name="pool_mixer",
    )(pu, pu, pu, w_pool, scale)


def _outproj_kernel(oa_ref, ob_ref, oc_ref, od_ref, x_ref, mod_ref, g2_ref, w_ref, o_ref, h_ref):
    mix = jnp.dot(oa_ref[...], w_ref[0:GW, :], preferred_element_type=F32)
    mix += jnp.dot(ob_ref[...], w_ref[GW:2 * GW, :], preferred_element_type=F32)
    mix += jnp.dot(oc_ref[...], w_ref[2 * GW:3 * GW, :], preferred_element_type=F32)
    mix += jnp.dot(od_ref[...], w_ref[3 * GW:4 * GW, :], preferred_element_type=F32)
    mod = mod_ref[0]
    x = x_ref[...] + mod[2:3, :] * mix
    o_ref[...] = x
    h_ref[...] = _modulated(x, g2_ref[...], mod, 3).astype(BF16)


def _outproj(oa, ob, oc, od, x2, mod, mod_row, g2, w_out, tm=256):
    t, d = x2.shape
    tok = lambda w: pl.BlockSpec((tm, w), lambda i: (i, 0))
    return pl.pallas_call(
        _outproj_kernel,
        grid=(t // tm,),
        in_specs=[tok(GW)] * 4 + [tok(d), pl.BlockSpec((1, N_MOD, d), lambda i: (mod_row(i), 0, 0)),
                                   _const_spec((1, d)), _const_spec(w_out.shape)],
        out_specs=[tok(d), tok(d)],
        out_shape=[jax.ShapeDtypeStruct((t, d), F32), jax.ShapeDtypeStruct((t, d), BF16)],
        compiler_params=_cp(("parallel",)),
        name="outproj",
    )(oa, ob, oc, od, x2, mod, g2, w_out)


FFN_HALO = 16


def _ffn_kernel(prev_ref, hm_ref, next_ref, x_ref, mod_ref, wg_ref, wv_ref, cwg_ref, cwv_ref, cbg_ref,
                cbv_ref, wd_ref, o_ref, h_ref, ug_ref, uv_ref, *, seq):
    i = pl.program_id(0)
    j = pl.program_id(1)
    tm = x_ref.shape[0]
    tiles_per_seq = max(seq // tm, 1)

    @pl.when(j == 0)
    def _():
        first = (i % tiles_per_seq) == 0
        last = (i % tiles_per_seq) == tiles_per_seq - 1
        h_ref[0:FFN_HALO, :] = jnp.where(first, jnp.zeros_like(prev_ref), prev_ref[...])
        h_ref[FFN_HALO:FFN_HALO + tm, :] = hm_ref[...]
        h_ref[FFN_HALO + tm:, :] = jnp.where(last, jnp.zeros_like(next_ref), next_ref[...])
        o_ref[...] = jnp.zeros_like(o_ref)

    h = h_ref[...]
    if seq < tm:
        pos = lax.broadcasted_iota(jnp.int32, (tm, 1), 0)
        seq_start = (pos % seq) == 0
        seq_end = (pos % seq) == seq - 1

    def conv(w_ref, cw_ref, cb_ref, u_ref):
        u_ref[...] = jnp.dot(h, w_ref[...], preferred_element_type=F32)
        cw = cw_ref[...]
        before = u_ref[FFN_HALO - 1:FFN_HALO - 1 + tm, :]
        after = u_ref[FFN_HALO + 1:FFN_HALO + 1 + tm, :]
        if seq < tm:
            before = jnp.where(seq_start, 0.0, before)
            after = jnp.where(seq_end, 0.0, after)
        return (before * cw[0:1] + u_ref[FFN_HALO:FFN_HALO + tm, :] * cw[1:2] + after * cw[2:3]
                + cb_ref[...])

    gate = conv(wg_ref, cwg_ref, cbg_ref, ug_ref)
    val = conv(wv_ref, cwv_ref, cbv_ref, uv_ref)
    a = (gate * jax.nn.sigmoid(gate) * val).astype(BF16)
    o_ref[...] += jnp.dot(a, wd_ref[...], preferred_element_type=F32)

    @pl.when(j == pl.num_programs(1) - 1)
    def _():
        o_ref[...] = x_ref[...] + mod_ref[0, 5:6, :] * o_ref[...]


def _ffn(x2, h2, mod, mod_row, w_up, conv_w, conv_b, w_down, seq, tm, tn=512):
    t, d = x2.shape
    dff = w_down.shape[0]
    assert t % tm == 0 and (seq % tm == 0 or tm % seq == 0)
    nj = dff // tn
    hb = tm // FFN_HALO
    nhb = t // FFN_HALO
    return pl.pallas_call(
        functools.partial(_ffn_kernel, seq=seq),
        grid=(t // tm, nj),
        in_specs=[pl.BlockSpec((FFN_HALO, d), lambda i, j: (jnp.maximum(i * hb - 1, 0), 0)),
                  pl.BlockSpec((tm, d), lambda i, j: (i, 0)),
                  pl.BlockSpec((FFN_HALO, d), lambda i, j: (jnp.minimum((i + 1) * hb, nhb - 1), 0)),
                  pl.BlockSpec((tm, d), lambda i, j: (i, 0)),
                  pl.BlockSpec((1, N_MOD, d), lambda i, j: (mod_row(i), 0, 0)),
                  pl.BlockSpec((d, tn), lambda i, j: (0, j)),
                  pl.BlockSpec((d, tn), lambda i, j: (0, nj + j)),
                  pl.BlockSpec((CONV_W, tn), lambda i, j: (0, j)),
                  pl.BlockSpec((CONV_W, tn), lambda i, j: (0, nj + j)),
                  pl.BlockSpec((1, tn), lambda i, j: (0, j)),
                  pl.BlockSpec((1, tn), lambda i, j: (0, nj + j)),
                  pl.BlockSpec((tn, d), lambda i, j: (j, 0))],
        out_specs=pl.BlockSpec((tm, d), lambda i, j: (i, 0)),
        out_shape=jax.ShapeDtypeStruct((t, d), F32),
        scratch_shapes=[pltpu.VMEM((tm + 2 * FFN_HALO, d), BF16),
                        pltpu.VMEM((tm + 2 * FFN_HALO, tn), F32), pltpu.VMEM((tm + 2 * FFN_HALO, tn), F32)],
        compiler_params=_cp(("parallel", "arbitrary")),
        name="conv_ffn",
    )(h2, h2, h2, x2, mod, w_up, w_up, conv_w, conv_w, conv_b, conv_b, w_down)


def _block_ones(group):
    idx = np.arange(GW) // group
    return jnp.asarray(idx[:, None] == idx[None, :], dtype=BF16)


def _pad_heads(w, heads, width):
    lead = w.shape[:-1]
    w = w.reshape(lead + (heads, width))
    w = jnp.pad(w, [(0, 0)] * len(lead) + [(0, 0), (0, LANES - width)])
    return w.reshape(lead + (heads * LANES,))


def _layer_weights(P, l):
    d = P["w_in"].shape[1]
    w_in = P["w_in"][l]
    o = np.cumsum((0, 512, 512, 512, MLA_Q_LORA, MLA_KV_LORA, MLA_ROPE, 512, 512, 512, 512))
    col = lambda k: w_in[:, o[k]:o[k + 1]]
    w_all = jnp.concatenate(
        [col(0), col(1), col(2), col(6), col(7), col(8), col(9), col(3), col(4), col(5),
         jnp.zeros((d, ML_W - MLA_Q_LORA - MLA_KV_LORA - MLA_ROPE), F32)], axis=1).astype(BF16)
    wukv = P["mla_w_ukv"][l].reshape(MLA_KV_LORA, MLA_H, MLA_NOPE + MLA_V)
    wuk = _pad_heads(wukv[:, :, :MLA_NOPE].reshape(MLA_KV_LORA, MLA_H * MLA_NOPE), MLA_H, MLA_NOPE)
    wuv = wukv[:, :, MLA_NOPE:].reshape(MLA_KV_LORA, MLA_H * MLA_V)
    tile = lambda g, reps: jnp.tile(g.astype(F32), reps)
    padw = lambda g: jnp.pad(g.astype(F32), (0, GW - g.shape[0]))
    gains = jnp.stack([
        tile(P["diff_qn_g"][l], 8) * (DIFF_DH ** -0.5 * LOG2E),
        tile(P["diff_kn_g"][l], 8),
        tile(P["na_qn_g"][l], 8) * (NA_DH ** -0.5 * LOG2E),
        tile(P["na_kn_g"][l], 8),
        _pad_heads(tile(P["mla_qn_g"][l], MLA_H), MLA_H, MLA_QK) * (MLA_QK ** -0.5 * LOG2E),
        _pad_heads(tile(P["mla_kn_g"][l], MLA_H), MLA_H, MLA_QK),
        padw(P["mla_qa_g"][l]),
        padw(P["mla_kva_g"][l]),
    ])
    return dict(
        w_in=w_all,
        wuq=_pad_heads(P["mla_w_uq"][l], MLA_H, MLA_QK).astype(BF16),
        wuk=wuk.astype(BF16), wuv=wuv.astype(BF16),
        gains=gains, g64=_block_ones(64), g128=_block_ones(LANES),
        norm1=P["norm1_g"][l].reshape(1, d).astype(F32),
        norm2=P["norm2_g"][l].reshape(1, d).astype(F32),
        lam=P["diff_lam"][l].astype(F32),
        subg=P["diff_sub_g"][l].reshape(1, 2 * DIFF_DH).astype(F32),
        pool_w=P["pool_w"][l].astype(BF16),
        pool_scale=P["pool_scale"][l].reshape(1, GW).astype(F32),
        w_out=P["w_out"][l].astype(BF16),
        w_up=P["w_up"][l].astype(BF16),
        conv_w=P["conv_w"][l].astype(F32),
        conv_b=P["conv_b"][l].reshape(1, -1).astype(F32),
        w_down=P["w_down"][l].astype(BF16),
    )


def _rope_tables(n):
    t = jnp.arange(n)
    rows = (t // GRID_W).astype(F32)
    cols = (t % GRID_W).astype(F32)

    def ang(dim):
        quarter = dim // 4
        inv = ROPE_BASE ** (-jnp.arange(quarter, dtype=F32) / quarter)
        return jnp.concatenate([rows[:, None] * inv, cols[:, None] * inv], axis=-1)

    a = ang(DIFF_DH)
    cos_a, sin_a = jnp.cos(a), jnp.sin(a)
    cd = jnp.tile(cos_a, (1, 4))
    sd = jnp.tile(jnp.concatenate([-sin_a, sin_a], axis=1), (1, 2))
    b = ang(MLA_ROPE)
    cos_b, sin_b = jnp.cos(b), jnp.sin(b)
    ones = jnp.ones((n, MLA_NOPE), F32)
    pad = LANES - MLA_NOPE - MLA_ROPE
    cm = jnp.concatenate([ones, cos_b, cos_b, jnp.ones((n, pad), F32)], axis=1)
    sm = jnp.concatenate([0.0 * ones, -sin_b, sin_b, jnp.zeros((n, pad), F32)], axis=1)
    return jnp.stack([cd, sd, cm, sm])


def _layer(x, mod, mod_row_of_batch, W, layer, cache, rope_tab, bias_tab):
    b, n, d = x.shape
    t = b * n
    x2 = x.reshape(t, d)
    lam_init = 0.8 - 0.6 * math.exp(-0.3 * layer)
    tm = min(512, n)
    row_in = lambda i: mod_row_of_batch(i // (n // tm))
    outs = _inproj(x2, mod, row_in, W["norm1"], W, rope_tab, n, states=cache is None, tm=tm)
    qd, kd, vd, qn, kn, vn, pu, qm, km, vm = [a.reshape(b, n, GW) for a in outs[:10]]
    if cache is None:
        oa = _attention("diff", qd, [(kd, vd)], n, n, W["lam"], W["subg"], lam_init)
        ob = _attention("full", qm, [(km, vm)], n, n)
        oc = _attention("pair", qn, [(kn, vn)], n, n)
        state = outs[10:]
    else:
        a_k, a_v, b_ckv, b_kpe, c_k, c_v = cache
        pc = a_k.shape[1]
        ck = a_k.reshape(b, pc, GW).astype(BF16)
        cv = a_v.reshape(b, pc, GW).astype(BF16)
        oa = _attention_latent("diff", qd, ck, cv, kd, vd, W["gains"][0:1], DIFF_DH, W["lam"], W["subg"],
                               lam_init)
        kpe_pad = jnp.pad(b_kpe.reshape(b * pc, MLA_ROPE), ((0, 0), (0, LANES - MLA_ROPE)))
        kmc, vmc = _mla_cache(b_ckv.reshape(b * pc, MLA_KV_LORA), kpe_pad, W)
        ob = _attention_latent("full", qm, kmc.reshape(b, pc, GW), vmc.reshape(b, pc, GW), km, vm,
                               W["gains"][4:5], MLA_QK)
        nk = c_k.reshape(b, pc, GW).astype(BF16)
        nv = c_v.reshape(b, pc, GW).astype(BF16)
        oc = _na_latent(qn, kn, vn, nk, nv, bias_tab)
        state = ()
    od = _pool(pu, W["pool_w"], W["pool_scale"])
    flat = lambda a: a.reshape(t, GW)
    x2, h2 = _outproj(flat(oa), flat(ob), flat(oc), flat(od), x2, mod, row_in, W["norm2"], W["w_out"], tm=tm)
    tmf = min(512, t) if cache is None else min(512, n)
    row_ffn = lambda i: mod_row_of_batch((i * tmf) // n)
    x2 = _ffn(x2, h2, mod, row_ffn, W["w_up"], W["conv_w"], W["conv_b"], W["w_down"], n, tm=tmf)
    return x2.reshape(b, n, d), state


def kernel(x_prompt, x_sample, cache_diff_k, cache_diff_v, cache_mla_ckv, cache_mla_kpe, cache_na_k, cache_na_v, c, c_ctx, norm1_g, norm2_g, ada_w, ada_b, w_in, diff_qn_g, diff_kn_g, diff_lam, diff_sub_g, mla_qa_g, mla_kva_g, mla_w_uq, mla_w_ukv, mla_qn_g, mla_kn_g, na_qn_g, na_kn_g, na_bias, pool_w, pool_scale, w_out, w_up, conv_w, conv_b, w_down):
    P = dict(norm1_g=norm1_g, norm2_g=norm2_g, w_in=w_in, diff_qn_g=diff_qn_g, diff_kn_g=diff_kn_g,
             diff_lam=diff_lam, diff_sub_g=diff_sub_g, mla_qa_g=mla_qa_g, mla_kva_g=mla_kva_g,
             mla_w_uq=mla_w_uq, mla_w_ukv=mla_w_ukv, mla_qn_g=mla_qn_g, mla_kn_g=mla_kn_g,
             na_qn_g=na_qn_g, na_kn_g=na_kn_g, pool_w=pool_w, pool_scale=pool_scale, w_out=w_out,
             w_up=w_up, conv_w=conv_w, conv_b=conv_b, w_down=w_down)
    depth = w_in.shape[0]
    d = x_prompt.shape[-1]
    bd, nd = x_sample.shape[:2]
    n_rows = -(-(bd + 1) // 8) * 8
    cv = jnp.concatenate([c, c_ctx[None, :], jnp.zeros((n_rows - bd - 1, d), F32)], axis=0)
    mod_all = _ada(cv, ada_w, ada_b).reshape(depth, n_rows, N_MOD, d)
    rope_tab = _rope_tables(nd)
    Ws = [_layer_weights(P, l) for l in range(depth)]

    xp = x_prompt
    states = []
    for l in range(depth):
        xp, st = _layer(xp, mod_all[l], lambda bi: bd, Ws[l], l, None, None, None)
        states.append(st)
    xs = x_sample
    for l in range(depth):
        cache_l = (cache_diff_k[:, l], cache_diff_v[:, l], cache_mla_ckv[:, l], cache_mla_kpe[:, l],
                   cache_na_k[:, l], cache_na_v[:, l])
        bias_tab = _na_bias_table(na_bias[l], nd // GRID_W)
        xs, _ = _layer(xs, mod_all[l], lambda bi: bi, Ws[l], l, cache_l, rope_tab, bias_tab)

    bp, npr = x_prompt.shape[:2]

    def stack(k, shape):
        return jnp.stack([s[k].reshape((bp, npr) + shape) for s in states], axis=1)

    return (xp, xs,
            stack(0, (DIFF_H, 2, DIFF_DH)), stack(1, (DIFF_H, 2 * DIFF_DH)),
            stack(2, (MLA_KV_LORA,)), stack(3, (MLA_ROPE,)),
            stack(4, (NA_H, NA_DH)), stack(5, (NA_H, NA_DH)))
```

```python
import functools
import math

import numpy as np
import jax
import jax.numpy as jnp
from jax import lax
from jax.experimental import pallas as pl
from jax.experimental.pallas import tpu as pltpu

F32 = jnp.float32
BF16 = jnp.bfloat16

GRID_W = 64
ROPE_BASE = 10000.0
EPS = 1e-6
N_MOD = 6
DIFF_H = 4
DIFF_DH = 64
MLA_H = 4
MLA_NOPE = 64
MLA_ROPE = 32
MLA_V = 128
MLA_Q_LORA = 384
MLA_KV_LORA = 128
NA_H = 8
NA_DH = 64
NA_ROWS = 8
NA_COLS = 16
POOL_WINDOWS = (2, 4, 8, 16)
POOL_G = 4
POOL_C = 128
CONV_W = 3

GW = 512
LANES = 128
NSLOT = GW // LANES
MXU_DIM = 256
MLA_QK = MLA_NOPE + MLA_ROPE
V7X_VMEM_LIMIT = 56 * 1024 * 1024
NEG = -1e30
LOG2E = math.log2(math.e)

C_DQ, C_DK, C_DV, C_NQ, C_NK, C_NV, C_PU, C_ML = 0, 512, 1024, 1536, 2048, 2560, 3072, 3584
ML_W = 640
IN_WP = C_ML + ML_W


def _cp(sem, vmem=V7X_VMEM_LIMIT):
    return pltpu.CompilerParams(dimension_semantics=sem, vmem_limit_bytes=vmem)


def _const_spec(shape):
    nd = len(shape)
    return pl.BlockSpec(shape, lambda *_: (0,) * nd, pipeline_mode=pl.Buffered(1))


def _ada_kernel(c_ref, w_ref, b_ref, o_ref):
    c = c_ref[...]
    s = c * jax.nn.sigmoid(c)
    o_ref[0] = jnp.dot(s.astype(BF16), w_ref[0].astype(BF16), preferred_element_type=F32) + b_ref[0]


def _ada(cv, ada_w, ada_b, tn=1024):
    depth, d, n = ada_w.shape
    r = cv.shape[0]
    return pl.pallas_call(
        _ada_kernel,
        grid=(depth, n // tn),
        in_specs=[pl.BlockSpec((r, d), lambda l, j: (0, 0)),
                  pl.BlockSpec((1, d, tn), lambda l, j: (l, 0, j)),
                  pl.BlockSpec((1, 1, tn), lambda l, j: (l, 0, j))],
        out_specs=pl.BlockSpec((1, r, tn), lambda l, j: (l, 0, j)),
        out_shape=jax.ShapeDtypeStruct((depth, r, n), F32),
        compiler_params=_cp(("parallel", "parallel")),
        name="adaln",
    )(cv, ada_w, ada_b.reshape(depth, 1, n))


def _modulated(x, g, mod, k):
    ms = jnp.mean(x * x, axis=-1, keepdims=True)
    y = x * lax.rsqrt(ms + EPS) * g
    return y * (1.0 + mod[k + 1:k + 2, :]) + mod[k:k + 1, :]


def _group_rms(z, gsum_ref, inv_n, gain):
    sq = (z * z).astype(BF16)
    ss = jnp.concatenate(
        [jnp.dot(sq[:, c:c + MXU_DIM], gsum_ref[c:c + MXU_DIM, c:c + MXU_DIM], preferred_element_type=F32)
         for c in range(0, z.shape[1], MXU_DIM)], axis=1)
    return z * lax.rsqrt(ss * inv_n + EPS) * gain


def _rope_chunks(z, cos_ref, sin_ref, lo_pred, shift_lo, shift_hi):
    c = cos_ref[...]
    s = sin_ref[...]
    outs = []
    for j in range(z.shape[1] // LANES):
        zc = z[:, j * LANES:(j + 1) * LANES]
        partner = jnp.where(lo_pred, pltpu.roll(zc, shift_lo, 1), pltpu.roll(zc, shift_hi, 1))
        outs.append(zc * c + partner * s)
    return jnp.concatenate(outs, axis=1)


def _mla_kv(ckv_b, kp, wuk_ref, wuv_ref, g128_ref, kgain):
    kp_slot = pltpu.roll(kp, MLA_NOPE, 1)
    kh = jnp.dot(ckv_b, wuk_ref[...], preferred_element_type=F32) + jnp.concatenate([kp_slot] * MLA_H, axis=1)
    k = _group_rms(kh, g128_ref, 1.0 / MLA_QK, kgain)
    v = jnp.dot(ckv_b, wuv_ref[...], preferred_element_type=F32)
    return k, v


def _inproj_kernel(*refs, rope, states):
    it = iter(refs)
    x_ref, mod_ref, g_ref, w_ref, wuq_ref, wuk_ref, wuv_ref = (next(it) for _ in range(7))
    gains_ref, g64_ref, g128_ref = (next(it) for _ in range(3))
    if rope:
        rope_ref = next(it)
    qd_ref, kd_ref, vd_ref, qn_ref, kn_ref, vn_ref, pu_ref, qm_ref, km_ref, vm_ref = (
        next(it) for _ in range(10))
    if states:
        ska_ref, sva_ref, sckv_ref, skpe_ref, skc_ref, svc_ref = (next(it) for _ in range(6))

    h = _modulated(x_ref[...], g_ref[...], mod_ref[0], 0).astype(BF16)
    gains = gains_ref[...]
    lane = lax.broadcasted_iota(jnp.int32, (1, LANES), 1)

    def seg(c0, w=GW):
        return jnp.dot(h, w_ref[:, c0:c0 + w], preferred_element_type=F32)

    def rope_d(z):
        if not rope:
            return z
        return _rope_chunks(z, rope_ref.at[0], rope_ref.at[1], (lane % 64) < 32, LANES - 32, 32)

    def rope_m(z):
        if not rope:
            return z
        return _rope_chunks(z, rope_ref.at[2], rope_ref.at[3], lane < MLA_NOPE + MLA_ROPE // 2,
                            LANES - MLA_ROPE // 2, MLA_ROPE // 2)

    qd_ref[...] = rope_d(_group_rms(seg(C_DQ), g64_ref, 1.0 / DIFF_DH, gains[0:1])).astype(BF16)
    kd = _group_rms(seg(C_DK), g64_ref, 1.0 / DIFF_DH, gains[1:2])
    kd_ref[...] = rope_d(kd).astype(BF16)
    vd = seg(C_DV)
    vd_ref[...] = vd.astype(BF16)
    qn_ref[...] = _group_rms(seg(C_NQ), g64_ref, 1.0 / NA_DH, gains[2:3]).astype(BF16)
    kn = _group_rms(seg(C_NK), g64_ref, 1.0 / NA_DH, gains[3:4])
    kn_ref[...] = kn.astype(BF16)
    vn = seg(C_NV)
    vn_ref[...] = vn.astype(BF16)
    pu_ref[...] = seg(C_PU)
    zm = seg(C_ML, ML_W)
    cq = zm[:, :MLA_Q_LORA]
    cq = cq * lax.rsqrt(jnp.mean(cq * cq, axis=-1, keepdims=True) + EPS) * gains[6:7, :MLA_Q_LORA]
    qh = jnp.dot(cq.astype(BF16), wuq_ref[...], preferred_element_type=F32)
    qm_ref[...] = rope_m(_group_rms(qh, g128_ref, 1.0 / MLA_QK, gains[4:5])).astype(BF16)
    ck = zm[:, MLA_Q_LORA:MLA_Q_LORA + MLA_KV_LORA]
    ckv = ck * lax.rsqrt(jnp.mean(ck * ck, axis=-1, keepdims=True) + EPS) * gains[7:8, :MLA_KV_LORA]
    kp = zm[:, GW:GW + LANES]
    km, vm = _mla_kv(ckv.astype(BF16), kp, wuk_ref, wuv_ref, g128_ref, gains[5:6])
    km_ref[...] = rope_m(km).astype(BF16)
    vm_ref[...] = vm.astype(BF16)
    if states:
        ska_ref[...] = kd
        sva_ref[...] = vd
        sckv_ref[...] = ckv
        skpe_ref[...] = kp[:, :MLA_ROPE]
        skc_ref[...] = kn
        svc_ref[...] = vn


def _inproj(x2, mod, mod_row, g1, W, rope_tab, seq, states, tm=256):
    t, d = x2.shape
    rope = rope_tab is not None
    tiles_per_seq = seq // tm
    tok = lambda w: pl.BlockSpec((tm, w), lambda i: (i, 0))
    in_specs = [tok(d),
                pl.BlockSpec((1, N_MOD, d), lambda i: (mod_row(i), 0, 0)),
                _const_spec((1, d)),
                _const_spec(W["w_in"].shape), _const_spec(W["wuq"].shape),
                _const_spec(W["wuk"].shape), _const_spec(W["wuv"].shape),
                _const_spec(W["gains"].shape), _const_spec(W["g64"].shape),
                _const_spec(W["g128"].shape)]
    args = [x2, mod, g1, W["w_in"], W["wuq"], W["wuk"], W["wuv"], W["gains"], W["g64"], W["g128"]]
    if rope:
        in_specs.append(pl.BlockSpec((4, tm, LANES), lambda i: (0, i % tiles_per_seq, 0)))
        args.append(rope_tab)
    out_specs = [tok(GW)] * 10
    out_shape = [jax.ShapeDtypeStruct((t, GW), BF16)] * 6 + [jax.ShapeDtypeStruct((t, GW), F32)] + \
                [jax.ShapeDtypeStruct((t, GW), BF16)] * 3
    if states:
        widths = (GW, GW, MLA_KV_LORA, MLA_ROPE, GW, GW)
        out_specs += [tok(w) for w in widths]
        out_shape += [jax.ShapeDtypeStruct((t, w), F32) for w in widths]
    return pl.pallas_call(
        functools.partial(_inproj_kernel, rope=rope, states=states),
        grid=(t // tm,),
        in_specs=in_specs, out_specs=out_specs, out_shape=out_shape,
        compiler_params=_cp(("parallel",)),
        name="inproj",
    )(*args)


def _mla_cache_kernel(ckv_ref, kp_ref, wuk_ref, wuv_ref, g128_ref, gains_ref, k_ref, v_ref):
    k, v = _mla_kv(ckv_ref[...].astype(BF16), kp_ref[...], wuk_ref, wuv_ref, g128_ref, gains_ref[5:6, :])
    k_ref[...] = k.astype(BF16)
    v_ref[...] = v.astype(BF16)


def _mla_cache(ckv2, kp2, W, tm=256):
    t = ckv2.shape[0]
    tok = lambda w: pl.BlockSpec((tm, w), lambda i: (i, 0))
    return pl.pallas_call(
        _mla_cache_kernel,
        grid=(t // tm,),
        in_specs=[tok(MLA_KV_LORA), tok(LANES), _const_spec(W["wuk"].shape), _const_spec(W["wuv"].shape),
                  _const_spec(W["g128"].shape), _const_spec(W["gains"].shape)],
        out_specs=[tok(GW), tok(GW)],
        out_shape=[jax.ShapeDtypeStruct((t, GW), BF16)] * 2,
        compiler_params=_cp(("parallel",)),
        name="mla_cache_kv",
    )(ckv2, kp2, W["wuk"], W["wuv"], W["g128"], W["gains"])


def _diff_lambda(lam_ref, lam_init):
    lf = lam_ref[...]
    return (jnp.exp(jnp.sum(lf[0:1] * lf[1:2], axis=-1, keepdims=True))
            - jnp.exp(jnp.sum(lf[2:3] * lf[3:4], axis=-1, keepdims=True)) + lam_init)


def _stream_queries(q, mode):
    if mode == "full":
        return [q]
    lane = lax.broadcasted_iota(jnp.int32, (1, LANES), 1)
    zero = jnp.zeros_like(q)
    return [jnp.where(lane < 64, q, zero), jnp.where(lane < 64, zero, q)]


def _combine_streams(outs, mode, lam, subg_ref, lam_init):
    if mode == "full":
        return outs[0]
    if mode == "pair":
        lane = lax.broadcasted_iota(jnp.int32, (1, LANES), 1)
        return jnp.where(lane < 64, outs[0], outs[1])
    o = outs[0] - lam * outs[1]
    o = o * lax.rsqrt(jnp.mean(o * o, axis=-1, keepdims=True) + EPS) * subg_ref[...]
    return o * (1.0 - lam_init)


def _online_attend(q_ref, srcs, chunks, mode, lam, subg_ref, lam_init, o_ref):
    tq = q_ref.shape[1]
    nstream = 1 if mode == "full" else 2
    for slot in range(NSLOT):
        sl = slice(slot * LANES, (slot + 1) * LANES)
        qs = _stream_queries(q_ref[0, :, sl], mode)

        def step(k_c, v_cs, state):
            new = []
            for si in range(nstream):
                m, l, acc = state[3 * si:3 * si + 3]
                s = lax.dot_general(qs[si], k_c, (((1,), (1,)), ((), ())), preferred_element_type=F32)
                m_new = jnp.maximum(m, jnp.max(s, axis=-1, keepdims=True))
                alpha = jnp.exp2(m - m_new)
                p = jnp.exp2(s - m_new)
                l = alpha * l + jnp.sum(p, axis=-1, keepdims=True)
                acc = alpha * acc + jnp.dot(p.astype(BF16), v_cs[si % len(v_cs)],
                                            preferred_element_type=F32)
                new += [m_new, l, acc]
            return tuple(new)

        state = ()
        for _ in range(nstream):
            state += (jnp.full((tq, 1), NEG, F32), jnp.zeros((tq, 1), F32), jnp.zeros((tq, LANES), F32))
        for (k_ref, v_refs), (n_chunk, tk) in zip(srcs, chunks):
            if n_chunk == 1:
                state = step(k_ref[0, :, sl], [v[0, :, sl] for v in v_refs], state)
            else:
                def body(c, st, k_ref=k_ref, v_refs=v_refs, tk=tk):
                    rows = pl.ds(pl.multiple_of(c * tk, tk), tk)
                    return step(k_ref[0, rows, sl], [v[0, rows, sl] for v in v_refs], st)
                state = lax.fori_loop(0, n_chunk, body, state)

        outs = [state[3 * si + 2] / state[3 * si + 1] for si in range(nstream)]
        o_ref[0, :, sl] = _combine_streams(outs, mode, lam, subg_ref, lam_init).astype(o_ref.dtype)


def _attn_kernel(*refs, mode, chunks, lam_init):
    it = iter(refs)
    q_ref = next(it)
    srcs = []
    for _ in chunks:
        k_ref = next(it)
        srcs.append((k_ref, (next(it),)))
    lam, subg_ref = None, None
    if mode == "diff":
        lam = _diff_lambda(next(it), lam_init)
        subg_ref = next(it)
    o_ref = next(it)
    _online_attend(q_ref, srcs, chunks, mode, lam, subg_ref, lam_init, o_ref)


AUG = 2 * LANES
SHIFT_MARGIN = 1.0 + 2.0 ** -6
SHIFT_LIMIT = 48.0
ATTN_SUB = 256


def _attn_fast_kernel(*refs, mode, chunks, lam_init, head_dim):
    it = iter(refs)
    q_ref, kc_ref, vc_ref, k_ref, v_ref, qgain_ref = (next(it) for _ in range(6))
    lam, subg_ref = None, None
    if mode == "diff":
        lam = _diff_lambda(next(it), lam_init)
        subg_ref = next(it)
    o_ref, kaug_ref, vext_ref, shift_ref = (next(it) for _ in range(4))

    tq = q_ref.shape[1]
    pc, n = kc_ref.shape[1], k_ref.shape[1]
    kt = pc + n
    nstream = 1 if mode == "full" else 2
    lane = lax.broadcasted_iota(jnp.int32, (1, LANES), 1)

    @pl.when(pl.program_id(1) == 0)
    def _():
        q_bound = math.sqrt(head_dim) * jnp.max(jnp.abs(qgain_ref[...]), axis=-1, keepdims=True)
        shift_ref[...] = jnp.zeros_like(shift_ref)
        for slot in range(NSLOT):
            sl = slice(slot * LANES, (slot + 1) * LANES)
            a0 = slot * AUG
            kaug_ref[0:pc, a0:a0 + LANES] = kc_ref[0, :, sl]
            kaug_ref[pc:kt, a0:a0 + LANES] = k_ref[0, :, sl]
            vext_ref[0:pc, a0:a0 + LANES] = vc_ref[0, :, sl]
            vext_ref[pc:kt, a0:a0 + LANES] = v_ref[0, :, sl]
            vext_ref[:, a0 + LANES:a0 + AUG] = jnp.ones((kt, LANES), BF16)
            shift_lanes = jnp.zeros((1, LANES), F32)
            for si in range(nstream):
                best = None
                for src in (kc_ref, k_ref):
                    kk = _stream_queries(src[0, :, sl], mode)[si].astype(F32)
                    nrm = jnp.max(jnp.sum(kk * kk, axis=-1, keepdims=True), axis=0, keepdims=True)
                    best = nrm if best is None else jnp.maximum(best, nrm)
                shift = q_bound * jnp.sqrt(best) * SHIFT_MARGIN
                row = slot * nstream + si
                shift_ref[row:row + 1, :] = jnp.broadcast_to(shift, (1, LANES))
                shift_lanes = jnp.where(lane == si, -shift, shift_lanes)
            kaug_ref[:, a0 + LANES:a0 + AUG] = jnp.broadcast_to(shift_lanes.astype(BF16), (kt, LANES))

    safe = jnp.max(shift_ref[...]) <= SHIFT_LIMIT

    @pl.when(safe)
    def _():
        def sub_tile(t, carry):
            rows = pl.ds(pl.multiple_of(t * ATTN_SUB, ATTN_SUB), ATTN_SUB)
            for slot in range(NSLOT):
                sl = slice(slot * LANES, (slot + 1) * LANES)
                a0 = slot * AUG
                outs = []
                for si, qm in enumerate(_stream_queries(q_ref[0, rows, sl], mode)):
                    pick = jnp.broadcast_to(jnp.where(lane == si, 1.0, 0.0).astype(BF16), (ATTN_SUB, LANES))
                    qa = jnp.concatenate([qm, pick], axis=1)
                    s = lax.dot_general(qa, kaug_ref[:, a0:a0 + AUG], (((1,), (1,)), ((), ())),
                                        preferred_element_type=F32)
                    acc = jnp.dot(jnp.exp2(s).astype(BF16), vext_ref[:, a0:a0 + AUG],
                                  preferred_element_type=F32)
                    outs.append(acc[:, 0:LANES] / acc[:, LANES:AUG])
                o = _combine_streams(outs, mode, lam, subg_ref, lam_init)
                o_ref[0, rows, sl] = o.astype(o_ref.dtype)
            return carry
        lax.fori_loop(0, tq // ATTN_SUB, sub_tile, 0)

    @pl.when(jnp.logical_not(safe))
    def _():
        _online_attend(q_ref, [(kc_ref, (vc_ref,)), (k_ref, (v_ref,))], chunks, mode, lam, subg_ref,
                       lam_init, o_ref)


def _attention_latent(mode, q, kc, vc, k, v, qgain, head_dim, lam=None, subg=None, lam_init=0.0,
                      tq=512, tk=512):
    b, n, _ = q.shape
    pc = kc.shape[1]
    kt = pc + n
    res = lambda m: pl.BlockSpec((1, m, GW), lambda bi, qi: (bi, 0, 0))
    in_specs = [pl.BlockSpec((1, tq, GW), lambda bi, qi: (bi, qi, 0)), res(pc), res(pc), res(n), res(n),
                _const_spec(qgain.shape)]
    args = [q, kc, vc, k, v, qgain]
    if mode == "diff":
        in_specs += [_const_spec(lam.shape), _const_spec(subg.shape)]
        args += [lam, subg]
    chunks = ((1, pc), (n // tk, tk))
    return pl.pallas_call(
        functools.partial(_attn_fast_kernel, mode=mode, chunks=chunks, lam_init=lam_init, head_dim=head_dim),
        grid=(b, n // tq),
        in_specs=in_specs,
        out_specs=pl.BlockSpec((1, tq, GW), lambda bi, qi: (bi, qi, 0)),
        out_shape=jax.ShapeDtypeStruct((b, n, GW), BF16),
        scratch_shapes=[pltpu.VMEM((kt, NSLOT * AUG), BF16), pltpu.VMEM((kt, NSLOT * AUG), BF16),
                        pltpu.VMEM((8, LANES), F32)],
        compiler_params=_cp(("parallel", "arbitrary")),
        name="attn_latent_" + mode,
    )(*args)


def _attention(mode, q, srcs, tq, tk, lam=None, subg=None, lam_init=0.0):
    b, n, _ = q.shape
    in_specs = [pl.BlockSpec((1, tq, GW), lambda bi, qi: (bi, qi, 0))]
    args = [q]
    chunks = []
    for src in srcs:
        ks = src[0].shape[1]
        step = min(tk, ks)
        chunks.append((ks // step, step))
        for a in src:
            in_specs.append(pl.BlockSpec((1, ks, GW), lambda bi, qi: (bi, 0, 0)))
            args.append(a)
    if mode == "diff":
        in_specs += [_const_spec(lam.shape), _const_spec(subg.shape)]
        args += [lam, subg]
    return pl.pallas_call(
        functools.partial(_attn_kernel, mode=mode, chunks=tuple(chunks), lam_init=lam_init),
        grid=(b, n // tq),
        in_specs=in_specs,
        out_specs=pl.BlockSpec((1, tq, GW), lambda bi, qi: (bi, qi, 0)),
        out_shape=jax.ShapeDtypeStruct((b, n, GW), BF16),
        compiler_params=_cp(("parallel", "parallel")),
        name="attn_" + mode,
    )(*args)


NA_QROWS = 4
NA_KROWS = NA_QROWS + NA_ROWS


def _na_key_start(r0, rows):
    return jnp.clip(r0 - NA_ROWS // 2, 0, rows - NA_KROWS)


def _na_kernel(q_ref, k_ref, v_ref, kc_ref, vc_ref, bias_ref, o_ref, *, rows):
    r0 = pl.program_id(1) * NA_QROWS
    ks = _na_key_start(r0, rows)
    loc = pl.ds(pl.multiple_of(ks * GRID_W, GRID_W), NA_KROWS * GRID_W)
    lane = lax.broadcasted_iota(jnp.int32, (1, LANES), 1)
    nt = (((1,), (1,)), ((), ()))
    for slot in range(NSLOT):
        sl = slice(slot * LANES, (slot + 1) * LANES)
        q = q_ref[0, :, sl]
        k_loc = k_ref[0, loc, sl]
        k_ctx = kc_ref[0, :, sl]
        v_loc = v_ref[0, loc, sl]
        v_ctx = vc_ref[0, :, sl]
        v_loc = jnp.concatenate([v_loc, jnp.ones_like(v_loc)], axis=1)
        v_ctx = jnp.concatenate([v_ctx, jnp.ones_like(v_ctx)], axis=1)
        halves = []
        for half, qh in enumerate(_stream_queries(q, "pair")):
            s_loc = lax.dot_general(qh, k_loc, nt, preferred_element_type=F32) + bias_ref[0, 2 * slot + half]
            s_ctx = lax.dot_general(qh, k_ctx, nt, preferred_element_type=F32)
            m = jnp.maximum(jnp.max(s_loc, axis=-1, keepdims=True), jnp.max(s_ctx, axis=-1, keepdims=True))
            acc = (jnp.dot(jnp.exp2(s_loc - m).astype(BF16), v_loc, preferred_element_type=F32)
                   + jnp.dot(jnp.exp2(s_ctx - m).astype(BF16), v_ctx, preferred_element_type=F32))
            halves.append(acc[:, 0:LANES] / acc[:, LANES:2 * LANES])
        o_ref[0, :, sl] = jnp.where(lane < 64, halves[0], halves[1]).astype(o_ref.dtype)


def _na_bias_table(bias_tab, rows):
    assert rows >= NA_KROWS + NA_QROWS and rows % NA_QROWS == 0
    qc = np.arange(GRID_W)[:, None]
    kc = np.arange(GRID_W)[None, :]
    cstart = np.clip(qc - NA_COLS // 2, 0, GRID_W - NA_COLS)
    in_win = (kc >= cstart) & (kc < cstart + NA_COLS)
    dc_idx = np.clip(kc - qc, 1 - NA_COLS, NA_COLS - 1) + NA_COLS - 1
    col_pick = (dc_idx[:, :, None] == np.arange(2 * NA_COLS - 1)).astype(np.float32)
    variants = (0, NA_ROWS // 2, NA_ROWS)
    row_pick = np.zeros((len(variants), NA_QROWS, NA_KROWS, 2 * NA_ROWS - 1), np.float32)
    for vi, var in enumerate(variants):
        for j in range(NA_QROWS):
            first = min(max(j + var - NA_ROWS // 2, 0), NA_KROWS - NA_ROWS)
            for i in range(first, first + NA_ROWS):
                row_pick[vi, j, i, i - j - var + NA_ROWS - 1] = 1.0
    valid = (row_pick.sum(-1) > 0)[:, None, :, None, :, None] & in_win[None, None, None, :, None, :]
    toep = jnp.einsum("qkd,hrd->hrqk", jnp.asarray(col_pick), bias_tab.astype(F32),
                      precision=lax.Precision.HIGHEST)
    tab = jnp.einsum("vjir,hrqk->vhjqik", jnp.asarray(row_pick), toep, precision=lax.Precision.HIGHEST)
    tab = jnp.where(jnp.asarray(valid), tab * LOG2E, NEG)
    return tab.reshape(len(variants), NA_H, NA_QROWS * GRID_W, NA_KROWS * GRID_W)


def _na_latent(q, k, v, kc, vc, bias):
    b, n, _ = q.shape
    rows = n // GRID_W
    pc = kc.shape[1]
    full = lambda m: pl.BlockSpec((1, m, GW), lambda bi, r: (bi, 0, 0))
    tq = NA_QROWS * GRID_W
    return pl.pallas_call(
        functools.partial(_na_kernel, rows=rows),
        grid=(b, rows // NA_QROWS),
        in_specs=[pl.BlockSpec((1, tq, GW), lambda bi, r: (bi, r, 0)),
                  full(n), full(n), full(pc), full(pc),
                  pl.BlockSpec((1, NA_H, tq, NA_KROWS * GRID_W),
                               lambda bi, r: ((r * NA_QROWS - _na_key_start(r * NA_QROWS, rows))
                                              // (NA_ROWS // 2), 0, 0, 0))],
        out_specs=pl.BlockSpec((1, tq, GW), lambda bi, r: (bi, r, 0)),
        out_shape=jax.ShapeDtypeStruct((b, n, GW), BF16),
        compiler_params=_cp(("parallel", "parallel")),
        name="attn_neighbourhood",
    )(q, k, v, kc, vc, bias)


POOL_HALO = 64


def _pool_kernel(prev_ref, main_ref, next_ref, band_ref, inv_ref, w_ref, scale_ref, o_ref):
    ext = jnp.concatenate([prev_ref[0], main_ref[0], next_ref[0]], axis=0)
    ext_hi = ext.astype(BF16)
    ext_lo = (ext - ext_hi.astype(F32)).astype(BF16)
    for g in range(POOL_G):
        sl = slice(g * POOL_C, (g + 1) * POOL_C)
        tot = jnp.dot(band_ref[0, g], jnp.concatenate([ext_hi[:, sl], ext_lo[:, sl]], axis=1),
                      preferred_element_type=F32)
        mean = (tot[:, :POOL_C] + tot[:, POOL_C:]) * inv_ref[0, :, sl]
        d = (mean - main_ref[0, :, sl]).astype(BF16)
        y = jnp.dot(d, w_ref[g], preferred_element_type=F32) * scale_ref[:, sl]
        o_ref[0, :, sl] = y.astype(o_ref.dtype)


def _pool_windows(n, tp):
    starts = (0, tp if n > 2 * tp else 0, n - tp, 0)
    seqs = (n if n > tp else 2 * tp, n if n > tp else 3 * tp, n, tp)
    bands = np.zeros((4, POOL_G, tp, tp + 2 * POOL_HALO), np.float32)
    inv = np.zeros((4, tp, GW), np.float32)
    for e, (t0, length) in enumerate(zip(starts, seqs)):
        t = t0 + np.arange(tp)[:, None]
        pos = t0 - POOL_HALO + np.arange(tp + 2 * POOL_HALO)[None, :]
        for g, win in enumerate(POOL_WINDOWS):
            lo = np.clip(t - win // 2, 0, length)
            hi = np.clip(t - win // 2 + win, 0, length)
            bands[e, g] = (pos >= lo) & (pos < hi)
            inv[e, :, g * POOL_C:(g + 1) * POOL_C] = 1.0 / (hi - lo)
    return jnp.asarray(bands, dtype=BF16), jnp.asarray(inv)


def _pool(pu, w_pool, scale, tp=256):
    b, n, _ = pu.shape
    hb = tp // POOL_HALO
    nt = n // tp
    last = n // POOL_HALO - 1
    bands, inv = _pool_windows(n, tp)
    edge = lambda i: jnp.where(i == 0, 0, jnp.where(i == nt - 1, 2, 1)) if nt > 1 else 3
    return pl.pallas_call(
        _pool_kernel,
        grid=(b, nt),
        in_specs=[pl.BlockSpec((1, POOL_HALO, GW), lambda bi, i: (bi, jnp.maximum(i * hb - 1, 0), 0)),
                  pl.BlockSpec((1, tp, GW), lambda bi, i: (bi, i, 0)),
                  pl.BlockSpec((1, POOL_HALO, GW), lambda bi, i: (bi, jnp.minimum((i + 1) * hb, last), 0)),
                  pl.BlockSpec((1,) + bands.shape[1:], lambda bi, i: (edge(i), 0, 0, 0)),
                  pl.BlockSpec((1,) + inv.shape[1:], lambda bi, i: (edge(i), 0, 0)),
                  _const_spec(w_pool.shape), _const_spec(scale.shape)],
        out_specs=pl.BlockSpec((1, tp, GW), lambda bi, i: (bi, i, 0)),
        out_shape=jax.ShapeDtypeStruct((b, n, GW), BF16),
        compiler_params=_cp(("parallel", "parallel")),
        name="pool_mixer",
    )(pu, pu, pu, bands, inv, w_pool, scale)


def _outproj_kernel(oa_ref, ob_ref, oc_ref, od_ref, x_ref, mod_ref, g2_ref, w_ref, o_ref, h_ref):
    mix = jnp.dot(oa_ref[...], w_ref[0:GW, :], preferred_element_type=F32)
    mix += jnp.dot(ob_ref[...], w_ref[GW:2 * GW, :], preferred_element_type=F32)
    mix += jnp.dot(oc_ref[...], w_ref[2 * GW:3 * GW, :], preferred_element_type=F32)
    mix += jnp.dot(od_ref[...], w_ref[3 * GW:4 * GW, :], preferred_element_type=F32)
    mod = mod_ref[0]
    x = x_ref[...] + mod[2:3, :] * mix
    o_ref[...] = x
    h_ref[...] = _modulated(x, g2_ref[...], mod, 3).astype(BF16)


def _outproj(oa, ob, oc, od, x2, mod, mod_row, g2, w_out, tm=256):
    t, d = x2.shape
    tok = lambda w: pl.BlockSpec((tm, w), lambda i: (i, 0))
    return pl.pallas_call(
        _outproj_kernel,
        grid=(t // tm,),
        in_specs=[tok(GW)] * 4 + [tok(d), pl.BlockSpec((1, N_MOD, d), lambda i: (mod_row(i), 0, 0)),
                                   _const_spec((1, d)), _const_spec(w_out.shape)],
        out_specs=[tok(d), tok(d)],
        out_shape=[jax.ShapeDtypeStruct((t, d), F32), jax.ShapeDtypeStruct((t, d), BF16)],
        compiler_params=_cp(("parallel",)),
        name="outproj",
    )(oa, ob, oc, od, x2, mod, g2, w_out)


FFN_HALO = 16


def _ffn_kernel(prev_ref, hm_ref, next_ref, x_ref, mod_ref, wg_ref, wv_ref, cwg_ref, cwv_ref, cbg_ref,
                cbv_ref, wd_ref, o_ref, h_ref, ug_ref, uv_ref, *, seq):
    i = pl.program_id(0)
    j = pl.program_id(1)
    tm = x_ref.shape[0]
    tiles_per_seq = max(seq // tm, 1)

    @pl.when(j == 0)
    def _():
        first = (i % tiles_per_seq) == 0
        last = (i % tiles_per_seq) == tiles_per_seq - 1
        h_ref[0:FFN_HALO, :] = jnp.where(first, jnp.zeros_like(prev_ref), prev_ref[...])
        h_ref[FFN_HALO:FFN_HALO + tm, :] = hm_ref[...]
        h_ref[FFN_HALO + tm:, :] = jnp.where(last, jnp.zeros_like(next_ref), next_ref[...])
        o_ref[...] = jnp.zeros_like(o_ref)

    h = h_ref[...]
    if seq < tm:
        pos = lax.broadcasted_iota(jnp.int32, (tm, 1), 0)
        seq_start = (pos % seq) == 0
        seq_end = (pos % seq) == seq - 1

    def conv(w_ref, cw_ref, cb_ref, u_ref):
        u_ref[...] = jnp.dot(h, w_ref[...], preferred_element_type=F32)
        cw = cw_ref[...]
        before = u_ref[FFN_HALO - 1:FFN_HALO - 1 + tm, :]
        after = u_ref[FFN_HALO + 1:FFN_HALO + 1 + tm, :]
        if seq < tm:
            before = jnp.where(seq_start, 0.0, before)
            after = jnp.where(seq_end, 0.0, after)
        return (before * cw[0:1] + u_ref[FFN_HALO:FFN_HALO + tm, :] * cw[1:2] + after * cw[2:3]
                + cb_ref[...])

    gate = conv(wg_ref, cwg_ref, cbg_ref, ug_ref)
    val = conv(wv_ref, cwv_ref, cbv_ref, uv_ref)
    a = (gate * jax.nn.sigmoid(gate) * val).astype(BF16)
    o_ref[...] += jnp.dot(a, wd_ref[...], preferred_element_type=F32)

    @pl.when(j == pl.num_programs(1) - 1)
    def _():
        o_ref[...] = x_ref[...] + mod_ref[0, 5:6, :] * o_ref[...]


def _ffn(x2, h2, mod, mod_row, w_up, conv_w, conv_b, w_down, seq, tm, tn=512):
    t, d = x2.shape
    dff = w_down.shape[0]
    assert t % tm == 0 and (seq % tm == 0 or tm % seq == 0)
    nj = dff // tn
    hb = tm // FFN_HALO
    nhb = t // FFN_HALO
    return pl.pallas_call(
        functools.partial(_ffn_kernel, seq=seq),
        grid=(t // tm, nj),
        in_specs=[pl.BlockSpec((FFN_HALO, d), lambda i, j: (jnp.maximum(i * hb - 1, 0), 0)),
                  pl.BlockSpec((tm, d), lambda i, j: (i, 0)),
                  pl.BlockSpec((FFN_HALO, d), lambda i, j: (jnp.minimum((i + 1) * hb, nhb - 1), 0)),
                  pl.BlockSpec((tm, d), lambda i, j: (i, 0)),
                  pl.BlockSpec((1, N_MOD, d), lambda i, j: (mod_row(i), 0, 0)),
                  pl.BlockSpec((d, tn), lambda i, j: (0, j)),
                  pl.BlockSpec((d, tn), lambda i, j: (0, nj + j)),
                  pl.BlockSpec((CONV_W, tn), lambda i, j: (0, j)),
                  pl.BlockSpec((CONV_W, tn), lambda i, j: (0, nj + j)),
                  pl.BlockSpec((1, tn), lambda i, j: (0, j)),
                  pl.BlockSpec((1, tn), lambda i, j: (0, nj + j)),
                  pl.BlockSpec((tn, d), lambda i, j: (j, 0))],
        out_specs=pl.BlockSpec((tm, d), lambda i, j: (i, 0)),
        out_shape=jax.ShapeDtypeStruct((t, d), F32),
        scratch_shapes=[pltpu.VMEM((tm + 2 * FFN_HALO, d), BF16),
                        pltpu.VMEM((tm + 2 * FFN_HALO, tn), F32), pltpu.VMEM((tm + 2 * FFN_HALO, tn), F32)],
        compiler_params=_cp(("parallel", "arbitrary")),
        name="conv_ffn",
    )(h2, h2, h2, x2, mod, w_up, w_up, conv_w, conv_w, conv_b, conv_b, w_down)


def _block_ones(group):
    idx = np.arange(GW) // group
    return jnp.asarray(idx[:, None] == idx[None, :], dtype=BF16)


def _pad_heads(w, heads, width):
    lead = w.shape[:-1]
    w = w.reshape(lead + (heads, width))
    w = jnp.pad(w, [(0, 0)] * len(lead) + [(0, 0), (0, LANES - width)])
    return w.reshape(lead + (heads * LANES,))


def _layer_weights(P, l):
    d = P["w_in"].shape[1]
    w_in = P["w_in"][l]
    o = np.cumsum((0, 512, 512, 512, MLA_Q_LORA, MLA_KV_LORA, MLA_ROPE, 512, 512, 512, 512))
    col = lambda k: w_in[:, o[k]:o[k + 1]]
    w_all = jnp.concatenate(
        [col(0), col(1), col(2), col(6), col(7), col(8), col(9), col(3), col(4), col(5),
         jnp.zeros((d, ML_W - MLA_Q_LORA - MLA_KV_LORA - MLA_ROPE), F32)], axis=1).astype(BF16)
    wukv = P["mla_w_ukv"][l].reshape(MLA_KV_LORA, MLA_H, MLA_NOPE + MLA_V)
    wuk = _pad_heads(wukv[:, :, :MLA_NOPE].reshape(MLA_KV_LORA, MLA_H * MLA_NOPE), MLA_H, MLA_NOPE)
    wuv = wukv[:, :, MLA_NOPE:].reshape(MLA_KV_LORA, MLA_H * MLA_V)
    tile = lambda g, reps: jnp.tile(g.astype(F32), reps)
    padw = lambda g: jnp.pad(g.astype(F32), (0, GW - g.shape[0]))
    gains = jnp.stack([
        tile(P["diff_qn_g"][l], 8) * (DIFF_DH ** -0.5 * LOG2E),
        tile(P["diff_kn_g"][l], 8),
        tile(P["na_qn_g"][l], 8) * (NA_DH ** -0.5 * LOG2E),
        tile(P["na_kn_g"][l], 8),
        _pad_heads(tile(P["mla_qn_g"][l], MLA_H), MLA_H, MLA_QK) * (MLA_QK ** -0.5 * LOG2E),
        _pad_heads(tile(P["mla_kn_g"][l], MLA_H), MLA_H, MLA_QK),
        padw(P["mla_qa_g"][l]),
        padw(P["mla_kva_g"][l]),
    ])
    return dict(
        w_in=w_all,
        wuq=_pad_heads(P["mla_w_uq"][l], MLA_H, MLA_QK).astype(BF16),
        wuk=wuk.astype(BF16), wuv=wuv.astype(BF16),
        gains=gains, g64=_block_ones(64), g128=_block_ones(LANES),
        norm1=P["norm1_g"][l].reshape(1, d).astype(F32),
        norm2=P["norm2_g"][l].reshape(1, d).astype(F32),
        lam=P["diff_lam"][l].astype(F32),
        subg=P["diff_sub_g"][l].reshape(1, 2 * DIFF_DH).astype(F32),
        pool_w=P["pool_w"][l].astype(BF16),
        pool_scale=P["pool_scale"][l].reshape(1, GW).astype(F32),
        w_out=P["w_out"][l].astype(BF16),
        w_up=P["w_up"][l].astype(BF16),
        conv_w=P["conv_w"][l].astype(F32),
        conv_b=P["conv_b"][l].reshape(1, -1).astype(F32),
        w_down=P["w_down"][l].astype(BF16),
    )


def _rope_tables(n):
    t = jnp.arange(n)
    rows = (t // GRID_W).astype(F32)
    cols = (t % GRID_W).astype(F32)

    def ang(dim):
        quarter = dim // 4
        inv = ROPE_BASE ** (-jnp.arange(quarter, dtype=F32) / quarter)
        return jnp.concatenate([rows[:, None] * inv, cols[:, None] * inv], axis=-1)

    a = ang(DIFF_DH)
    cos_a, sin_a = jnp.cos(a), jnp.sin(a)
    cd = jnp.tile(cos_a, (1, 4))
    sd = jnp.tile(jnp.concatenate([-sin_a, sin_a], axis=1), (1, 2))
    b = ang(MLA_ROPE)
    cos_b, sin_b = jnp.cos(b), jnp.sin(b)
    ones = jnp.ones((n, MLA_NOPE), F32)
    pad = LANES - MLA_NOPE - MLA_ROPE
    cm = jnp.concatenate([ones, cos_b, cos_b, jnp.ones((n, pad), F32)], axis=1)
    sm = jnp.concatenate([0.0 * ones, -sin_b, sin_b, jnp.zeros((n, pad), F32)], axis=1)
    return jnp.stack([cd, sd, cm, sm])


def _layer(x, mod, mod_row_of_batch, W, layer, cache, rope_tab, bias_tab):
    b, n, d = x.shape
    t = b * n
    x2 = x.reshape(t, d)
    lam_init = 0.8 - 0.6 * math.exp(-0.3 * layer)
    tm = min(512, n)
    row_in = lambda i: mod_row_of_batch(i // (n // tm))
    outs = _inproj(x2, mod, row_in, W["norm1"], W, rope_tab, n, states=cache is None, tm=tm)
    qd, kd, vd, qn, kn, vn, pu, qm, km, vm = [a.reshape(b, n, GW) for a in outs[:10]]
    if cache is None:
        oa = _attention("diff", qd, [(kd, vd)], n, n, W["lam"], W["subg"], lam_init)
        ob = _attention("full", qm, [(km, vm)], n, n)
        oc = _attention("pair", qn, [(kn, vn)], n, n)
        state = outs[10:]
    else:
        a_k, a_v, b_ckv, b_kpe, c_k, c_v = cache
        pc = a_k.shape[1]
        ck = a_k.reshape(b, pc, GW).astype(BF16)
        cv = a_v.reshape(b, pc, GW).astype(BF16)
        oa = _attention_latent("diff", qd, ck, cv, kd, vd, W["gains"][0:1], DIFF_DH, W["lam"], W["subg"],
                               lam_init)
        kpe_pad = jnp.pad(b_kpe.reshape(b * pc, MLA_ROPE), ((0, 0), (0, LANES - MLA_ROPE)))
        kmc, vmc = _mla_cache(b_ckv.reshape(b * pc, MLA_KV_LORA), kpe_pad, W)
        ob = _attention_latent("full", qm, kmc.reshape(b, pc, GW), vmc.reshape(b, pc, GW), km, vm,
                               W["gains"][4:5], MLA_QK)
        nk = c_k.reshape(b, pc, GW).astype(BF16)
        nv = c_v.reshape(b, pc, GW).astype(BF16)
        oc = _na_latent(qn, kn, vn, nk, nv, bias_tab)
        state = ()
    od = _pool(pu, W["pool_w"], W["pool_scale"])
    flat = lambda a: a.reshape(t, GW)
    x2, h2 = _outproj(flat(oa), flat(ob), flat(oc), flat(od), x2, mod, row_in, W["norm2"], W["w_out"], tm=tm)
    tmf = min(512, t) if cache is None else min(512, n)
    row_ffn = lambda i: mod_row_of_batch((i * tmf) // n)
    x2 = _ffn(x2, h2, mod, row_ffn, W["w_up"], W["conv_w"], W["conv_b"], W["w_down"], n, tm=tmf)
    return x2.reshape(b, n, d), state


def kernel(x_prompt, x_sample, cache_diff_k, cache_diff_v, cache_mla_ckv, cache_mla_kpe, cache_na_k, cache_na_v, c, c_ctx, norm1_g, norm2_g, ada_w, ada_b, w_in, diff_qn_g, diff_kn_g, diff_lam, diff_sub_g, mla_qa_g, mla_kva_g, mla_w_uq, mla_w_ukv, mla_qn_g, mla_kn_g, na_qn_g, na_kn_g, na_bias, pool_w, pool_scale, w_out, w_up, conv_w, conv_b, w_down):
    P = dict(norm1_g=norm1_g, norm2_g=norm2_g, w_in=w_in, diff_qn_g=diff_qn_g, diff_kn_g=diff_kn_g,
             diff_lam=diff_lam, diff_sub_g=diff_sub_g, mla_qa_g=mla_qa_g, mla_kva_g=mla_kva_g,
             mla_w_uq=mla_w_uq, mla_w_ukv=mla_w_ukv, mla_qn_g=mla_qn_g, mla_kn_g=mla_kn_g,
             na_qn_g=na_qn_g, na_kn_g=na_kn_g, pool_w=pool_w, pool_scale=pool_scale, w_out=w_out,
             w_up=w_up, conv_w=conv_w, conv_b=conv_b, w_down=w_down)
    depth = w_in.shape[0]
    d = x_prompt.shape[-1]
    bd, nd = x_sample.shape[:2]
    n_rows = -(-(bd + 1) // 8) * 8
    cv = jnp.concatenate([c, c_ctx[None, :], jnp.zeros((n_rows - bd - 1, d), F32)], axis=0)
    mod_all = _ada(cv, ada_w, ada_b).reshape(depth, n_rows, N_MOD, d)
    rope_tab = _rope_tables(nd)
    Ws = [_layer_weights(P, l) for l in range(depth)]

    xp = x_prompt
    states = []
    for l in range(depth):
        xp, st = _layer(xp, mod_all[l], lambda bi: bd, Ws[l], l, None, None, None)
        states.append(st)
    xs = x_sample
    for l in range(depth):
        cache_l = (cache_diff_k[:, l], cache_diff_v[:, l], cache_mla_ckv[:, l], cache_mla_kpe[:, l],
                   cache_na_k[:, l], cache_na_v[:, l])
        bias_tab = _na_bias_table(na_bias[l], nd // GRID_W)
        xs, _ = _layer(xs, mod_all[l], lambda bi: bi, Ws[l], l, cache_l, rope_tab, bias_tab)

    bp, npr = x_prompt.shape[:2]

    def stack(k, shape):
        return jnp.stack([s[k].reshape((bp, npr) + shape) for s in states], axis=1)

    return (xp, xs,
            stack(0, (DIFF_H, 2, DIFF_DH)), stack(1, (DIFF_H, 2 * DIFF_DH)),
            stack(2, (MLA_KV_LORA,)), stack(3, (MLA_ROPE,)),
            stack(4, (NA_H, NA_DH)), stack(5, (NA_H, NA_DH)))
```

```python
import functools
import math

import numpy as np
import jax
import jax.numpy as jnp
from jax import lax
from jax.experimental import pallas as pl
from jax.experimental.pallas import tpu as pltpu

F32 = jnp.float32
BF16 = jnp.bfloat16

GRID_W = 64
ROPE_BASE = 10000.0
EPS = 1e-6
N_MOD = 6
DIFF_H = 4
DIFF_DH = 64
MLA_H = 4
MLA_NOPE = 64
MLA_ROPE = 32
MLA_V = 128
MLA_Q_LORA = 384
MLA_KV_LORA = 128
NA_H = 8
NA_DH = 64
NA_ROWS = 8
NA_COLS = 16
POOL_WINDOWS = (2, 4, 8, 16)
POOL_G = 4
POOL_C = 128
CONV_W = 3

GW = 512
LANES = 128
NSLOT = GW // LANES
MXU_DIM = 256
MLA_QK = MLA_NOPE + MLA_ROPE
V7X_VMEM_LIMIT = 56 * 1024 * 1024
NEG = -1e30
LOG2E = math.log2(math.e)

C_DQ, C_DK, C_DV, C_NQ, C_NK, C_NV, C_PU, C_ML = 0, 512, 1024, 1536, 2048, 2560, 3072, 3584
ML_W = 640
IN_WP = C_ML + ML_W


def _cp(sem, vmem=V7X_VMEM_LIMIT):
    return pltpu.CompilerParams(dimension_semantics=sem, vmem_limit_bytes=vmem)


def _const_spec(shape):
    nd = len(shape)
    return pl.BlockSpec(shape, lambda *_: (0,) * nd, pipeline_mode=pl.Buffered(1))


def _ada_kernel(c_ref, w_ref, b_ref, o_ref):
    c = c_ref[...]
    s = c * jax.nn.sigmoid(c)
    o_ref[0] = jnp.dot(s.astype(BF16), w_ref[0].astype(BF16), preferred_element_type=F32) + b_ref[0]


def _ada(cv, ada_w, ada_b, tn=1024):
    depth, d, n = ada_w.shape
    r = cv.shape[0]
    return pl.pallas_call(
        _ada_kernel,
        grid=(depth, n // tn),
        in_specs=[pl.BlockSpec((r, d), lambda l, j: (0, 0)),
                  pl.BlockSpec((1, d, tn), lambda l, j: (l, 0, j)),
                  pl.BlockSpec((1, 1, tn), lambda l, j: (l, 0, j))],
        out_specs=pl.BlockSpec((1, r, tn), lambda l, j: (l, 0, j)),
        out_shape=jax.ShapeDtypeStruct((depth, r, n), F32),
        compiler_params=_cp(("parallel", "parallel")),
        name="adaln",
    )(cv, ada_w, ada_b.reshape(depth, 1, n))


def _modulated(x, g, mod, k):
    ms = jnp.mean(x * x, axis=-1, keepdims=True)
    y = x * lax.rsqrt(ms + EPS) * g
    return y * (1.0 + mod[k + 1:k + 2, :]) + mod[k:k + 1, :]


def _group_rms(z, gsum_ref, inv_n, gain):
    sq = (z * z).astype(BF16)
    ss = jnp.concatenate(
        [jnp.dot(sq[:, c:c + MXU_DIM], gsum_ref[c:c + MXU_DIM, c:c + MXU_DIM], preferred_element_type=F32)
         for c in range(0, z.shape[1], MXU_DIM)], axis=1)
    return z * lax.rsqrt(ss * inv_n + EPS) * gain


def _rope_chunks(z, cos_ref, sin_ref, lo_pred, shift_lo, shift_hi):
    c = cos_ref[...]
    s = sin_ref[...]
    outs = []
    for j in range(z.shape[1] // LANES):
        zc = z[:, j * LANES:(j + 1) * LANES]
        partner = jnp.where(lo_pred, pltpu.roll(zc, shift_lo, 1), pltpu.roll(zc, shift_hi, 1))
        outs.append(zc * c + partner * s)
    return jnp.concatenate(outs, axis=1)


def _mla_kv(ckv_b, kp, wuk_ref, wuv_ref, g128_ref, kgain):
    kp_slot = pltpu.roll(kp, MLA_NOPE, 1)
    kh = jnp.dot(ckv_b, wuk_ref[...], preferred_element_type=F32) + jnp.concatenate([kp_slot] * MLA_H, axis=1)
    k = _group_rms(kh, g128_ref, 1.0 / MLA_QK, kgain)
    v = jnp.dot(ckv_b, wuv_ref[...], preferred_element_type=F32)
    return k, v


def _inproj_kernel(*refs, rope, states):
    it = iter(refs)
    x_ref, mod_ref, g_ref, w_ref, wuq_ref, wuk_ref, wuv_ref = (next(it) for _ in range(7))
    gains_ref, g64_ref, g128_ref = (next(it) for _ in range(3))
    if rope:
        rope_ref = next(it)
    qd_ref, kd_ref, vd_ref, qn_ref, kn_ref, vn_ref, pu_ref, qm_ref, km_ref, vm_ref = (
        next(it) for _ in range(10))
    if states:
        ska_ref, sva_ref, sckv_ref, skpe_ref, skc_ref, svc_ref = (next(it) for _ in range(6))

    h = _modulated(x_ref[...], g_ref[...], mod_ref[0], 0).astype(BF16)
    gains = gains_ref[...]
    lane = lax.broadcasted_iota(jnp.int32, (1, LANES), 1)

    def seg(c0, w=GW):
        return jnp.dot(h, w_ref[:, c0:c0 + w], preferred_element_type=F32)

    def rope_d(z):
        if not rope:
            return z
        return _rope_chunks(z, rope_ref.at[0], rope_ref.at[1], (lane % 64) < 32, LANES - 32, 32)

    def rope_m(z):
        if not rope:
            return z
        return _rope_chunks(z, rope_ref.at[2], rope_ref.at[3], lane < MLA_NOPE + MLA_ROPE // 2,
                            LANES - MLA_ROPE // 2, MLA_ROPE // 2)

    qd_ref[...] = rope_d(_group_rms(seg(C_DQ), g64_ref, 1.0 / DIFF_DH, gains[0:1])).astype(BF16)
    kd = _group_rms(seg(C_DK), g64_ref, 1.0 / DIFF_DH, gains[1:2])
    kd_ref[...] = rope_d(kd).astype(BF16)
    vd = seg(C_DV)
    vd_ref[...] = vd.astype(BF16)
    qn_ref[...] = _group_rms(seg(C_NQ), g64_ref, 1.0 / NA_DH, gains[2:3]).astype(BF16)
    kn = _group_rms(seg(C_NK), g64_ref, 1.0 / NA_DH, gains[3:4])
    kn_ref[...] = kn.astype(BF16)
    vn = seg(C_NV)
    vn_ref[...] = vn.astype(BF16)
    pu_ref[...] = seg(C_PU)
    zm = seg(C_ML, ML_W)
    cq = zm[:, :MLA_Q_LORA]
    cq = cq * lax.rsqrt(jnp.mean(cq * cq, axis=-1, keepdims=True) + EPS) * gains[6:7, :MLA_Q_LORA]
    qh = jnp.dot(cq.astype(BF16), wuq_ref[...], preferred_element_type=F32)
    qm_ref[...] = rope_m(_group_rms(qh, g128_ref, 1.0 / MLA_QK, gains[4:5])).astype(BF16)
    ck = zm[:, MLA_Q_LORA:MLA_Q_LORA + MLA_KV_LORA]
    ckv = ck * lax.rsqrt(jnp.mean(ck * ck, axis=-1, keepdims=True) + EPS) * gains[7:8, :MLA_KV_LORA]
    kp = zm[:, GW:GW + LANES]
    km, vm = _mla_kv(ckv.astype(BF16), kp, wuk_ref, wuv_ref, g128_ref, gains[5:6])
    km_ref[...] = rope_m(km).astype(BF16)
    vm_ref[...] = vm.astype(BF16)
    if states:
        ska_ref[...] = kd
        sva_ref[...] = vd
        sckv_ref[...] = ckv
        skpe_ref[...] = kp[:, :MLA_ROPE]
        skc_ref[...] = kn
        svc_ref[...] = vn


def _inproj(x2, mod, mod_row, g1, W, rope_tab, seq, states, tm=256):
    t, d = x2.shape
    rope = rope_tab is not None
    tiles_per_seq = seq // tm
    tok = lambda w: pl.BlockSpec((tm, w), lambda i: (i, 0))
    in_specs = [tok(d),
                pl.BlockSpec((1, N_MOD, d), lambda i: (mod_row(i), 0, 0)),
                _const_spec((1, d)),
                _const_spec(W["w_in"].shape), _const_spec(W["wuq"].shape),
                _const_spec(W["wuk"].shape), _const_spec(W["wuv"].shape),
                _const_spec(W["gains"].shape), _const_spec(W["g64"].shape),
                _const_spec(W["g128"].shape)]
    args = [x2, mod, g1, W["w_in"], W["wuq"], W["wuk"], W["wuv"], W["gains"], W["g64"], W["g128"]]
    if rope:
        in_specs.append(pl.BlockSpec((4, tm, LANES), lambda i: (0, i % tiles_per_seq, 0)))
        args.append(rope_tab)
    out_specs = [tok(GW)] * 10
    out_shape = [jax.ShapeDtypeStruct((t, GW), BF16)] * 6 + [jax.ShapeDtypeStruct((t, GW), F32)] + \
                [jax.ShapeDtypeStruct((t, GW), BF16)] * 3
    if states:
        widths = (GW, GW, MLA_KV_LORA, MLA_ROPE, GW, GW)
        out_specs += [tok(w) for w in widths]
        out_shape += [jax.ShapeDtypeStruct((t, w), F32) for w in widths]
    return pl.pallas_call(
        functools.partial(_inproj_kernel, rope=rope, states=states),
        grid=(t // tm,),
        in_specs=in_specs, out_specs=out_specs, out_shape=out_shape,
        compiler_params=_cp(("parallel",)),
        name="inproj",
    )(*args)


def _mla_cache_kernel(ckv_ref, kp_ref, wuk_ref, wuv_ref, g128_ref, gains_ref, k_ref, v_ref):
    k, v = _mla_kv(ckv_ref[...].astype(BF16), kp_ref[...], wuk_ref, wuv_ref, g128_ref, gains_ref[5:6, :])
    k_ref[...] = k.astype(BF16)
    v_ref[...] = v.astype(BF16)


def _mla_cache(ckv2, kp2, W, tm=256):
    t = ckv2.shape[0]
    tok = lambda w: pl.BlockSpec((tm, w), lambda i: (i, 0))
    return pl.pallas_call(
        _mla_cache_kernel,
        grid=(t // tm,),
        in_specs=[tok(MLA_KV_LORA), tok(LANES), _const_spec(W["wuk"].shape), _const_spec(W["wuv"].shape),
                  _const_spec(W["g128"].shape), _const_spec(W["gains"].shape)],
        out_specs=[tok(GW), tok(GW)],
        out_shape=[jax.ShapeDtypeStruct((t, GW), BF16)] * 2,
        compiler_params=_cp(("parallel",)),
        name="mla_cache_kv",
    )(ckv2, kp2, W["wuk"], W["wuv"], W["g128"], W["gains"])


def _diff_lambda(lam_ref, lam_init):
    lf = lam_ref[...]
    return (jnp.exp(jnp.sum(lf[0:1] * lf[1:2], axis=-1, keepdims=True))
            - jnp.exp(jnp.sum(lf[2:3] * lf[3:4], axis=-1, keepdims=True)) + lam_init)


def _stream_queries(q, mode):
    if mode == "full":
        return [q]
    lane = lax.broadcasted_iota(jnp.int32, (1, LANES), 1)
    zero = jnp.zeros_like(q)
    return [jnp.where(lane < 64, q, zero), jnp.where(lane < 64, zero, q)]


def _combine_streams(outs, mode, lam, subg_ref, lam_init):
    if mode == "full":
        return outs[0]
    if mode == "pair":
        lane = lax.broadcasted_iota(jnp.int32, (1, LANES), 1)
        return jnp.where(lane < 64, outs[0], outs[1])
    o = outs[0] - lam * outs[1]
    o = o * lax.rsqrt(jnp.mean(o * o, axis=-1, keepdims=True) + EPS) * subg_ref[...]
    return o * (1.0 - lam_init)


def _online_attend(q_ref, srcs, chunks, mode, lam, subg_ref, lam_init, o_ref):
    tq = q_ref.shape[1]
    nstream = 1 if mode == "full" else 2
    for slot in range(NSLOT):
        sl = slice(slot * LANES, (slot + 1) * LANES)
        qs = _stream_queries(q_ref[0, :, sl], mode)

        def step(k_c, v_cs, state):
            new = []
            for si in range(nstream):
                m, l, acc = state[3 * si:3 * si + 3]
                s = lax.dot_general(qs[si], k_c, (((1,), (1,)), ((), ())), preferred_element_type=F32)
                m_new = jnp.maximum(m, jnp.max(s, axis=-1, keepdims=True))
                alpha = jnp.exp2(m - m_new)
                p = jnp.exp2(s - m_new)
                l = alpha * l + jnp.sum(p, axis=-1, keepdims=True)
                acc = alpha * acc + jnp.dot(p.astype(BF16), v_cs[si % len(v_cs)],
                                            preferred_element_type=F32)
                new += [m_new, l, acc]
            return tuple(new)

        state = ()
        for _ in range(nstream):
            state += (jnp.full((tq, 1), NEG, F32), jnp.zeros((tq, 1), F32), jnp.zeros((tq, LANES), F32))
        for (k_ref, v_refs), (n_chunk, tk) in zip(srcs, chunks):
            if n_chunk == 1:
                state = step(k_ref[0, :, sl], [v[0, :, sl] for v in v_refs], state)
            else:
                def body(c, st, k_ref=k_ref, v_refs=v_refs, tk=tk):
                    rows = pl.ds(pl.multiple_of(c * tk, tk), tk)
                    return step(k_ref[0, rows, sl], [v[0, rows, sl] for v in v_refs], st)
                state = lax.fori_loop(0, n_chunk, body, state)

        outs = [state[3 * si + 2] / state[3 * si + 1] for si in range(nstream)]
        o_ref[0, :, sl] = _combine_streams(outs, mode, lam, subg_ref, lam_init).astype(o_ref.dtype)


def _attn_kernel(*refs, mode, chunks, lam_init):
    it = iter(refs)
    q_ref = next(it)
    srcs = []
    for _ in chunks:
        k_ref = next(it)
        srcs.append((k_ref, (next(it),)))
    lam, subg_ref = None, None
    if mode == "diff":
        lam = _diff_lambda(next(it), lam_init)
        subg_ref = next(it)
    o_ref = next(it)
    _online_attend(q_ref, srcs, chunks, mode, lam, subg_ref, lam_init, o_ref)


AUG = 2 * LANES
SHIFT_MARGIN = 1.0 + 2.0 ** -6
SHIFT_LIMIT = 48.0
ATTN_SUB = 512
VT_ROWS = LANES + 16


def _attn_fast_kernel(*refs, mode, chunks, lam_init, head_dim):
    it = iter(refs)
    q_ref, kc_ref, vc_ref, k_ref, v_ref, qgain_ref = (next(it) for _ in range(6))
    lam, subg_ref = None, None
    if mode == "diff":
        lam = _diff_lambda(next(it), lam_init)
        subg_ref = next(it)
    o_ref, kaug_ref, vext_ref, shift_ref = (next(it) for _ in range(4))

    tq = q_ref.shape[1]
    pc, n = kc_ref.shape[1], k_ref.shape[1]
    kt = pc + n
    nstream = 1 if mode == "full" else 2
    lane = lax.broadcasted_iota(jnp.int32, (1, LANES), 1)

    @pl.when(pl.program_id(1) == 0)
    def _():
        q_bound = math.sqrt(head_dim) * jnp.max(jnp.abs(qgain_ref[...]), axis=-1, keepdims=True)
        shift_ref[...] = jnp.zeros_like(shift_ref)
        for slot in range(NSLOT):
            sl = slice(slot * LANES, (slot + 1) * LANES)
            a0 = slot * AUG
            kaug_ref[0:pc, a0:a0 + LANES] = kc_ref[0, :, sl]
            kaug_ref[pc:kt, a0:a0 + LANES] = k_ref[0, :, sl]
            r0 = slot * VT_ROWS
            for c0 in range(0, kt, MXU_DIM):
                chunk = (vc_ref[0, c0:c0 + MXU_DIM, sl] if c0 < pc
                         else v_ref[0, c0 - pc:c0 - pc + MXU_DIM, sl])
                vext_ref[r0:r0 + LANES, c0:c0 + MXU_DIM] = chunk.astype(F32).T.astype(BF16)
            sub = lax.broadcasted_iota(jnp.int32, (VT_ROWS - LANES, 1), 0)
            vext_ref[r0 + LANES:r0 + VT_ROWS, :] = jnp.broadcast_to(
                jnp.where(sub == 0, 1.0, 0.0).astype(BF16), (VT_ROWS - LANES, kt))
            shift_lanes = jnp.zeros((1, LANES), F32)
            for si in range(nstream):
                best = None
                for src in (kc_ref, k_ref):
                    kk = _stream_queries(src[0, :, sl], mode)[si].astype(F32)
                    nrm = jnp.max(jnp.sum(kk * kk, axis=-1, keepdims=True), axis=0, keepdims=True)
                    best = nrm if best is None else jnp.maximum(best, nrm)
                shift = q_bound * jnp.sqrt(best) * SHIFT_MARGIN
                row = slot * nstream + si
                shift_ref[row:row + 1, :] = jnp.broadcast_to(shift, (1, LANES))
                shift_lanes = jnp.where(lane == si, -shift, shift_lanes)
            kaug_ref[:, a0 + LANES:a0 + AUG] = jnp.broadcast_to(shift_lanes.astype(BF16), (kt, LANES))

    safe = jnp.max(shift_ref[...]) <= SHIFT_LIMIT

    @pl.when(safe)
    def _():
        def sub_tile(t, carry):
            rows = pl.ds(pl.multiple_of(t * ATTN_SUB, ATTN_SUB), ATTN_SUB)
            for slot in range(NSLOT):
                sl = slice(slot * LANES, (slot + 1) * LANES)
                a0 = slot * AUG
                outs = []
                for si, qm in enumerate(_stream_queries(q_ref[0, rows, sl], mode)):
                    pick = jnp.broadcast_to(jnp.where(lane == si, 1.0, 0.0).astype(BF16), (ATTN_SUB, LANES))
                    qa = jnp.concatenate([qm, pick], axis=1)
                    s_t = lax.dot_general(kaug_ref[:, a0:a0 + AUG], qa, (((1,), (1,)), ((), ())),
                                          preferred_element_type=F32)
                    acc_t = jnp.dot(vext_ref[slot * VT_ROWS:(slot + 1) * VT_ROWS, :],
                                    jnp.exp2(s_t).astype(BF16), preferred_element_type=F32)
                    outs.append((acc_t[0:LANES] / acc_t[LANES:LANES + 1]).T)
                o = _combine_streams(outs, mode, lam, subg_ref, lam_init)
                o_ref[0, rows, sl] = o.astype(o_ref.dtype)
            return carry
        lax.fori_loop(0, tq // ATTN_SUB, sub_tile, 0)

    @pl.when(jnp.logical_not(safe))
    def _():
        _online_attend(q_ref, [(kc_ref, (vc_ref,)), (k_ref, (v_ref,))], chunks, mode, lam, subg_ref,
                       lam_init, o_ref)


def _attention_latent(mode, q, kc, vc, k, v, qgain, head_dim, lam=None, subg=None, lam_init=0.0,
                      tq=512, tk=512):
    b, n, _ = q.shape
    pc = kc.shape[1]
    kt = pc + n
    res = lambda m: pl.BlockSpec((1, m, GW), lambda bi, qi: (bi, 0, 0))
    in_specs = [pl.BlockSpec((1, tq, GW), lambda bi, qi: (bi, qi, 0)), res(pc), res(pc), res(n), res(n),
                _const_spec(qgain.shape)]
    args = [q, kc, vc, k, v, qgain]
    if mode == "diff":
        in_specs += [_const_spec(lam.shape), _const_spec(subg.shape)]
        args += [lam, subg]
    chunks = ((1, pc), (n // tk, tk))
    return pl.pallas_call(
        functools.partial(_attn_fast_kernel, mode=mode, chunks=chunks, lam_init=lam_init, head_dim=head_dim),
        grid=(b, n // tq),
        in_specs=in_specs,
        out_specs=pl.BlockSpec((1, tq, GW), lambda bi, qi: (bi, qi, 0)),
        out_shape=jax.ShapeDtypeStruct((b, n, GW), BF16),
        scratch_shapes=[pltpu.VMEM((kt, NSLOT * AUG), BF16), pltpu.VMEM((NSLOT * VT_ROWS, kt), BF16),
                        pltpu.VMEM((8, LANES), F32)],
        compiler_params=_cp(("parallel", "arbitrary")),
        name="attn_latent_" + mode,
    )(*args)


def _attention(mode, q, srcs, tq, tk, lam=None, subg=None, lam_init=0.0):
    b, n, _ = q.shape
    in_specs = [pl.BlockSpec((1, tq, GW), lambda bi, qi: (bi, qi, 0))]
    args = [q]
    chunks = []
    for src in srcs:
        ks = src[0].shape[1]
        step = min(tk, ks)
        chunks.append((ks // step, step))
        for a in src:
            in_specs.append(pl.BlockSpec((1, ks, GW), lambda bi, qi: (bi, 0, 0)))
            args.append(a)
    if mode == "diff":
        in_specs += [_const_spec(lam.shape), _const_spec(subg.shape)]
        args += [lam, subg]
    return pl.pallas_call(
        functools.partial(_attn_kernel, mode=mode, chunks=tuple(chunks), lam_init=lam_init),
        grid=(b, n // tq),
        in_specs=in_specs,
        out_specs=pl.BlockSpec((1, tq, GW), lambda bi, qi: (bi, qi, 0)),
        out_shape=jax.ShapeDtypeStruct((b, n, GW), BF16),
        compiler_params=_cp(("parallel", "parallel")),
        name="attn_" + mode,
    )(*args)


NA_QROWS = 4
NA_KROWS = NA_QROWS + NA_ROWS


def _na_key_start(r0, rows):
    return jnp.clip(r0 - NA_ROWS // 2, 0, rows - NA_KROWS)


def _na_kernel(q_ref, k_ref, v_ref, kc_ref, vc_ref, bias_ref, o_ref, *, rows):
    r0 = pl.program_id(1) * NA_QROWS
    ks = _na_key_start(r0, rows)
    loc = pl.ds(pl.multiple_of(ks * GRID_W, GRID_W), NA_KROWS * GRID_W)
    lane = lax.broadcasted_iota(jnp.int32, (1, LANES), 1)
    nt = (((1,), (1,)), ((), ()))
    for slot in range(NSLOT):
        sl = slice(slot * LANES, (slot + 1) * LANES)
        q = q_ref[0, :, sl]
        k_loc = k_ref[0, loc, sl]
        k_ctx = kc_ref[0, :, sl]
        v_loc = v_ref[0, loc, sl]
        v_ctx = vc_ref[0, :, sl]
        v_loc = jnp.concatenate([v_loc, jnp.ones_like(v_loc)], axis=1)
        v_ctx = jnp.concatenate([v_ctx, jnp.ones_like(v_ctx)], axis=1)
        halves = []
        for half, qh in enumerate(_stream_queries(q, "pair")):
            s_loc = lax.dot_general(qh, k_loc, nt, preferred_element_type=F32) + bias_ref[0, 2 * slot + half]
            s_ctx = lax.dot_general(qh, k_ctx, nt, preferred_element_type=F32)
            m = jnp.maximum(jnp.max(s_loc, axis=-1, keepdims=True), jnp.max(s_ctx, axis=-1, keepdims=True))
            acc = (jnp.dot(jnp.exp2(s_loc - m).astype(BF16), v_loc, preferred_element_type=F32)
                   + jnp.dot(jnp.exp2(s_ctx - m).astype(BF16), v_ctx, preferred_element_type=F32))
            halves.append(acc[:, 0:LANES] / acc[:, LANES:2 * LANES])
        o_ref[0, :, sl] = jnp.where(lane < 64, halves[0], halves[1]).astype(o_ref.dtype)


def _na_bias_table(bias_tab, rows):
    assert rows >= NA_KROWS + NA_QROWS and rows % NA_QROWS == 0
    qc = np.arange(GRID_W)[:, None]
    kc = np.arange(GRID_W)[None, :]
    cstart = np.clip(qc - NA_COLS // 2, 0, GRID_W - NA_COLS)
    in_win = (kc >= cstart) & (kc < cstart + NA_COLS)
    dc_idx = np.clip(kc - qc, 1 - NA_COLS, NA_COLS - 1) + NA_COLS - 1
    col_pick = (dc_idx[:, :, None] == np.arange(2 * NA_COLS - 1)).astype(np.float32)
    variants = (0, NA_ROWS // 2, NA_ROWS)
    row_pick = np.zeros((len(variants), NA_QROWS, NA_KROWS, 2 * NA_ROWS - 1), np.float32)
    for vi, var in enumerate(variants):
        for j in range(NA_QROWS):
            first = min(max(j + var - NA_ROWS // 2, 0), NA_KROWS - NA_ROWS)
            for i in range(first, first + NA_ROWS):
                row_pick[vi, j, i, i - j - var + NA_ROWS - 1] = 1.0
    valid = (row_pick.sum(-1) > 0)[:, None, :, None, :, None] & in_win[None, None, None, :, None, :]
    toep = jnp.einsum("qkd,hrd->hrqk", jnp.asarray(col_pick), bias_tab.astype(F32),
                      precision=lax.Precision.HIGHEST)
    tab = jnp.einsum("vjir,hrqk->vhjqik", jnp.asarray(row_pick), toep, precision=lax.Precision.HIGHEST)
    tab = jnp.where(jnp.asarray(valid), tab * LOG2E, NEG)
    return tab.reshape(len(variants), NA_H, NA_QROWS * GRID_W, NA_KROWS * GRID_W)


def _na_latent(q, k, v, kc, vc, bias):
    b, n, _ = q.shape
    rows = n // GRID_W
    pc = kc.shape[1]
    full = lambda m: pl.BlockSpec((1, m, GW), lambda bi, r: (bi, 0, 0))
    tq = NA_QROWS * GRID_W
    return pl.pallas_call(
        functools.partial(_na_kernel, rows=rows),
        grid=(b, rows // NA_QROWS),
        in_specs=[pl.BlockSpec((1, tq, GW), lambda bi, r: (bi, r, 0)),
                  full(n), full(n), full(pc), full(pc),
                  pl.BlockSpec((1, NA_H, tq, NA_KROWS * GRID_W),
                               lambda bi, r: ((r * NA_QROWS - _na_key_start(r * NA_QROWS, rows))
                                              // (NA_ROWS // 2), 0, 0, 0))],
        out_specs=pl.BlockSpec((1, tq, GW), lambda bi, r: (bi, r, 0)),
        out_shape=jax.ShapeDtypeStruct((b, n, GW), BF16),
        compiler_params=_cp(("parallel", "parallel")),
        name="attn_neighbourhood",
    )(q, k, v, kc, vc, bias)


POOL_HALO = 64


def _pool_kernel(prev_ref, main_ref, next_ref, band_ref, inv_ref, w_ref, scale_ref, o_ref):
    ext = jnp.concatenate([prev_ref[0], main_ref[0], next_ref[0]], axis=0)
    ext_hi = ext.astype(BF16)
    ext_lo = (ext - ext_hi.astype(F32)).astype(BF16)
    for g in range(POOL_G):
        sl = slice(g * POOL_C, (g + 1) * POOL_C)
        tot = jnp.dot(band_ref[0, g], jnp.concatenate([ext_hi[:, sl], ext_lo[:, sl]], axis=1),
                      preferred_element_type=F32)
        mean = (tot[:, :POOL_C] + tot[:, POOL_C:]) * inv_ref[0, :, sl]
        d = (mean - main_ref[0, :, sl]).astype(BF16)
        y = jnp.dot(d, w_ref[g], preferred_element_type=F32) * scale_ref[:, sl]
        o_ref[0, :, sl] = y.astype(o_ref.dtype)


def _pool_windows(n, tp):
    starts = (0, tp if n > 2 * tp else 0, n - tp, 0)
    seqs = (n if n > tp else 2 * tp, n if n > tp else 3 * tp, n, tp)
    bands = np.zeros((4, POOL_G, tp, tp + 2 * POOL_HALO), np.float32)
    inv = np.zeros((4, tp, GW), np.float32)
    for e, (t0, length) in enumerate(zip(starts, seqs)):
        t = t0 + np.arange(tp)[:, None]
        pos = t0 - POOL_HALO + np.arange(tp + 2 * POOL_HALO)[None, :]
        for g, win in enumerate(POOL_WINDOWS):
            lo = np.clip(t - win // 2, 0, length)
            hi = np.clip(t - win // 2 + win, 0, length)
            bands[e, g] = (pos >= lo) & (pos < hi)
            inv[e, :, g * POOL_C:(g + 1) * POOL_C] = 1.0 / (hi - lo)
    return jnp.asarray(bands, dtype=BF16), jnp.asarray(inv)


def _pool(pu, w_pool, scale, tp=256):
    b, n, _ = pu.shape
    hb = tp // POOL_HALO
    nt = n // tp
    last = n // POOL_HALO - 1
    bands, inv = _pool_windows(n, tp)
    edge = lambda i: jnp.where(i == 0, 0, jnp.where(i == nt - 1, 2, 1)) if nt > 1 else 3
    return pl.pallas_call(
        _pool_kernel,
        grid=(b, nt),
        in_specs=[pl.BlockSpec((1, POOL_HALO, GW), lambda bi, i: (bi, jnp.maximum(i * hb - 1, 0), 0)),
                  pl.BlockSpec((1, tp, GW), lambda bi, i: (bi, i, 0)),
                  pl.BlockSpec((1, POOL_HALO, GW), lambda bi, i: (bi, jnp.minimum((i + 1) * hb, last), 0)),
                  pl.BlockSpec((1,) + bands.shape[1:], lambda bi, i: (edge(i), 0, 0, 0)),
                  pl.BlockSpec((1,) + inv.shape[1:], lambda bi, i: (edge(i), 0, 0)),
                  _const_spec(w_pool.shape), _const_spec(scale.shape)],
        out_specs=pl.BlockSpec((1, tp, GW), lambda bi, i: (bi, i, 0)),
        out_shape=jax.ShapeDtypeStruct((b, n, GW), BF16),
        compiler_params=_cp(("parallel", "parallel")),
        name="pool_mixer",
    )(pu, pu, pu, bands, inv, w_pool, scale)


def _outproj_kernel(oa_ref, ob_ref, oc_ref, od_ref, x_ref, mod_ref, g2_ref, w_ref, o_ref, h_ref):
    mix = jnp.dot(oa_ref[...], w_ref[0:GW, :], preferred_element_type=F32)
    mix += jnp.dot(ob_ref[...], w_ref[GW:2 * GW, :], preferred_element_type=F32)
    mix += jnp.dot(oc_ref[...], w_ref[2 * GW:3 * GW, :], preferred_element_type=F32)
    mix += jnp.dot(od_ref[...], w_ref[3 * GW:4 * GW, :], preferred_element_type=F32)
    mod = mod_ref[0]
    x = x_ref[...] + mod[2:3, :] * mix
    o_ref[...] = x
    h_ref[...] = _modulated(x, g2_ref[...], mod, 3).astype(BF16)


def _outproj(oa, ob, oc, od, x2, mod, mod_row, g2, w_out, tm=256):
    t, d = x2.shape
    tok = lambda w: pl.BlockSpec((tm, w), lambda i: (i, 0))
    return pl.pallas_call(
        _outproj_kernel,
        grid=(t // tm,),
        in_specs=[tok(GW)] * 4 + [tok(d), pl.BlockSpec((1, N_MOD, d), lambda i: (mod_row(i), 0, 0)),
                                   _const_spec((1, d)), _const_spec(w_out.shape)],
        out_specs=[tok(d), tok(d)],
        out_shape=[jax.ShapeDtypeStruct((t, d), F32), jax.ShapeDtypeStruct((t, d), BF16)],
        compiler_params=_cp(("parallel",)),
        name="outproj",
    )(oa, ob, oc, od, x2, mod, g2, w_out)


FFN_HALO = 16


def _ffn_kernel(prev_ref, hm_ref, next_ref, x_ref, mod_ref, wg_ref, wv_ref, cwg_ref, cwv_ref, cbg_ref,
                cbv_ref, wd_ref, o_ref, h_ref, ug_ref, uv_ref, *, seq):
    i = pl.program_id(0)
    j = pl.program_id(1)
    tm = x_ref.shape[0]
    tiles_per_seq = max(seq // tm, 1)

    @pl.when(j == 0)
    def _():
        first = (i % tiles_per_seq) == 0
        last = (i % tiles_per_seq) == tiles_per_seq - 1
        h_ref[0:FFN_HALO, :] = jnp.where(first, jnp.zeros_like(prev_ref), prev_ref[...])
        h_ref[FFN_HALO:FFN_HALO + tm, :] = hm_ref[...]
        h_ref[FFN_HALO + tm:, :] = jnp.where(last, jnp.zeros_like(next_ref), next_ref[...])
        o_ref[...] = jnp.zeros_like(o_ref)

    h = h_ref[...]
    if seq < tm:
        pos = lax.broadcasted_iota(jnp.int32, (tm, 1), 0)
        seq_start = (pos % seq) == 0
        seq_end = (pos % seq) == seq - 1

    def conv(w_ref, cw_ref, cb_ref, u_ref):
        u_ref[...] = jnp.dot(h, w_ref[...], preferred_element_type=F32)
        cw = cw_ref[...]
        before = u_ref[FFN_HALO - 1:FFN_HALO - 1 + tm, :]
        after = u_ref[FFN_HALO + 1:FFN_HALO + 1 + tm, :]
        if seq < tm:
            before = jnp.where(seq_start, 0.0, before)
            after = jnp.where(seq_end, 0.0, after)
        return (before * cw[0:1] + u_ref[FFN_HALO:FFN_HALO + tm, :] * cw[1:2] + after * cw[2:3]
                + cb_ref[...])

    gate = conv(wg_ref, cwg_ref, cbg_ref, ug_ref)
    val = conv(wv_ref, cwv_ref, cbv_ref, uv_ref)
    a = (gate * jax.nn.sigmoid(gate) * val).astype(BF16)
    o_ref[...] += jnp.dot(a, wd_ref[...], preferred_element_type=F32)

    @pl.when(j == pl.num_programs(1) - 1)
    def _():
        o_ref[...] = x_ref[...] + mod_ref[0, 5:6, :] * o_ref[...]


def _ffn(x2, h2, mod, mod_row, w_up, conv_w, conv_b, w_down, seq, tm, tn=512):
    t, d = x2.shape
    dff = w_down.shape[0]
    assert t % tm == 0 and (seq % tm == 0 or tm % seq == 0)
    nj = dff // tn
    hb = tm // FFN_HALO
    nhb = t // FFN_HALO
    return pl.pallas_call(
        functools.partial(_ffn_kernel, seq=seq),
        grid=(t // tm, nj),
        in_specs=[pl.BlockSpec((FFN_HALO, d), lambda i, j: (jnp.maximum(i * hb - 1, 0), 0)),
                  pl.BlockSpec((tm, d), lambda i, j: (i, 0)),
                  pl.BlockSpec((FFN_HALO, d), lambda i, j: (jnp.minimum((i + 1) * hb, nhb - 1), 0)),
                  pl.BlockSpec((tm, d), lambda i, j: (i, 0)),
                  pl.BlockSpec((1, N_MOD, d), lambda i, j: (mod_row(i), 0, 0)),
                  pl.BlockSpec((d, tn), lambda i, j: (0, j)),
                  pl.BlockSpec((d, tn), lambda i, j: (0, nj + j)),
                  pl.BlockSpec((CONV_W, tn), lambda i, j: (0, j)),
                  pl.BlockSpec((CONV_W, tn), lambda i, j: (0, nj + j)),
                  pl.BlockSpec((1, tn), lambda i, j: (0, j)),
                  pl.BlockSpec((1, tn), lambda i, j: (0, nj + j)),
                  pl.BlockSpec((tn, d), lambda i, j: (j, 0))],
        out_specs=pl.BlockSpec((tm, d), lambda i, j: (i, 0)),
        out_shape=jax.ShapeDtypeStruct((t, d), F32),
        scratch_shapes=[pltpu.VMEM((tm + 2 * FFN_HALO, d), BF16),
                        pltpu.VMEM((tm + 2 * FFN_HALO, tn), F32), pltpu.VMEM((tm + 2 * FFN_HALO, tn), F32)],
        compiler_params=_cp(("parallel", "arbitrary")),
        name="conv_ffn",
    )(h2, h2, h2, x2, mod, w_up, w_up, conv_w, conv_w, conv_b, conv_b, w_down)


def _block_ones(group):
    idx = np.arange(GW) // group
    return jnp.asarray(idx[:, None] == idx[None, :], dtype=BF16)


def _pad_heads(w, heads, width):
    lead = w.shape[:-1]
    w = w.reshape(lead + (heads, width))
    w = jnp.pad(w, [(0, 0)] * len(lead) + [(0, 0), (0, LANES - width)])
    return w.reshape(lead + (heads * LANES,))


def _layer_weights(P, l):
    d = P["w_in"].shape[1]
    w_in = P["w_in"][l]
    o = np.cumsum((0, 512, 512, 512, MLA_Q_LORA, MLA_KV_LORA, MLA_ROPE, 512, 512, 512, 512))
    col = lambda k: w_in[:, o[k]:o[k + 1]]
    w_all = jnp.concatenate(
        [col(0), col(1), col(2), col(6), col(7), col(8), col(9), col(3), col(4), col(5),
         jnp.zeros((d, ML_W - MLA_Q_LORA - MLA_KV_LORA - MLA_ROPE), F32)], axis=1).astype(BF16)
    wukv = P["mla_w_ukv"][l].reshape(MLA_KV_LORA, MLA_H, MLA_NOPE + MLA_V)
    wuk = _pad_heads(wukv[:, :, :MLA_NOPE].reshape(MLA_KV_LORA, MLA_H * MLA_NOPE), MLA_H, MLA_NOPE)
    wuv = wukv[:, :, MLA_NOPE:].reshape(MLA_KV_LORA, MLA_H * MLA_V)
    tile = lambda g, reps: jnp.tile(g.astype(F32), reps)
    padw = lambda g: jnp.pad(g.astype(F32), (0, GW - g.shape[0]))
    gains = jnp.stack([
        tile(P["diff_qn_g"][l], 8) * (DIFF_DH ** -0.5 * LOG2E),
        tile(P["diff_kn_g"][l], 8),
        tile(P["na_qn_g"][l], 8) * (NA_DH ** -0.5 * LOG2E),
        tile(P["na_kn_g"][l], 8),
        _pad_heads(tile(P["mla_qn_g"][l], MLA_H), MLA_H, MLA_QK) * (MLA_QK ** -0.5 * LOG2E),
        _pad_heads(tile(P["mla_kn_g"][l], MLA_H), MLA_H, MLA_QK),
        padw(P["mla_qa_g"][l]),
        padw(P["mla_kva_g"][l]),
    ])
    return dict(
        w_in=w_all,
        wuq=_pad_heads(P["mla_w_uq"][l], MLA_H, MLA_QK).astype(BF16),
        wuk=wuk.astype(BF16), wuv=wuv.astype(BF16),
        gains=gains, g64=_block_ones(64), g128=_block_ones(LANES),
        norm1=P["norm1_g"][l].reshape(1, d).astype(F32),
        norm2=P["norm2_g"][l].reshape(1, d).astype(F32),
        lam=P["diff_lam"][l].astype(F32),
        subg=P["diff_sub_g"][l].reshape(1, 2 * DIFF_DH).astype(F32),
        pool_w=P["pool_w"][l].astype(BF16),
        pool_scale=P["pool_scale"][l].reshape(1, GW).astype(F32),
        w_out=P["w_out"][l].astype(BF16),
        w_up=P["w_up"][l].astype(BF16),
        conv_w=P["conv_w"][l].astype(F32),
        conv_b=P["conv_b"][l].reshape(1, -1).astype(F32),
        w_down=P["w_down"][l].astype(BF16),
    )


def _rope_tables(n):
    t = jnp.arange(n)
    rows = (t // GRID_W).astype(F32)
    cols = (t % GRID_W).astype(F32)

    def ang(dim):
        quarter = dim // 4
        inv = ROPE_BASE ** (-jnp.arange(quarter, dtype=F32) / quarter)
        return jnp.concatenate([rows[:, None] * inv, cols[:, None] * inv], axis=-1)

    a = ang(DIFF_DH)
    cos_a, sin_a = jnp.cos(a), jnp.sin(a)
    cd = jnp.tile(cos_a, (1, 4))
    sd = jnp.tile(jnp.concatenate([-sin_a, sin_a], axis=1), (1, 2))
    b = ang(MLA_ROPE)
    cos_b, sin_b = jnp.cos(b), jnp.sin(b)
    ones = jnp.ones((n, MLA_NOPE), F32)
    pad = LANES - MLA_NOPE - MLA_ROPE
    cm = jnp.concatenate([ones, cos_b, cos_b, jnp.ones((n, pad), F32)], axis=1)
    sm = jnp.concatenate([0.0 * ones, -sin_b, sin_b, jnp.zeros((n, pad), F32)], axis=1)
    return jnp.stack([cd, sd, cm, sm])


def _layer(x, mod, mod_row_of_batch, W, layer, cache, rope_tab, bias_tab):
    b, n, d = x.shape
    t = b * n
    x2 = x.reshape(t, d)
    lam_init = 0.8 - 0.6 * math.exp(-0.3 * layer)
    tm = min(512, n)
    row_in = lambda i: mod_row_of_batch(i // (n // tm))
    outs = _inproj(x2, mod, row_in, W["norm1"], W, rope_tab, n, states=cache is None, tm=tm)
    qd, kd, vd, qn, kn, vn, pu, qm, km, vm = [a.reshape(b, n, GW) for a in outs[:10]]
    if cache is None:
        oa = _attention("diff", qd, [(kd, vd)], n, n, W["lam"], W["subg"], lam_init)
        ob = _attention("full", qm, [(km, vm)], n, n)
        oc = _attention("pair", qn, [(kn, vn)], n, n)
        state = outs[10:]
    else:
        a_k, a_v, b_ckv, b_kpe, c_k, c_v = cache
        pc = a_k.shape[1]
        ck = a_k.reshape(b, pc, GW).astype(BF16)
        cv = a_v.reshape(b, pc, GW).astype(BF16)
        oa = _attention_latent("diff", qd, ck, cv, kd, vd, W["gains"][0:1], DIFF_DH, W["lam"], W["subg"],
                               lam_init)
        kpe_pad = jnp.pad(b_kpe.reshape(b * pc, MLA_ROPE), ((0, 0), (0, LANES - MLA_ROPE)))
        kmc, vmc = _mla_cache(b_ckv.reshape(b * pc, MLA_KV_LORA), kpe_pad, W)
        ob = _attention_latent("full", qm, kmc.reshape(b, pc, GW), vmc.reshape(b, pc, GW), km, vm,
                               W["gains"][4:5], MLA_QK)
        nk = c_k.reshape(b, pc, GW).astype(BF16)
        nv = c_v.reshape(b, pc, GW).astype(BF16)
        oc = _na_latent(qn, kn, vn, nk, nv, bias_tab)
        state = ()
    od = _pool(pu, W["pool_w"], W["pool_scale"])
    flat = lambda a: a.reshape(t, GW)
    x2, h2 = _outproj(flat(oa), flat(ob), flat(oc), flat(od), x2, mod, row_in, W["norm2"], W["w_out"], tm=tm)
    tmf = min(512, t) if cache is None else min(512, n)
    row_ffn = lambda i: mod_row_of_batch((i * tmf) // n)
    x2 = _ffn(x2, h2, mod, row_ffn, W["w_up"], W["conv_w"], W["conv_b"], W["w_down"], n, tm=tmf)
    return x2.reshape(b, n, d), state


def kernel(x_prompt, x_sample, cache_diff_k, cache_diff_v, cache_mla_ckv, cache_mla_kpe, cache_na_k, cache_na_v, c, c_ctx, norm1_g, norm2_g, ada_w, ada_b, w_in, diff_qn_g, diff_kn_g, diff_lam, diff_sub_g, mla_qa_g, mla_kva_g, mla_w_uq, mla_w_ukv, mla_qn_g, mla_kn_g, na_qn_g, na_kn_g, na_bias, pool_w, pool_scale, w_out, w_up, conv_w, conv_b, w_down):
    P = dict(norm1_g=norm1_g, norm2_g=norm2_g, w_in=w_in, diff_qn_g=diff_qn_g, diff_kn_g=diff_kn_g,
             diff_lam=diff_lam, diff_sub_g=diff_sub_g, mla_qa_g=mla_qa_g, mla_kva_g=mla_kva_g,
             mla_w_uq=mla_w_uq, mla_w_ukv=mla_w_ukv, mla_qn_g=mla_qn_g, mla_kn_g=mla_kn_g,
             na_qn_g=na_qn_g, na_kn_g=na_kn_g, pool_w=pool_w, pool_scale=pool_scale, w_out=w_out,
             w_up=w_up, conv_w=conv_w, conv_b=conv_b, w_down=w_down)
    depth = w_in.shape[0]
    d = x_prompt.shape[-1]
    bd, nd = x_sample.shape[:2]
    n_rows = -(-(bd + 1) // 8) * 8
    cv = jnp.concatenate([c, c_ctx[None, :], jnp.zeros((n_rows - bd - 1, d), F32)], axis=0)
    mod_all = _ada(cv, ada_w, ada_b).reshape(depth, n_rows, N_MOD, d)
    rope_tab = _rope_tables(nd)
    Ws = [_layer_weights(P, l) for l in range(depth)]

    xp = x_prompt
    states = []
    for l in range(depth):
        xp, st = _layer(xp, mod_all[l], lambda bi: bd, Ws[l], l, None, None, None)
        states.append(st)
    xs = x_sample
    for l in range(depth):
        cache_l = (cache_diff_k[:, l], cache_diff_v[:, l], cache_mla_ckv[:, l], cache_mla_kpe[:, l],
                   cache_na_k[:, l], cache_na_v[:, l])
        bias_tab = _na_bias_table(na_bias[l], nd // GRID_W)
        xs, _ = _layer(xs, mod_all[l], lambda bi: bi, Ws[l], l, cache_l, rope_tab, bias_tab)

    bp, npr = x_prompt.shape[:2]

    def stack(k, shape):
        return jnp.stack([s[k].reshape((bp, npr) + shape) for s in states], axis=1)

    return (xp, xs,
            stack(0, (DIFF_H, 2, DIFF_DH)), stack(1, (DIFF_H, 2 * DIFF_DH)),
            stack(2, (MLA_KV_LORA,)), stack(3, (MLA_ROPE,)),
            stack(4, (NA_H, NA_DH)), stack(5, (NA_H, NA_DH)))
```

```python
import functools
import math

import numpy as np
import jax
import jax.numpy as jnp
from jax import lax
from jax.experimental import pallas as pl
from jax.experimental.pallas import tpu as pltpu

F32 = jnp.float32
BF16 = jnp.bfloat16

GRID_W = 64
ROPE_BASE = 10000.0
EPS = 1e-6
N_MOD = 6
DIFF_H = 4
DIFF_DH = 64
MLA_H = 4
MLA_NOPE = 64
MLA_ROPE = 32
MLA_V = 128
MLA_Q_LORA = 384
MLA_KV_LORA = 128
NA_H = 8
NA_DH = 64
NA_ROWS = 8
NA_COLS = 16
POOL_WINDOWS = (2, 4, 8, 16)
POOL_G = 4
POOL_C = 128
CONV_W = 3

GW = 512
LANES = 128
NSLOT = GW // LANES
MXU_DIM = 256
MLA_QK = MLA_NOPE + MLA_ROPE
V7X_VMEM_LIMIT = 56 * 1024 * 1024
NEG = -1e30
LOG2E = math.log2(math.e)

C_DQ, C_DK, C_DV, C_NQ, C_NK, C_NV, C_PU, C_ML = 0, 512, 1024, 1536, 2048, 2560, 3072, 3584
ML_W = 640
IN_WP = C_ML + ML_W


def _cp(sem, vmem=V7X_VMEM_LIMIT):
    return pltpu.CompilerParams(dimension_semantics=sem, vmem_limit_bytes=vmem)


def _const_spec(shape):
    nd = len(shape)
    return pl.BlockSpec(shape, lambda *_: (0,) * nd, pipeline_mode=pl.Buffered(1))


def _ada_kernel(c_ref, w_ref, b_ref, o_ref):
    c = c_ref[...]
    s = c * jax.nn.sigmoid(c)
    o_ref[0] = jnp.dot(s.astype(BF16), w_ref[0].astype(BF16), preferred_element_type=F32) + b_ref[0]


def _ada(cv, ada_w, ada_b, tn=1024):
    depth, d, n = ada_w.shape
    r = cv.shape[0]
    return pl.pallas_call(
        _ada_kernel,
        grid=(depth, n // tn),
        in_specs=[pl.BlockSpec((r, d), lambda l, j: (0, 0)),
                  pl.BlockSpec((1, d, tn), lambda l, j: (l, 0, j)),
                  pl.BlockSpec((1, 1, tn), lambda l, j: (l, 0, j))],
        out_specs=pl.BlockSpec((1, r, tn), lambda l, j: (l, 0, j)),
        out_shape=jax.ShapeDtypeStruct((depth, r, n), F32),
        compiler_params=_cp(("parallel", "parallel")),
        name="adaln",
    )(cv, ada_w, ada_b.reshape(depth, 1, n))


def _modulated(x, g, mod, k):
    ms = jnp.mean(x * x, axis=-1, keepdims=True)
    y = x * lax.rsqrt(ms + EPS) * g
    return y * (1.0 + mod[k + 1:k + 2, :]) + mod[k:k + 1, :]


def _group_rms(z, gsum_ref, inv_n, gain):
    sq = (z * z).astype(BF16)
    ss = jnp.concatenate(
        [jnp.dot(sq[:, c:c + MXU_DIM], gsum_ref[c:c + MXU_DIM, c:c + MXU_DIM], preferred_element_type=F32)
         for c in range(0, z.shape[1], MXU_DIM)], axis=1)
    return z * lax.rsqrt(ss * inv_n + EPS) * gain


def _rope_chunks(z, cos_ref, sin_ref, lo_pred, shift_lo, shift_hi):
    c = cos_ref[...]
    s = sin_ref[...]
    outs = []
    for j in range(z.shape[1] // LANES):
        zc = z[:, j * LANES:(j + 1) * LANES]
        partner = jnp.where(lo_pred, pltpu.roll(zc, shift_lo, 1), pltpu.roll(zc, shift_hi, 1))
        outs.append(zc * c + partner * s)
    return jnp.concatenate(outs, axis=1)


def _mla_kv(ckv_b, kp, wuk_ref, wuv_ref, g128_ref, kgain):
    kp_slot = pltpu.roll(kp, MLA_NOPE, 1)
    kh = jnp.dot(ckv_b, wuk_ref[...], preferred_element_type=F32) + jnp.concatenate([kp_slot] * MLA_H, axis=1)
    k = _group_rms(kh, g128_ref, 1.0 / MLA_QK, kgain)
    v = jnp.dot(ckv_b, wuv_ref[...], preferred_element_type=F32)
    return k, v


def _inproj_kernel(*refs, rope, states):
    it = iter(refs)
    x_ref, mod_ref, g_ref, w_ref, wuq_ref, wuk_ref, wuv_ref = (next(it) for _ in range(7))
    gains_ref, g64_ref, g128_ref = (next(it) for _ in range(3))
    if rope:
        rope_ref = next(it)
    qd_ref, kd_ref, vd_ref, qn_ref, kn_ref, vn_ref, pu_ref, qm_ref, km_ref, vm_ref = (
        next(it) for _ in range(10))
    if states:
        ska_ref, sva_ref, sckv_ref, skpe_ref, skc_ref, svc_ref = (next(it) for _ in range(6))

    h = _modulated(x_ref[...], g_ref[...], mod_ref[0], 0).astype(BF16)
    gains = gains_ref[...]
    lane = lax.broadcasted_iota(jnp.int32, (1, LANES), 1)

    def seg(c0, w=GW):
        return jnp.dot(h, w_ref[:, c0:c0 + w], preferred_element_type=F32)

    def rope_d(z):
        if not rope:
            return z
        return _rope_chunks(z, rope_ref.at[0], rope_ref.at[1], (lane % 64) < 32, LANES - 32, 32)

    def rope_m(z):
        if not rope:
            return z
        return _rope_chunks(z, rope_ref.at[2], rope_ref.at[3], lane < MLA_NOPE + MLA_ROPE // 2,
                            LANES - MLA_ROPE // 2, MLA_ROPE // 2)

    qd_ref[...] = rope_d(_group_rms(seg(C_DQ), g64_ref, 1.0 / DIFF_DH, gains[0:1])).astype(BF16)
    kd = _group_rms(seg(C_DK), g64_ref, 1.0 / DIFF_DH, gains[1:2])
    kd_ref[...] = rope_d(kd).astype(BF16)
    vd = seg(C_DV)
    vd_ref[...] = vd.astype(BF16)
    qn_ref[...] = _group_rms(seg(C_NQ), g64_ref, 1.0 / NA_DH, gains[2:3]).astype(BF16)
    kn = _group_rms(seg(C_NK), g64_ref, 1.0 / NA_DH, gains[3:4])
    kn_ref[...] = kn.astype(BF16)
    vn = seg(C_NV)
    vn_ref[...] = vn.astype(BF16)
    pu_ref[...] = seg(C_PU)
    zm = seg(C_ML, ML_W)
    cq = zm[:, :MLA_Q_LORA]
    cq = cq * lax.rsqrt(jnp.mean(cq * cq, axis=-1, keepdims=True) + EPS) * gains[6:7, :MLA_Q_LORA]
    qh = jnp.dot(cq.astype(BF16), wuq_ref[...], preferred_element_type=F32)
    qm_ref[...] = rope_m(_group_rms(qh, g128_ref, 1.0 / MLA_QK, gains[4:5])).astype(BF16)
    ck = zm[:, MLA_Q_LORA:MLA_Q_LORA + MLA_KV_LORA]
    ckv = ck * lax.rsqrt(jnp.mean(ck * ck, axis=-1, keepdims=True) + EPS) * gains[7:8, :MLA_KV_LORA]
    kp = zm[:, GW:GW + LANES]
    km, vm = _mla_kv(ckv.astype(BF16), kp, wuk_ref, wuv_ref, g128_ref, gains[5:6])
    km_ref[...] = rope_m(km).astype(BF16)
    vm_ref[...] = vm.astype(BF16)
    if states:
        ska_ref[...] = kd
        sva_ref[...] = vd
        sckv_ref[...] = ckv
        skpe_ref[...] = kp[:, :MLA_ROPE]
        skc_ref[...] = kn
        svc_ref[...] = vn


def _inproj(x2, mod, mod_row, g1, W, rope_tab, seq, states, tm=256):
    t, d = x2.shape
    rope = rope_tab is not None
    tiles_per_seq = seq // tm
    tok = lambda w: pl.BlockSpec((tm, w), lambda i: (i, 0))
    in_specs = [tok(d),
                pl.BlockSpec((1, N_MOD, d), lambda i: (mod_row(i), 0, 0)),
                _const_spec((1, d)),
                _const_spec(W["w_in"].shape), _const_spec(W["wuq"].shape),
                _const_spec(W["wuk"].shape), _const_spec(W["wuv"].shape),
                _const_spec(W["gains"].shape), _const_spec(W["g64"].shape),
                _const_spec(W["g128"].shape)]
    args = [x2, mod, g1, W["w_in"], W["wuq"], W["wuk"], W["wuv"], W["gains"], W["g64"], W["g128"]]
    if rope:
        in_specs.append(pl.BlockSpec((4, tm, LANES), lambda i: (0, i % tiles_per_seq, 0)))
        args.append(rope_tab)
    out_specs = [tok(GW)] * 10
    out_shape = [jax.ShapeDtypeStruct((t, GW), BF16)] * 6 + [jax.ShapeDtypeStruct((t, GW), F32)] + \
                [jax.ShapeDtypeStruct((t, GW), BF16)] * 3
    if states:
        widths = (GW, GW, MLA_KV_LORA, MLA_ROPE, GW, GW)
        out_specs += [tok(w) for w in widths]
        out_shape += [jax.ShapeDtypeStruct((t, w), F32) for w in widths]
    return pl.pallas_call(
        functools.partial(_inproj_kernel, rope=rope, states=states),
        grid=(t // tm,),
        in_specs=in_specs, out_specs=out_specs, out_shape=out_shape,
        compiler_params=_cp(("parallel",)),
        name="inproj",
    )(*args)


def _mla_cache_kernel(ckv_ref, kp_ref, wuk_ref, wuv_ref, g128_ref, gains_ref, k_ref, v_ref):
    k, v = _mla_kv(ckv_ref[...].astype(BF16), kp_ref[...], wuk_ref, wuv_ref, g128_ref, gains_ref[5:6, :])
    k_ref[...] = k.astype(BF16)
    v_ref[...] = v.astype(BF16)


def _mla_cache(ckv2, kp2, W, tm=256):
    t = ckv2.shape[0]
    tok = lambda w: pl.BlockSpec((tm, w), lambda i: (i, 0))
    return pl.pallas_call(
        _mla_cache_kernel,
        grid=(t // tm,),
        in_specs=[tok(MLA_KV_LORA), tok(LANES), _const_spec(W["wuk"].shape), _const_spec(W["wuv"].shape),
                  _const_spec(W["g128"].shape), _const_spec(W["gains"].shape)],
        out_specs=[tok(GW), tok(GW)],
        out_shape=[jax.ShapeDtypeStruct((t, GW), BF16)] * 2,
        compiler_params=_cp(("parallel",)),
        name="mla_cache_kv",
    )(ckv2, kp2, W["wuk"], W["wuv"], W["g128"], W["gains"])


def _diff_lambda(lam_ref, lam_init):
    lf = lam_ref[...]
    return (jnp.exp(jnp.sum(lf[0:1] * lf[1:2], axis=-1, keepdims=True))
            - jnp.exp(jnp.sum(lf[2:3] * lf[3:4], axis=-1, keepdims=True)) + lam_init)


def _stream_queries(q, mode):
    if mode == "full":
        return [q]
    lane = lax.broadcasted_iota(jnp.int32, (1, LANES), 1)
    zero = jnp.zeros_like(q)
    return [jnp.where(lane < 64, q, zero), jnp.where(lane < 64, zero, q)]


def _combine_streams(outs, mode, lam, subg_ref, lam_init):
    if mode == "full":
        return outs[0]
    if mode == "pair":
        lane = lax.broadcasted_iota(jnp.int32, (1, LANES), 1)
        return jnp.where(lane < 64, outs[0], outs[1])
    o = outs[0] - lam * outs[1]
    o = o * lax.rsqrt(jnp.mean(o * o, axis=-1, keepdims=True) + EPS) * subg_ref[...]
    return o * (1.0 - lam_init)


def _online_attend(q_ref, srcs, chunks, mode, lam, subg_ref, lam_init, o_ref):
    tq = q_ref.shape[1]
    nstream = 1 if mode == "full" else 2
    for slot in range(NSLOT):
        sl = slice(slot * LANES, (slot + 1) * LANES)
        qs = _stream_queries(q_ref[0, :, sl], mode)

        def step(k_c, v_cs, state):
            new = []
            for si in range(nstream):
                m, l, acc = state[3 * si:3 * si + 3]
                s = lax.dot_general(qs[si], k_c, (((1,), (1,)), ((), ())), preferred_element_type=F32)
                m_new = jnp.maximum(m, jnp.max(s, axis=-1, keepdims=True))
                alpha = jnp.exp2(m - m_new)
                p = jnp.exp2(s - m_new)
                l = alpha * l + jnp.sum(p, axis=-1, keepdims=True)
                acc = alpha * acc + jnp.dot(p.astype(BF16), v_cs[si % len(v_cs)],
                                            preferred_element_type=F32)
                new += [m_new, l, acc]
            return tuple(new)

        state = ()
        for _ in range(nstream):
            state += (jnp.full((tq, 1), NEG, F32), jnp.zeros((tq, 1), F32), jnp.zeros((tq, LANES), F32))
        for (k_ref, v_refs), (n_chunk, tk) in zip(srcs, chunks):
            if n_chunk == 1:
                state = step(k_ref[0, :, sl], [v[0, :, sl] for v in v_refs], state)
            else:
                def body(c, st, k_ref=k_ref, v_refs=v_refs, tk=tk):
                    rows = pl.ds(pl.multiple_of(c * tk, tk), tk)
                    return step(k_ref[0, rows, sl], [v[0, rows, sl] for v in v_refs], st)
                state = lax.fori_loop(0, n_chunk, body, state)

        outs = [state[3 * si + 2] / state[3 * si + 1] for si in range(nstream)]
        o_ref[0, :, sl] = _combine_streams(outs, mode, lam, subg_ref, lam_init).astype(o_ref.dtype)


def _attn_kernel(*refs, mode, chunks, lam_init):
    it = iter(refs)
    q_ref = next(it)
    srcs = []
    for _ in chunks:
        k_ref = next(it)
        srcs.append((k_ref, (next(it),)))
    lam, subg_ref = None, None
    if mode == "diff":
        lam = _diff_lambda(next(it), lam_init)
        subg_ref = next(it)
    o_ref = next(it)
    _online_attend(q_ref, srcs, chunks, mode, lam, subg_ref, lam_init, o_ref)


AUG = 2 * LANES
SHIFT_MARGIN = 1.0 + 2.0 ** -6
SHIFT_LIMIT = 48.0
ATTN_SUB = 512
VT_ROWS = LANES + 16


def _attn_fast_kernel(*refs, mode, chunks, lam_init, head_dim):
    it = iter(refs)
    q_ref, kc_ref, vc_ref, k_ref, v_ref, qgain_ref = (next(it) for _ in range(6))
    lam, subg_ref = None, None
    if mode == "diff":
        lam = _diff_lambda(next(it), lam_init)
        subg_ref = next(it)
    o_ref, kaug_ref, vext_ref, shift_ref = (next(it) for _ in range(4))

    tq = q_ref.shape[1]
    pc, n = kc_ref.shape[1], k_ref.shape[1]
    kt = pc + n
    nstream = 1 if mode == "full" else 2
    lane = lax.broadcasted_iota(jnp.int32, (1, LANES), 1)

    @pl.when(pl.program_id(1) == 0)
    def _():
        q_bound = math.sqrt(head_dim) * jnp.max(jnp.abs(qgain_ref[...]), axis=-1, keepdims=True)
        shift_ref[...] = jnp.zeros_like(shift_ref)
        for slot in range(NSLOT):
            sl = slice(slot * LANES, (slot + 1) * LANES)
            a0 = slot * AUG
            kaug_ref[0:pc, a0:a0 + LANES] = kc_ref[0, :, sl]
            kaug_ref[pc:kt, a0:a0 + LANES] = k_ref[0, :, sl]
            r0 = slot * VT_ROWS
            for c0 in range(0, kt, MXU_DIM):
                chunk = (vc_ref[0, c0:c0 + MXU_DIM, sl] if c0 < pc
                         else v_ref[0, c0 - pc:c0 - pc + MXU_DIM, sl])
                vext_ref[r0:r0 + LANES, c0:c0 + MXU_DIM] = chunk.astype(F32).T.astype(BF16)
            sub = lax.broadcasted_iota(jnp.int32, (VT_ROWS - LANES, 1), 0)
            vext_ref[r0 + LANES:r0 + VT_ROWS, :] = jnp.broadcast_to(
                jnp.where(sub == 0, 1.0, 0.0).astype(BF16), (VT_ROWS - LANES, kt))
            shift_lanes = jnp.zeros((1, LANES), F32)
            for si in range(nstream):
                best = None
                for src in (kc_ref, k_ref):
                    kk = _stream_queries(src[0, :, sl], mode)[si].astype(F32)
                    nrm = jnp.max(jnp.sum(kk * kk, axis=-1, keepdims=True), axis=0, keepdims=True)
                    best = nrm if best is None else jnp.maximum(best, nrm)
                shift = q_bound * jnp.sqrt(best) * SHIFT_MARGIN
                row = slot * nstream + si
                shift_ref[row:row + 1, :] = jnp.broadcast_to(shift, (1, LANES))
                shift_lanes = jnp.where(lane == si, -shift, shift_lanes)
            kaug_ref[:, a0 + LANES:a0 + AUG] = jnp.broadcast_to(shift_lanes.astype(BF16), (kt, LANES))

    safe = jnp.max(shift_ref[...]) <= SHIFT_LIMIT

    @pl.when(safe)
    def _():
        def sub_tile(t, carry):
            rows = pl.ds(pl.multiple_of(t * ATTN_SUB, ATTN_SUB), ATTN_SUB)
            for slot in range(NSLOT):
                sl = slice(slot * LANES, (slot + 1) * LANES)
                a0 = slot * AUG
                outs = []
                for si, qm in enumerate(_stream_queries(q_ref[0, rows, sl], mode)):
                    pick = jnp.broadcast_to(jnp.where(lane == si, 1.0, 0.0).astype(BF16), (ATTN_SUB, LANES))
                    qa = jnp.concatenate([qm, pick], axis=1)
                    s_t = lax.dot_general(kaug_ref[:, a0:a0 + AUG], qa, (((1,), (1,)), ((), ())),
                                          preferred_element_type=F32)
                    acc_t = jnp.dot(vext_ref[slot * VT_ROWS:(slot + 1) * VT_ROWS, :],
                                    jnp.exp2(s_t).astype(BF16), preferred_element_type=F32)
                    outs.append((acc_t[0:LANES] / acc_t[LANES:LANES + 1]).T)
                o = _combine_streams(outs, mode, lam, subg_ref, lam_init)
                o_ref[0, rows, sl] = o.astype(o_ref.dtype)
            return carry
        lax.fori_loop(0, tq // ATTN_SUB, sub_tile, 0)

    @pl.when(jnp.logical_not(safe))
    def _():
        _online_attend(q_ref, [(kc_ref, (vc_ref,)), (k_ref, (v_ref,))], chunks, mode, lam, subg_ref,
                       lam_init, o_ref)


def _attention_latent(mode, q, kc, vc, k, v, qgain, head_dim, lam=None, subg=None, lam_init=0.0,
                      tq=512, tk=512):
    b, n, _ = q.shape
    pc = kc.shape[1]
    kt = pc + n
    res = lambda m: pl.BlockSpec((1, m, GW), lambda bi, qi: (bi, 0, 0))
    in_specs = [pl.BlockSpec((1, tq, GW), lambda bi, qi: (bi, qi, 0)), res(pc), res(pc), res(n), res(n),
                _const_spec(qgain.shape)]
    args = [q, kc, vc, k, v, qgain]
    if mode == "diff":
        in_specs += [_const_spec(lam.shape), _const_spec(subg.shape)]
        args += [lam, subg]
    chunks = ((1, pc), (n // tk, tk))
    return pl.pallas_call(
        functools.partial(_attn_fast_kernel, mode=mode, chunks=chunks, lam_init=lam_init, head_dim=head_dim),
        grid=(b, n // tq),
        in_specs=in_specs,
        out_specs=pl.BlockSpec((1, tq, GW), lambda bi, qi: (bi, qi, 0)),
        out_shape=jax.ShapeDtypeStruct((b, n, GW), BF16),
        scratch_shapes=[pltpu.VMEM((kt, NSLOT * AUG), BF16), pltpu.VMEM((NSLOT * VT_ROWS, kt), BF16),
                        pltpu.VMEM((8, LANES), F32)],
        compiler_params=_cp(("parallel", "arbitrary")),
        name="attn_latent_" + mode,
    )(*args)


def _attention(mode, q, srcs, tq, tk, lam=None, subg=None, lam_init=0.0):
    b, n, _ = q.shape
    in_specs = [pl.BlockSpec((1, tq, GW), lambda bi, qi: (bi, qi, 0))]
    args = [q]
    chunks = []
    for src in srcs:
        ks = src[0].shape[1]
        step = min(tk, ks)
        chunks.append((ks // step, step))
        for a in src:
            in_specs.append(pl.BlockSpec((1, ks, GW), lambda bi, qi: (bi, 0, 0)))
            args.append(a)
    if mode == "diff":
        in_specs += [_const_spec(lam.shape), _const_spec(subg.shape)]
        args += [lam, subg]
    return pl.pallas_call(
        functools.partial(_attn_kernel, mode=mode, chunks=tuple(chunks), lam_init=lam_init),
        grid=(b, n // tq),
        in_specs=in_specs,
        out_specs=pl.BlockSpec((1, tq, GW), lambda bi, qi: (bi, qi, 0)),
        out_shape=jax.ShapeDtypeStruct((b, n, GW), BF16),
        compiler_params=_cp(("parallel", "parallel")),
        name="attn_" + mode,
    )(*args)


NA_QROWS = 4
NA_KROWS = NA_QROWS + NA_ROWS


def _na_key_start(r0, rows):
    return jnp.clip(r0 - NA_ROWS // 2, 0, rows - NA_KROWS)


def _na_kernel(q_ref, k_ref, v_ref, kc_ref, vc_ref, bias_ref, o_ref, *, rows):
    r0 = pl.program_id(1) * NA_QROWS
    ks = _na_key_start(r0, rows)
    loc = pl.ds(pl.multiple_of(ks * GRID_W, GRID_W), NA_KROWS * GRID_W)
    lane = lax.broadcasted_iota(jnp.int32, (1, LANES), 1)
    nt = (((1,), (1,)), ((), ()))
    for slot in range(NSLOT):
        sl = slice(slot * LANES, (slot + 1) * LANES)
        q = q_ref[0, :, sl]
        k_loc = k_ref[0, loc, sl]
        k_ctx = kc_ref[0, :, sl]
        v_loc = v_ref[0, loc, sl]
        v_ctx = vc_ref[0, :, sl]
        v_loc = jnp.concatenate([v_loc, jnp.ones_like(v_loc)], axis=1)
        v_ctx = jnp.concatenate([v_ctx, jnp.ones_like(v_ctx)], axis=1)
        halves = []
        for half, qh in enumerate(_stream_queries(q, "pair")):
            s_loc = lax.dot_general(qh, k_loc, nt, preferred_element_type=F32) + bias_ref[0, 2 * slot + half]
            s_ctx = lax.dot_general(qh, k_ctx, nt, preferred_element_type=F32)
            m = jnp.maximum(jnp.max(s_loc, axis=-1, keepdims=True), jnp.max(s_ctx, axis=-1, keepdims=True))
            acc = (jnp.dot(jnp.exp2(s_loc - m).astype(BF16), v_loc, preferred_element_type=F32)
                   + jnp.dot(jnp.exp2(s_ctx - m).astype(BF16), v_ctx, preferred_element_type=F32))
            halves.append(acc[:, 0:LANES] / acc[:, LANES:2 * LANES])
        o_ref[0, :, sl] = jnp.where(lane < 64, halves[0], halves[1]).astype(o_ref.dtype)


def _na_bias_table(bias_tab, rows):
    assert rows >= NA_KROWS + NA_QROWS and rows % NA_QROWS == 0
    qc = np.arange(GRID_W)[:, None]
    kc = np.arange(GRID_W)[None, :]
    cstart = np.clip(qc - NA_COLS // 2, 0, GRID_W - NA_COLS)
    in_win = (kc >= cstart) & (kc < cstart + NA_COLS)
    dc_idx = np.clip(kc - qc, 1 - NA_COLS, NA_COLS - 1) + NA_COLS - 1
    col_pick = (dc_idx[:, :, None] == np.arange(2 * NA_COLS - 1)).astype(np.float32)
    variants = (0, NA_ROWS // 2, NA_ROWS)
    row_pick = np.zeros((len(variants), NA_QROWS, NA_KROWS, 2 * NA_ROWS - 1), np.float32)
    for vi, var in enumerate(variants):
        for j in range(NA_QROWS):
            first = min(max(j + var - NA_ROWS // 2, 0), NA_KROWS - NA_ROWS)
            for i in range(first, first + NA_ROWS):
                row_pick[vi, j, i, i - j - var + NA_ROWS - 1] = 1.0
    valid = (row_pick.sum(-1) > 0)[:, None, :, None, :, None] & in_win[None, None, None, :, None, :]
    toep = jnp.einsum("qkd,hrd->hrqk", jnp.asarray(col_pick), bias_tab.astype(F32),
                      precision=lax.Precision.HIGHEST)
    tab = jnp.einsum("vjir,hrqk->vhjqik", jnp.asarray(row_pick), toep, precision=lax.Precision.HIGHEST)
    tab = jnp.where(jnp.asarray(valid), tab * LOG2E, NEG)
    return tab.reshape(len(variants), NA_H, NA_QROWS * GRID_W, NA_KROWS * GRID_W)


def _na_latent(q, k, v, kc, vc, bias):
    b, n, _ = q.shape
    rows = n // GRID_W
    pc = kc.shape[1]
    full = lambda m: pl.BlockSpec((1, m, GW), lambda bi, r: (bi, 0, 0))
    tq = NA_QROWS * GRID_W
    return pl.pallas_call(
        functools.partial(_na_kernel, rows=rows),
        grid=(b, rows // NA_QROWS),
        in_specs=[pl.BlockSpec((1, tq, GW), lambda bi, r: (bi, r, 0)),
                  full(n), full(n), full(pc), full(pc),
                  pl.BlockSpec((1, NA_H, tq, NA_KROWS * GRID_W),
                               lambda bi, r: ((r * NA_QROWS - _na_key_start(r * NA_QROWS, rows))
                                              // (NA_ROWS // 2), 0, 0, 0))],
        out_specs=pl.BlockSpec((1, tq, GW), lambda bi, r: (bi, r, 0)),
        out_shape=jax.ShapeDtypeStruct((b, n, GW), BF16),
        compiler_params=_cp(("parallel", "parallel")),
        name="attn_neighbourhood",
    )(q, k, v, kc, vc, bias)


POOL_HALO = 64


def _pool_kernel(prev_ref, main_ref, next_ref, band_ref, inv_ref, w_ref, scale_ref, o_ref):
    ext = jnp.concatenate([prev_ref[0], main_ref[0], next_ref[0]], axis=0)
    ext_hi = ext.astype(BF16)
    ext_lo = (ext - ext_hi.astype(F32)).astype(BF16)
    for g in range(POOL_G):
        sl = slice(g * POOL_C, (g + 1) * POOL_C)
        tot = jnp.dot(band_ref[0, g], jnp.concatenate([ext_hi[:, sl], ext_lo[:, sl]], axis=1),
                      preferred_element_type=F32)
        mean = (tot[:, :POOL_C] + tot[:, POOL_C:]) * inv_ref[0, :, sl]
        d = (mean - main_ref[0, :, sl]).astype(BF16)
        y = jnp.dot(d, w_ref[g], preferred_element_type=F32) * scale_ref[:, sl]
        o_ref[0, :, sl] = y.astype(o_ref.dtype)


def _pool_windows(n, tp):
    starts = (0, tp if n > 2 * tp else 0, n - tp, 0)
    seqs = (n if n > tp else 2 * tp, n if n > tp else 3 * tp, n, tp)
    bands = np.zeros((4, POOL_G, tp, tp + 2 * POOL_HALO), np.float32)
    inv = np.zeros((4, tp, GW), np.float32)
    for e, (t0, length) in enumerate(zip(starts, seqs)):
        t = t0 + np.arange(tp)[:, None]
        pos = t0 - POOL_HALO + np.arange(tp + 2 * POOL_HALO)[None, :]
        for g, win in enumerate(POOL_WINDOWS):
            lo = np.clip(t - win // 2, 0, length)
            hi = np.clip(t - win // 2 + win, 0, length)
            bands[e, g] = (pos >= lo) & (pos < hi)
            inv[e, :, g * POOL_C:(g + 1) * POOL_C] = 1.0 / (hi - lo)
    return jnp.asarray(bands, dtype=BF16), jnp.asarray(inv)


def _pool(pu, w_pool, scale, tp=256):
    b, n, _ = pu.shape
    hb = tp // POOL_HALO
    nt = n // tp
    last = n // POOL_HALO - 1
    bands, inv = _pool_windows(n, tp)
    edge = lambda i: jnp.where(i == 0, 0, jnp.where(i == nt - 1, 2, 1)) if nt > 1 else 3
    return pl.pallas_call(
        _pool_kernel,
        grid=(b, nt),
        in_specs=[pl.BlockSpec((1, POOL_HALO, GW), lambda bi, i: (bi, jnp.maximum(i * hb - 1, 0), 0)),
                  pl.BlockSpec((1, tp, GW), lambda bi, i: (bi, i, 0)),
                  pl.BlockSpec((1, POOL_HALO, GW), lambda bi, i: (bi, jnp.minimum((i + 1) * hb, last), 0)),
                  pl.BlockSpec((1,) + bands.shape[1:], lambda bi, i: (edge(i), 0, 0, 0)),
                  pl.BlockSpec((1,) + inv.shape[1:], lambda bi, i: (edge(i), 0, 0)),
                  _const_spec(w_pool.shape), _const_spec(scale.shape)],
        out_specs=pl.BlockSpec((1, tp, GW), lambda bi, i: (bi, i, 0)),
        out_shape=jax.ShapeDtypeStruct((b, n, GW), BF16),
        compiler_params=_cp(("parallel", "parallel")),
        name="pool_mixer",
    )(pu, pu, pu, bands, inv, w_pool, scale)


def _outproj_kernel(oa_ref, ob_ref, oc_ref, od_ref, x_ref, mod_ref, g2_ref, w_ref, o_ref, h_ref):
    mix = jnp.dot(oa_ref[...], w_ref[0:GW, :], preferred_element_type=F32)
    mix += jnp.dot(ob_ref[...], w_ref[GW:2 * GW, :], preferred_element_type=F32)
    mix += jnp.dot(oc_ref[...], w_ref[2 * GW:3 * GW, :], preferred_element_type=F32)
    mix += jnp.dot(od_ref[...], w_ref[3 * GW:4 * GW, :], preferred_element_type=F32)
    mod = mod_ref[0]
    x = x_ref[...] + mod[2:3, :] * mix
    o_ref[...] = x
    h_ref[...] = _modulated(x, g2_ref[...], mod, 3).astype(BF16)


def _outproj(oa, ob, oc, od, x2, mod, mod_row, g2, w_out, tm=256):
    t, d = x2.shape
    tok = lambda w: pl.BlockSpec((tm, w), lambda i: (i, 0))
    return pl.pallas_call(
        _outproj_kernel,
        grid=(t // tm,),
        in_specs=[tok(GW)] * 4 + [tok(d), pl.BlockSpec((1, N_MOD, d), lambda i: (mod_row(i), 0, 0)),
                                   _const_spec((1, d)), _const_spec(w_out.shape)],
        out_specs=[tok(d), tok(d)],
        out_shape=[jax.ShapeDtypeStruct((t, d), F32), jax.ShapeDtypeStruct((t, d), BF16)],
        compiler_params=_cp(("parallel",)),
        name="outproj",
    )(oa, ob, oc, od, x2, mod, g2, w_out)


FFN_HALO = 16


def _ffn_kernel(prev_ref, hm_ref, next_ref, x_ref, mod_ref, wg_ref, wv_ref, cwg_ref, cwv_ref, cbg_ref,
                cbv_ref, wd_ref, o_ref, h_ref, ug_ref, uv_ref, *, seq):
    i = pl.program_id(0)
    j = pl.program_id(1)
    tm = x_ref.shape[0]
    tiles_per_seq = max(seq // tm, 1)

    @pl.when(j == 0)
    def _():
        first = (i % tiles_per_seq) == 0
        last = (i % tiles_per_seq) == tiles_per_seq - 1
        h_ref[0:FFN_HALO, :] = jnp.where(first, jnp.zeros_like(prev_ref), prev_ref[...])
        h_ref[FFN_HALO:FFN_HALO + tm, :] = hm_ref[...]
        h_ref[FFN_HALO + tm:, :] = jnp.where(last, jnp.zeros_like(next_ref), next_ref[...])
        o_ref[...] = jnp.zeros_like(o_ref)

    h = h_ref[...]
    if seq < tm:
        pos = lax.broadcasted_iota(jnp.int32, (tm, 1), 0)
        seq_start = (pos % seq) == 0
        seq_end = (pos % seq) == seq - 1

    def conv(w_ref, cw_ref, cb_ref, u_ref):
        u_ref[...] = jnp.dot(h, w_ref[...], preferred_element_type=F32)
        cw = cw_ref[...]
        before = u_ref[FFN_HALO - 1:FFN_HALO - 1 + tm, :]
        after = u_ref[FFN_HALO + 1:FFN_HALO + 1 + tm, :]
        if seq < tm:
            before = jnp.where(seq_start, 0.0, before)
            after = jnp.where(seq_end, 0.0, after)
        return (before * cw[0:1] + u_ref[FFN_HALO:FFN_HALO + tm, :] * cw[1:2] + after * cw[2:3]
                + cb_ref[...])

    gate = conv(wg_ref, cwg_ref, cbg_ref, ug_ref)
    val = conv(wv_ref, cwv_ref, cbv_ref, uv_ref)
    a = (gate * jax.nn.sigmoid(gate) * val).astype(BF16)
    o_ref[...] += jnp.dot(a, wd_ref[...], preferred_element_type=F32)

    @pl.when(j == pl.num_programs(1) - 1)
    def _():
        o_ref[...] = x_ref[...] + mod_ref[0, 5:6, :] * o_ref[...]


def _ffn(x2, h2, mod, mod_row, w_up, conv_w, conv_b, w_down, layer, seq, tm, tn=512):
    t, d = x2.shape
    dff = w_down.shape[1]
    assert t % tm == 0 and (seq % tm == 0 or tm % seq == 0)
    nj = dff // tn
    hb = tm // FFN_HALO
    nhb = t // FFN_HALO
    return pl.pallas_call(
        functools.partial(_ffn_kernel, seq=seq),
        grid=(t // tm, nj),
        in_specs=[pl.BlockSpec((FFN_HALO, d), lambda i, j: (jnp.maximum(i * hb - 1, 0), 0)),
                  pl.BlockSpec((tm, d), lambda i, j: (i, 0)),
                  pl.BlockSpec((FFN_HALO, d), lambda i, j: (jnp.minimum((i + 1) * hb, nhb - 1), 0)),
                  pl.BlockSpec((tm, d), lambda i, j: (i, 0)),
                  pl.BlockSpec((1, N_MOD, d), lambda i, j: (mod_row(i), 0, 0)),
                  pl.BlockSpec((None, d, tn), lambda i, j: (layer, 0, j)),
                  pl.BlockSpec((None, d, tn), lambda i, j: (layer, 0, nj + j)),
                  pl.BlockSpec((CONV_W, tn), lambda i, j: (0, j)),
                  pl.BlockSpec((CONV_W, tn), lambda i, j: (0, nj + j)),
                  pl.BlockSpec((1, tn), lambda i, j: (0, j)),
                  pl.BlockSpec((1, tn), lambda i, j: (0, nj + j)),
                  pl.BlockSpec((None, tn, d), lambda i, j: (layer, j, 0))],
        out_specs=pl.BlockSpec((tm, d), lambda i, j: (i, 0)),
        out_shape=jax.ShapeDtypeStruct((t, d), F32),
        scratch_shapes=[pltpu.VMEM((tm + 2 * FFN_HALO, d), BF16),
                        pltpu.VMEM((tm + 2 * FFN_HALO, tn), F32), pltpu.VMEM((tm + 2 * FFN_HALO, tn), F32)],
        compiler_params=_cp(("parallel", "arbitrary")),
        name="conv_ffn",
    )(h2, h2, h2, x2, mod, w_up, w_up, conv_w, conv_w, conv_b, conv_b, w_down)


def _block_ones(group):
    idx = np.arange(GW) // group
    return jnp.asarray(idx[:, None] == idx[None, :], dtype=BF16)


def _pad_heads(w, heads, width):
    lead = w.shape[:-1]
    w = w.reshape(lead + (heads, width))
    w = jnp.pad(w, [(0, 0)] * len(lead) + [(0, 0), (0, LANES - width)])
    return w.reshape(lead + (heads * LANES,))


def _layer_weights(P, l):
    d = P["w_in"].shape[1]
    w_in = P["w_in"][l].astype(BF16)
    ml0, ml1 = 3 * GW, 3 * GW + MLA_Q_LORA + MLA_KV_LORA + MLA_ROPE
    w_all = jnp.concatenate([w_in[:, :ml0], w_in[:, ml1:], w_in[:, ml0:ml1],
                             jnp.zeros((d, ML_W - (ml1 - ml0)), BF16)], axis=1)
    wukv = P["mla_w_ukv"][l].reshape(MLA_KV_LORA, MLA_H, MLA_NOPE + MLA_V)
    wuk = _pad_heads(wukv[:, :, :MLA_NOPE].reshape(MLA_KV_LORA, MLA_H * MLA_NOPE), MLA_H, MLA_NOPE)
    wuv = wukv[:, :, MLA_NOPE:].reshape(MLA_KV_LORA, MLA_H * MLA_V)
    tile = lambda g, reps: jnp.tile(g.astype(F32), reps)
    padw = lambda g: jnp.pad(g.astype(F32), (0, GW - g.shape[0]))
    gains = jnp.stack([
        tile(P["diff_qn_g"][l], 8) * (DIFF_DH ** -0.5 * LOG2E),
        tile(P["diff_kn_g"][l], 8),
        tile(P["na_qn_g"][l], 8) * (NA_DH ** -0.5 * LOG2E),
        tile(P["na_kn_g"][l], 8),
        _pad_heads(tile(P["mla_qn_g"][l], MLA_H), MLA_H, MLA_QK) * (MLA_QK ** -0.5 * LOG2E),
        _pad_heads(tile(P["mla_kn_g"][l], MLA_H), MLA_H, MLA_QK),
        padw(P["mla_qa_g"][l]),
        padw(P["mla_kva_g"][l]),
    ])
    return dict(
        w_in=w_all,
        wuq=_pad_heads(P["mla_w_uq"][l], MLA_H, MLA_QK).astype(BF16),
        wuk=wuk.astype(BF16), wuv=wuv.astype(BF16),
        gains=gains, g64=_block_ones(64), g128=_block_ones(LANES),
        norm1=P["norm1_g"][l].reshape(1, d).astype(F32),
        norm2=P["norm2_g"][l].reshape(1, d).astype(F32),
        lam=P["diff_lam"][l].astype(F32),
        subg=P["diff_sub_g"][l].reshape(1, 2 * DIFF_DH).astype(F32),
        pool_w=P["pool_w"][l].astype(BF16),
        pool_scale=P["pool_scale"][l].reshape(1, GW).astype(F32),
        w_out=P["w_out"][l].astype(BF16),
        conv_w=P["conv_w"][l].astype(F32),
        conv_b=P["conv_b"][l].reshape(1, -1).astype(F32),
    )


def _rope_tables(n):
    t = jnp.arange(n)
    rows = (t // GRID_W).astype(F32)
    cols = (t % GRID_W).astype(F32)

    def ang(dim):
        quarter = dim // 4
        inv = ROPE_BASE ** (-jnp.arange(quarter, dtype=F32) / quarter)
        return jnp.concatenate([rows[:, None] * inv, cols[:, None] * inv], axis=-1)

    a = ang(DIFF_DH)
    cos_a, sin_a = jnp.cos(a), jnp.sin(a)
    cd = jnp.tile(cos_a, (1, 4))
    sd = jnp.tile(jnp.concatenate([-sin_a, sin_a], axis=1), (1, 2))
    b = ang(MLA_ROPE)
    cos_b, sin_b = jnp.cos(b), jnp.sin(b)
    ones = jnp.ones((n, MLA_NOPE), F32)
    pad = LANES - MLA_NOPE - MLA_ROPE
    cm = jnp.concatenate([ones, cos_b, cos_b, jnp.ones((n, pad), F32)], axis=1)
    sm = jnp.concatenate([0.0 * ones, -sin_b, sin_b, jnp.zeros((n, pad), F32)], axis=1)
    return jnp.stack([cd, sd, cm, sm])


def _layer(x, mod, mod_row_of_batch, W, layer, cache, rope_tab, bias_tab):
    b, n, d = x.shape
    t = b * n
    x2 = x.reshape(t, d)
    lam_init = 0.8 - 0.6 * math.exp(-0.3 * layer)
    tm = min(512, n)
    row_in = lambda i: mod_row_of_batch(i // (n // tm))
    outs = _inproj(x2, mod, row_in, W["norm1"], W, rope_tab, n, states=cache is None, tm=tm)
    qd, kd, vd, qn, kn, vn, pu, qm, km, vm = [a.reshape(b, n, GW) for a in outs[:10]]
    if cache is None:
        oa = _attention("diff", qd, [(kd, vd)], n, n, W["lam"], W["subg"], lam_init)
        ob = _attention("full", qm, [(km, vm)], n, n)
        oc = _attention("pair", qn, [(kn, vn)], n, n)
        state = outs[10:]
    else:
        a_k, a_v, b_ckv, b_kpe, c_k, c_v = cache
        pc = a_k.shape[1]
        ck = a_k.reshape(b, pc, GW).astype(BF16)
        cv = a_v.reshape(b, pc, GW).astype(BF16)
        oa = _attention_latent("diff", qd, ck, cv, kd, vd, W["gains"][0:1], DIFF_DH, W["lam"], W["subg"],
                               lam_init)
        kpe_pad = jnp.pad(b_kpe.reshape(b * pc, MLA_ROPE), ((0, 0), (0, LANES - MLA_ROPE)))
        kmc, vmc = _mla_cache(b_ckv.reshape(b * pc, MLA_KV_LORA), kpe_pad, W)
        ob = _attention_latent("full", qm, kmc.reshape(b, pc, GW), vmc.reshape(b, pc, GW), km, vm,
                               W["gains"][4:5], MLA_QK)
        nk = c_k.reshape(b, pc, GW).astype(BF16)
        nv = c_v.reshape(b, pc, GW).astype(BF16)
        oc = _na_latent(qn, kn, vn, nk, nv, bias_tab)
        state = ()
    od = _pool(pu, W["pool_w"], W["pool_scale"])
    flat = lambda a: a.reshape(t, GW)
    x2, h2 = _outproj(flat(oa), flat(ob), flat(oc), flat(od), x2, mod, row_in, W["norm2"], W["w_out"], tm=tm)
    tmf = min(512, t) if cache is None else min(512, n)
    row_ffn = lambda i: mod_row_of_batch((i * tmf) // n)
    x2 = _ffn(x2, h2, mod, row_ffn, W["w_up"], W["conv_w"], W["conv_b"], W["w_down"], layer, n, tm=tmf)
    return x2.reshape(b, n, d), state


def kernel(x_prompt, x_sample, cache_diff_k, cache_diff_v, cache_mla_ckv, cache_mla_kpe, cache_na_k, cache_na_v, c, c_ctx, norm1_g, norm2_g, ada_w, ada_b, w_in, diff_qn_g, diff_kn_g, diff_lam, diff_sub_g, mla_qa_g, mla_kva_g, mla_w_uq, mla_w_ukv, mla_qn_g, mla_kn_g, na_qn_g, na_kn_g, na_bias, pool_w, pool_scale, w_out, w_up, conv_w, conv_b, w_down):
    P = dict(norm1_g=norm1_g, norm2_g=norm2_g, w_in=w_in, diff_qn_g=diff_qn_g, diff_kn_g=diff_kn_g,
             diff_lam=diff_lam, diff_sub_g=diff_sub_g, mla_qa_g=mla_qa_g, mla_kva_g=mla_kva_g,
             mla_w_uq=mla_w_uq, mla_w_ukv=mla_w_ukv, mla_qn_g=mla_qn_g, mla_kn_g=mla_kn_g,
             na_qn_g=na_qn_g, na_kn_g=na_kn_g, pool_w=pool_w, pool_scale=pool_scale, w_out=w_out,
             w_up=w_up, conv_w=conv_w, conv_b=conv_b, w_down=w_down)
    depth = w_in.shape[0]
    d = x_prompt.shape[-1]
    bd, nd = x_sample.shape[:2]
    n_rows = -(-(bd + 1) // 8) * 8
    cv = jnp.concatenate([c, c_ctx[None, :], jnp.zeros((n_rows - bd - 1, d), F32)], axis=0)
    mod_all = _ada(cv, ada_w, ada_b).reshape(depth, n_rows, N_MOD, d)
    rope_tab = _rope_tables(nd)
    Ws = [_layer_weights(P, l) for l in range(depth)]
    w_up_all, w_down_all = w_up.astype(BF16), w_down.astype(BF16)
    for W in Ws:
        W.update(w_up=w_up_all, w_down=w_down_all)

    xp = x_prompt
    states = []
    for l in range(depth):
        xp, st = _layer(xp, mod_all[l], lambda bi: bd, Ws[l], l, None, None, None)
        states.append(st)
    xs = x_sample
    for l in range(depth):
        cache_l = (cache_diff_k[:, l], cache_diff_v[:, l], cache_mla_ckv[:, l], cache_mla_kpe[:, l],
                   cache_na_k[:, l], cache_na_v[:, l])
        bias_tab = _na_bias_table(na_bias[l], nd // GRID_W)
        xs, _ = _layer(xs, mod_all[l], lambda bi: bi, Ws[l], l, cache_l, rope_tab, bias_tab)

    bp, npr = x_prompt.shape[:2]

    def stack(k, shape):
        return jnp.stack([s[k].reshape((bp, npr) + shape) for s in states], axis=1)

    return (xp, xs,
            stack(0, (DIFF_H, 2, DIFF_DH)), stack(1, (DIFF_H, 2 * DIFF_DH)),
            stack(2, (MLA_KV_LORA,)), stack(3, (MLA_ROPE,)),
            stack(4, (NA_H, NA_DH)), stack(5, (NA_H, NA_DH)))
```

```python
import functools
import math

import numpy as np
import jax
import jax.numpy as jnp
from jax import lax
from jax.experimental import pallas as pl
from jax.experimental.pallas import tpu as pltpu

F32 = jnp.float32
BF16 = jnp.bfloat16

GRID_W = 64
ROPE_BASE = 10000.0
EPS = 1e-6
N_MOD = 6
DIFF_H = 4
DIFF_DH = 64
MLA_H = 4
MLA_NOPE = 64
MLA_ROPE = 32
MLA_V = 128
MLA_Q_LORA = 384
MLA_KV_LORA = 128
NA_H = 8
NA_DH = 64
NA_ROWS = 8
NA_COLS = 16
POOL_WINDOWS = (2, 4, 8, 16)
POOL_G = 4
POOL_C = 128
CONV_W = 3

GW = 512
LANES = 128
NSLOT = GW // LANES
MXU_DIM = 256
MLA_QK = MLA_NOPE + MLA_ROPE
V7X_VMEM_LIMIT = 56 * 1024 * 1024
NEG = -1e30
LOG2E = math.log2(math.e)

C_DQ, C_DK, C_DV, C_NQ, C_NK, C_NV, C_PU, C_ML = 0, 512, 1024, 1536, 2048, 2560, 3072, 3584
ML_W = 640
IN_WP = C_ML + ML_W


def _cp(sem, vmem=V7X_VMEM_LIMIT):
    return pltpu.CompilerParams(dimension_semantics=sem, vmem_limit_bytes=vmem)


def _const_spec(shape):
    nd = len(shape)
    return pl.BlockSpec(shape, lambda *_: (0,) * nd, pipeline_mode=pl.Buffered(1))


def _ada_kernel(c_ref, w_ref, b_ref, o_ref):
    c = c_ref[...]
    s = c * jax.nn.sigmoid(c)
    o_ref[0] = jnp.dot(s.astype(BF16), w_ref[0].astype(BF16), preferred_element_type=F32) + b_ref[0]


def _ada(cv, ada_w, ada_b, tn=1024):
    depth, d, n = ada_w.shape
    r = cv.shape[0]
    return pl.pallas_call(
        _ada_kernel,
        grid=(depth, n // tn),
        in_specs=[pl.BlockSpec((r, d), lambda l, j: (0, 0)),
                  pl.BlockSpec((1, d, tn), lambda l, j: (l, 0, j)),
                  pl.BlockSpec((1, 1, tn), lambda l, j: (l, 0, j))],
        out_specs=pl.BlockSpec((1, r, tn), lambda l, j: (l, 0, j)),
        out_shape=jax.ShapeDtypeStruct((depth, r, n), F32),
        compiler_params=_cp(("parallel", "parallel")),
        name="adaln",
    )(cv, ada_w, ada_b.reshape(depth, 1, n))


def _modulated(x, g, mod, k):
    ms = jnp.mean(x * x, axis=-1, keepdims=True)
    y = x * lax.rsqrt(ms + EPS) * g
    return y * (1.0 + mod[k + 1:k + 2, :]) + mod[k:k + 1, :]


def _group_rms(z, gsum_ref, inv_n, gain):
    sq = (z * z).astype(BF16)
    ss = jnp.concatenate(
        [jnp.dot(sq[:, c:c + MXU_DIM], gsum_ref[c:c + MXU_DIM, c:c + MXU_DIM], preferred_element_type=F32)
         for c in range(0, z.shape[1], MXU_DIM)], axis=1)
    return z * lax.rsqrt(ss * inv_n + EPS) * gain


def _rope_chunks(z, cos_ref, sin_ref, lo_pred, shift_lo, shift_hi):
    c = cos_ref[...]
    s = sin_ref[...]
    outs = []
    for j in range(z.shape[1] // LANES):
        zc = z[:, j * LANES:(j + 1) * LANES]
        partner = jnp.where(lo_pred, pltpu.roll(zc, shift_lo, 1), pltpu.roll(zc, shift_hi, 1))
        outs.append(zc * c + partner * s)
    return jnp.concatenate(outs, axis=1)


def _mla_kv(ckv_b, kp, wuk_ref, wuv_ref, g128_ref, kgain):
    kp_slot = pltpu.roll(kp, MLA_NOPE, 1)
    kh = jnp.dot(ckv_b, wuk_ref[...], preferred_element_type=F32) + jnp.concatenate([kp_slot] * MLA_H, axis=1)
    k = _group_rms(kh, g128_ref, 1.0 / MLA_QK, kgain)
    v = jnp.dot(ckv_b, wuv_ref[...], preferred_element_type=F32)
    return k, v


def _inproj_kernel(*refs, rope, states):
    it = iter(refs)
    x_ref, mod_ref, g_ref, w_ref, wuq_ref, wuk_ref, wuv_ref = (next(it) for _ in range(7))
    gains_ref, g64_ref, g128_ref = (next(it) for _ in range(3))
    if rope:
        rope_ref = next(it)
    qd_ref, kd_ref, vd_ref, qn_ref, kn_ref, vn_ref, pu_ref, qm_ref, km_ref, vm_ref = (
        next(it) for _ in range(10))
    if states:
        ska_ref, sva_ref, sckv_ref, skpe_ref, skc_ref, svc_ref = (next(it) for _ in range(6))

    h = _modulated(x_ref[...], g_ref[...], mod_ref[0], 0).astype(BF16)
    gains = gains_ref[...]
    lane = lax.broadcasted_iota(jnp.int32, (1, LANES), 1)

    def seg(c0, w=GW):
        return jnp.dot(h, w_ref[:, c0:c0 + w], preferred_element_type=F32)

    def rope_d(z):
        if not rope:
            return z
        return _rope_chunks(z, rope_ref.at[0], rope_ref.at[1], (lane % 64) < 32, LANES - 32, 32)

    def rope_m(z):
        if not rope:
            return z
        return _rope_chunks(z, rope_ref.at[2], rope_ref.at[3], lane < MLA_NOPE + MLA_ROPE // 2,
                            LANES - MLA_ROPE // 2, MLA_ROPE // 2)

    qd_ref[...] = rope_d(_group_rms(seg(C_DQ), g64_ref, 1.0 / DIFF_DH, gains[0:1])).astype(BF16)
    kd = _group_rms(seg(C_DK), g64_ref, 1.0 / DIFF_DH, gains[1:2])
    kd_ref[...] = rope_d(kd).astype(BF16)
    vd = seg(C_DV)
    vd_ref[...] = vd.astype(BF16)
    qn_ref[...] = _group_rms(seg(C_NQ), g64_ref, 1.0 / NA_DH, gains[2:3]).astype(BF16)
    kn = _group_rms(seg(C_NK), g64_ref, 1.0 / NA_DH, gains[3:4])
    kn_ref[...] = kn.astype(BF16)
    vn = seg(C_NV)
    vn_ref[...] = vn.astype(BF16)
    pu_ref[...] = seg(C_PU)
    zm = seg(C_ML, ML_W)
    cq = zm[:, :MLA_Q_LORA]
    cq = cq * lax.rsqrt(jnp.mean(cq * cq, axis=-1, keepdims=True) + EPS) * gains[6:7, :MLA_Q_LORA]
    qh = jnp.dot(cq.astype(BF16), wuq_ref[...], preferred_element_type=F32)
    qm_ref[...] = rope_m(_group_rms(qh, g128_ref, 1.0 / MLA_QK, gains[4:5])).astype(BF16)
    ck = zm[:, MLA_Q_LORA:MLA_Q_LORA + MLA_KV_LORA]
    ckv = ck * lax.rsqrt(jnp.mean(ck * ck, axis=-1, keepdims=True) + EPS) * gains[7:8, :MLA_KV_LORA]
    kp = zm[:, GW:GW + LANES]
    km, vm = _mla_kv(ckv.astype(BF16), kp, wuk_ref, wuv_ref, g128_ref, gains[5:6])
    km_ref[...] = rope_m(km).astype(BF16)
    vm_ref[...] = vm.astype(BF16)
    if states:
        ska_ref[...] = kd
        sva_ref[...] = vd
        sckv_ref[...] = ckv
        skpe_ref[...] = kp[:, :MLA_ROPE]
        skc_ref[...] = kn
        svc_ref[...] = vn


def _inproj(x2, mod, mod_row, g1, W, rope_tab, seq, states, tm=256):
    t, d = x2.shape
    rope = rope_tab is not None
    tiles_per_seq = seq // tm
    tok = lambda w: pl.BlockSpec((tm, w), lambda i: (i, 0))
    in_specs = [tok(d),
                pl.BlockSpec((1, N_MOD, d), lambda i: (mod_row(i), 0, 0)),
                _const_spec((1, d)),
                _const_spec(W["w_in"].shape), _const_spec(W["wuq"].shape),
                _const_spec(W["wuk"].shape), _const_spec(W["wuv"].shape),
                _const_spec(W["gains"].shape), _const_spec(W["g64"].shape),
                _const_spec(W["g128"].shape)]
    args = [x2, mod, g1, W["w_in"], W["wuq"], W["wuk"], W["wuv"], W["gains"], W["g64"], W["g128"]]
    if rope:
        in_specs.append(pl.BlockSpec((4, tm, LANES), lambda i: (0, i % tiles_per_seq, 0)))
        args.append(rope_tab)
    out_specs = [tok(GW)] * 10
    out_shape = [jax.ShapeDtypeStruct((t, GW), BF16)] * 6 + [jax.ShapeDtypeStruct((t, GW), F32)] + \
                [jax.ShapeDtypeStruct((t, GW), BF16)] * 3
    if states:
        widths = (GW, GW, MLA_KV_LORA, MLA_ROPE, GW, GW)
        out_specs += [tok(w) for w in widths]
        out_shape += [jax.ShapeDtypeStruct((t, w), F32) for w in widths]
    return pl.pallas_call(
        functools.partial(_inproj_kernel, rope=rope, states=states),
        grid=(t // tm,),
        in_specs=in_specs, out_specs=out_specs, out_shape=out_shape,
        compiler_params=_cp(("parallel",)),
        name="inproj",
    )(*args)


def _mla_cache_kernel(ckv_ref, kp_ref, wuk_ref, wuv_ref, g128_ref, gains_ref, k_ref, v_ref):
    k, v = _mla_kv(ckv_ref[...].astype(BF16), kp_ref[...], wuk_ref, wuv_ref, g128_ref, gains_ref[5:6, :])
    k_ref[...] = k.astype(BF16)
    v_ref[...] = v.astype(BF16)


def _mla_cache(ckv2, kp2, W, tm=256):
    t = ckv2.shape[0]
    tok = lambda w: pl.BlockSpec((tm, w), lambda i: (i, 0))
    return pl.pallas_call(
        _mla_cache_kernel,
        grid=(t // tm,),
        in_specs=[tok(MLA_KV_LORA), tok(LANES), _const_spec(W["wuk"].shape), _const_spec(W["wuv"].shape),
                  _const_spec(W["g128"].shape), _const_spec(W["gains"].shape)],
        out_specs=[tok(GW), tok(GW)],
        out_shape=[jax.ShapeDtypeStruct((t, GW), BF16)] * 2,
        compiler_params=_cp(("parallel",)),
        name="mla_cache_kv",
    )(ckv2, kp2, W["wuk"], W["wuv"], W["g128"], W["gains"])


def _diff_lambda(lam_ref, lam_init):
    lf = lam_ref[...]
    return (jnp.exp(jnp.sum(lf[0:1] * lf[1:2], axis=-1, keepdims=True))
            - jnp.exp(jnp.sum(lf[2:3] * lf[3:4], axis=-1, keepdims=True)) + lam_init)


def _stream_queries(q, mode):
    if mode == "full":
        return [q]
    lane = lax.broadcasted_iota(jnp.int32, (1, LANES), 1)
    zero = jnp.zeros_like(q)
    return [jnp.where(lane < 64, q, zero), jnp.where(lane < 64, zero, q)]


def _combine_streams(outs, mode, lam, subg_ref, lam_init):
    if mode == "full":
        return outs[0]
    if mode == "pair":
        lane = lax.broadcasted_iota(jnp.int32, (1, LANES), 1)
        return jnp.where(lane < 64, outs[0], outs[1])
    o = outs[0] - lam * outs[1]
    o = o * lax.rsqrt(jnp.mean(o * o, axis=-1, keepdims=True) + EPS) * subg_ref[...]
    return o * (1.0 - lam_init)


def _online_attend(q_ref, srcs, chunks, mode, lam, subg_ref, lam_init, o_ref):
    tq = q_ref.shape[1]
    nstream = 1 if mode == "full" else 2
    for slot in range(NSLOT):
        sl = slice(slot * LANES, (slot + 1) * LANES)
        qs = _stream_queries(q_ref[0, :, sl], mode)

        def step(k_c, v_cs, state):
            new = []
            for si in range(nstream):
                m, l, acc = state[3 * si:3 * si + 3]
                s = lax.dot_general(qs[si], k_c, (((1,), (1,)), ((), ())), preferred_element_type=F32)
                m_new = jnp.maximum(m, jnp.max(s, axis=-1, keepdims=True))
                alpha = jnp.exp2(m - m_new)
                p = jnp.exp2(s - m_new)
                l = alpha * l + jnp.sum(p, axis=-1, keepdims=True)
                acc = alpha * acc + jnp.dot(p.astype(BF16), v_cs[si % len(v_cs)],
                                            preferred_element_type=F32)
                new += [m_new, l, acc]
            return tuple(new)

        state = ()
        for _ in range(nstream):
            state += (jnp.full((tq, 1), NEG, F32), jnp.zeros((tq, 1), F32), jnp.zeros((tq, LANES), F32))
        for (k_ref, v_refs), (n_chunk, tk) in zip(srcs, chunks):
            if n_chunk == 1:
                state = step(k_ref[0, :, sl], [v[0, :, sl] for v in v_refs], state)
            else:
                def body(c, st, k_ref=k_ref, v_refs=v_refs, tk=tk):
                    rows = pl.ds(pl.multiple_of(c * tk, tk), tk)
                    return step(k_ref[0, rows, sl], [v[0, rows, sl] for v in v_refs], st)
                state = lax.fori_loop(0, n_chunk, body, state)

        outs = [state[3 * si + 2] / state[3 * si + 1] for si in range(nstream)]
        o_ref[0, :, sl] = _combine_streams(outs, mode, lam, subg_ref, lam_init).astype(o_ref.dtype)


def _attn_kernel(*refs, mode, chunks, lam_init):
    it = iter(refs)
    q_ref = next(it)
    srcs = []
    for _ in chunks:
        k_ref = next(it)
        srcs.append((k_ref, (next(it),)))
    lam, subg_ref = None, None
    if mode == "diff":
        lam = _diff_lambda(next(it), lam_init)
        subg_ref = next(it)
    o_ref = next(it)
    _online_attend(q_ref, srcs, chunks, mode, lam, subg_ref, lam_init, o_ref)


AUG = 2 * LANES
SHIFT_MARGIN = 1.0 + 2.0 ** -6
SHIFT_LIMIT = 48.0
ATTN_SUB = 1024
VT_ROWS = LANES + 16


def _attn_fast_kernel(*refs, mode, chunks, lam_init, head_dim):
    it = iter(refs)
    q_ref, kc_ref, vc_ref, k_ref, v_ref, qgain_ref = (next(it) for _ in range(6))
    lam, subg_ref = None, None
    if mode == "diff":
        lam = _diff_lambda(next(it), lam_init)
        subg_ref = next(it)
    o_ref, kaug_ref, vext_ref, shift_ref = (next(it) for _ in range(4))

    tq = q_ref.shape[1]
    pc, n = kc_ref.shape[1], k_ref.shape[1]
    kt = pc + n
    nstream = 1 if mode == "full" else 2
    lane = lax.broadcasted_iota(jnp.int32, (1, LANES), 1)

    @pl.when(pl.program_id(1) == 0)
    def _():
        q_bound = math.sqrt(head_dim) * jnp.max(jnp.abs(qgain_ref[...]), axis=-1, keepdims=True)
        shift_ref[...] = jnp.zeros_like(shift_ref)
        for slot in range(NSLOT):
            sl = slice(slot * LANES, (slot + 1) * LANES)
            a0 = slot * AUG
            kaug_ref[0:pc, a0:a0 + LANES] = kc_ref[0, :, sl]
            kaug_ref[pc:kt, a0:a0 + LANES] = k_ref[0, :, sl]
            r0 = slot * VT_ROWS
            for c0 in range(0, kt, MXU_DIM):
                chunk = (vc_ref[0, c0:c0 + MXU_DIM, sl] if c0 < pc
                         else v_ref[0, c0 - pc:c0 - pc + MXU_DIM, sl])
                vext_ref[r0:r0 + LANES, c0:c0 + MXU_DIM] = chunk.astype(F32).T.astype(BF16)
            sub = lax.broadcasted_iota(jnp.int32, (VT_ROWS - LANES, 1), 0)
            vext_ref[r0 + LANES:r0 + VT_ROWS, :] = jnp.broadcast_to(
                jnp.where(sub == 0, 1.0, 0.0).astype(BF16), (VT_ROWS - LANES, kt))
            shift_lanes = jnp.zeros((1, LANES), F32)
            for si in range(nstream):
                best = None
                for src in (kc_ref, k_ref):
                    kk = _stream_queries(src[0, :, sl], mode)[si].astype(F32)
                    nrm = jnp.max(jnp.sum(kk * kk, axis=-1, keepdims=True), axis=0, keepdims=True)
                    best = nrm if best is None else jnp.maximum(best, nrm)
                shift = q_bound * jnp.sqrt(best) * SHIFT_MARGIN
                row = slot * nstream + si
                shift_ref[row:row + 1, :] = jnp.broadcast_to(shift, (1, LANES))
                shift_lanes = jnp.where(lane == si, -shift, shift_lanes)
            kaug_ref[:, a0 + LANES:a0 + AUG] = jnp.broadcast_to(shift_lanes.astype(BF16), (kt, LANES))

    safe = jnp.max(shift_ref[...]) <= SHIFT_LIMIT

    @pl.when(safe)
    def _():
        def sub_tile(t, carry):
            rows = pl.ds(pl.multiple_of(t * ATTN_SUB, ATTN_SUB), ATTN_SUB)
            for slot in range(NSLOT):
                sl = slice(slot * LANES, (slot + 1) * LANES)
                a0 = slot * AUG
                outs = []
                for si, qm in enumerate(_stream_queries(q_ref[0, rows, sl], mode)):
                    pick = jnp.broadcast_to(jnp.where(lane == si, 1.0, 0.0).astype(BF16), (ATTN_SUB, LANES))
                    qa = jnp.concatenate([qm, pick], axis=1)
                    s_t = lax.dot_general(kaug_ref[:, a0:a0 + AUG], qa, (((1,), (1,)), ((), ())),
                                          preferred_element_type=F32)
                    acc_t = jnp.dot(vext_ref[slot * VT_ROWS:(slot + 1) * VT_ROWS, :],
                                    jnp.exp2(s_t).astype(BF16), preferred_element_type=F32)
                    outs.append((acc_t[0:LANES] / acc_t[LANES:LANES + 1]).T)
                o = _combine_streams(outs, mode, lam, subg_ref, lam_init)
                o_ref[0, rows, sl] = o.astype(o_ref.dtype)
            return carry
        lax.fori_loop(0, tq // ATTN_SUB, sub_tile, 0)

    @pl.when(jnp.logical_not(safe))
    def _():
        _online_attend(q_ref, [(kc_ref, (vc_ref,)), (k_ref, (v_ref,))], chunks, mode, lam, subg_ref,
                       lam_init, o_ref)


def _attention_latent(mode, q, kc, vc, k, v, qgain, head_dim, lam=None, subg=None, lam_init=0.0,
                      tq=1024, tk=512):
    b, n, _ = q.shape
    pc = kc.shape[1]
    kt = pc + n
    res = lambda m: pl.BlockSpec((1, m, GW), lambda bi, qi: (bi, 0, 0))
    in_specs = [pl.BlockSpec((1, tq, GW), lambda bi, qi: (bi, qi, 0)), res(pc), res(pc), res(n), res(n),
                _const_spec(qgain.shape)]
    args = [q, kc, vc, k, v, qgain]
    if mode == "diff":
        in_specs += [_const_spec(lam.shape), _const_spec(subg.shape)]
        args += [lam, subg]
    chunks = ((1, pc), (n // tk, tk))
    return pl.pallas_call(
        functools.partial(_attn_fast_kernel, mode=mode, chunks=chunks, lam_init=lam_init, head_dim=head_dim),
        grid=(b, n // tq),
        in_specs=in_specs,
        out_specs=pl.BlockSpec((1, tq, GW), lambda bi, qi: (bi, qi, 0)),
        out_shape=jax.ShapeDtypeStruct((b, n, GW), BF16),
        scratch_shapes=[pltpu.VMEM((kt, NSLOT * AUG), BF16), pltpu.VMEM((NSLOT * VT_ROWS, kt), BF16),
                        pltpu.VMEM((8, LANES), F32)],
        compiler_params=_cp(("parallel", "arbitrary")),
        name="attn_latent_" + mode,
    )(*args)


def _attention(mode, q, srcs, tq, tk, lam=None, subg=None, lam_init=0.0):
    b, n, _ = q.shape
    in_specs = [pl.BlockSpec((1, tq, GW), lambda bi, qi: (bi, qi, 0))]
    args = [q]
    chunks = []
    for src in srcs:
        ks = src[0].shape[1]
        step = min(tk, ks)
        chunks.append((ks // step, step))
        for a in src:
            in_specs.append(pl.BlockSpec((1, ks, GW), lambda bi, qi: (bi, 0, 0)))
            args.append(a)
    if mode == "diff":
        in_specs += [_const_spec(lam.shape), _const_spec(subg.shape)]
        args += [lam, subg]
    return pl.pallas_call(
        functools.partial(_attn_kernel, mode=mode, chunks=tuple(chunks), lam_init=lam_init),
        grid=(b, n // tq),
        in_specs=in_specs,
        out_specs=pl.BlockSpec((1, tq, GW), lambda bi, qi: (bi, qi, 0)),
        out_shape=jax.ShapeDtypeStruct((b, n, GW), BF16),
        compiler_params=_cp(("parallel", "parallel")),
        name="attn_" + mode,
    )(*args)


NA_QROWS = 4
NA_KROWS = NA_QROWS + NA_ROWS


def _na_key_start(r0, rows):
    return jnp.clip(r0 - NA_ROWS // 2, 0, rows - NA_KROWS)


def _na_kernel(q_ref, k_ref, v_ref, kc_ref, vc_ref, bias_ref, o_ref, *, rows):
    r0 = pl.program_id(1) * NA_QROWS
    ks = _na_key_start(r0, rows)
    loc = pl.ds(pl.multiple_of(ks * GRID_W, GRID_W), NA_KROWS * GRID_W)
    lane = lax.broadcasted_iota(jnp.int32, (1, LANES), 1)
    nt = (((1,), (1,)), ((), ()))
    for slot in range(NSLOT):
        sl = slice(slot * LANES, (slot + 1) * LANES)
        q = q_ref[0, :, sl]
        k_loc = k_ref[0, loc, sl]
        k_ctx = kc_ref[0, :, sl]
        v_loc = v_ref[0, loc, sl]
        v_ctx = vc_ref[0, :, sl]
        v_loc = jnp.concatenate([v_loc, jnp.ones_like(v_loc)], axis=1)
        v_ctx = jnp.concatenate([v_ctx, jnp.ones_like(v_ctx)], axis=1)
        halves = []
        for half, qh in enumerate(_stream_queries(q, "pair")):
            s_loc = lax.dot_general(qh, k_loc, nt, preferred_element_type=F32) + bias_ref[0, 2 * slot + half]
            s_ctx = lax.dot_general(qh, k_ctx, nt, preferred_element_type=F32)
            m = jnp.maximum(jnp.max(s_loc, axis=-1, keepdims=True), jnp.max(s_ctx, axis=-1, keepdims=True))
            acc = (jnp.dot(jnp.exp2(s_loc - m).astype(BF16), v_loc, preferred_element_type=F32)
                   + jnp.dot(jnp.exp2(s_ctx - m).astype(BF16), v_ctx, preferred_element_type=F32))
            halves.append(acc[:, 0:LANES] / acc[:, LANES:2 * LANES])
        o_ref[0, :, sl] = jnp.where(lane < 64, halves[0], halves[1]).astype(o_ref.dtype)


def _na_bias_table(bias_tab, rows):
    assert rows >= NA_KROWS + NA_QROWS and rows % NA_QROWS == 0
    qc = np.arange(GRID_W)[:, None]
    kc = np.arange(GRID_W)[None, :]
    cstart = np.clip(qc - NA_COLS // 2, 0, GRID_W - NA_COLS)
    in_win = (kc >= cstart) & (kc < cstart + NA_COLS)
    dc_idx = np.clip(kc - qc, 1 - NA_COLS, NA_COLS - 1) + NA_COLS - 1
    col_pick = (dc_idx[:, :, None] == np.arange(2 * NA_COLS - 1)).astype(np.float32)
    variants = (0, NA_ROWS // 2, NA_ROWS)
    row_pick = np.zeros((len(variants), NA_QROWS, NA_KROWS, 2 * NA_ROWS - 1), np.float32)
    for vi, var in enumerate(variants):
        for j in range(NA_QROWS):
            first = min(max(j + var - NA_ROWS // 2, 0), NA_KROWS - NA_ROWS)
            for i in range(first, first + NA_ROWS):
                row_pick[vi, j, i, i - j - var + NA_ROWS - 1] = 1.0
    valid = (row_pick.sum(-1) > 0)[:, None, :, None, :, None] & in_win[None, None, None, :, None, :]
    toep = jnp.einsum("qkd,hrd->hrqk", jnp.asarray(col_pick), bias_tab.astype(F32),
                      precision=lax.Precision.HIGHEST)
    tab = jnp.einsum("vjir,hrqk->vhjqik", jnp.asarray(row_pick), toep, precision=lax.Precision.HIGHEST)
    tab = jnp.where(jnp.asarray(valid), tab * LOG2E, NEG)
    return tab.reshape(len(variants), NA_H, NA_QROWS * GRID_W, NA_KROWS * GRID_W)


def _na_latent(q, k, v, kc, vc, bias):
    b, n, _ = q.shape
    rows = n // GRID_W
    pc = kc.shape[1]
    full = lambda m: pl.BlockSpec((1, m, GW), lambda bi, r: (bi, 0, 0))
    tq = NA_QROWS * GRID_W
    return pl.pallas_call(
        functools.partial(_na_kernel, rows=rows),
        grid=(b, rows // NA_QROWS),
        in_specs=[pl.BlockSpec((1, tq, GW), lambda bi, r: (bi, r, 0)),
                  full(n), full(n), full(pc), full(pc),
                  pl.BlockSpec((1, NA_H, tq, NA_KROWS * GRID_W),
                               lambda bi, r: ((r * NA_QROWS - _na_key_start(r * NA_QROWS, rows))
                                              // (NA_ROWS // 2), 0, 0, 0))],
        out_specs=pl.BlockSpec((1, tq, GW), lambda bi, r: (bi, r, 0)),
        out_shape=jax.ShapeDtypeStruct((b, n, GW), BF16),
        compiler_params=_cp(("parallel", "parallel")),
        name="attn_neighbourhood",
    )(q, k, v, kc, vc, bias)


POOL_HALO = 64


def _pool_kernel(prev_ref, main_ref, next_ref, band_ref, inv_ref, w_ref, scale_ref, o_ref):
    ext = jnp.concatenate([prev_ref[0], main_ref[0], next_ref[0]], axis=0)
    ext_hi = ext.astype(BF16)
    ext_lo = (ext - ext_hi.astype(F32)).astype(BF16)
    for g in range(POOL_G):
        sl = slice(g * POOL_C, (g + 1) * POOL_C)
        tot = jnp.dot(band_ref[0, g], jnp.concatenate([ext_hi[:, sl], ext_lo[:, sl]], axis=1),
                      preferred_element_type=F32)
        mean = (tot[:, :POOL_C] + tot[:, POOL_C:]) * inv_ref[0, :, sl]
        d = (mean - main_ref[0, :, sl]).astype(BF16)
        y = jnp.dot(d, w_ref[g], preferred_element_type=F32) * scale_ref[:, sl]
        o_ref[0, :, sl] = y.astype(o_ref.dtype)


def _pool_windows(n, tp):
    starts = (0, tp if n > 2 * tp else 0, n - tp, 0)
    seqs = (n if n > tp else 2 * tp, n if n > tp else 3 * tp, n, tp)
    bands = np.zeros((4, POOL_G, tp, tp + 2 * POOL_HALO), np.float32)
    inv = np.zeros((4, tp, GW), np.float32)
    for e, (t0, length) in enumerate(zip(starts, seqs)):
        t = t0 + np.arange(tp)[:, None]
        pos = t0 - POOL_HALO + np.arange(tp + 2 * POOL_HALO)[None, :]
        for g, win in enumerate(POOL_WINDOWS):
            lo = np.clip(t - win // 2, 0, length)
            hi = np.clip(t - win // 2 + win, 0, length)
            bands[e, g] = (pos >= lo) & (pos < hi)
            inv[e, :, g * POOL_C:(g + 1) * POOL_C] = 1.0 / (hi - lo)
    return jnp.asarray(bands, dtype=BF16), jnp.asarray(inv)


def _pool(pu, w_pool, scale, tp=256):
    b, n, _ = pu.shape
    hb = tp // POOL_HALO
    nt = n // tp
    last = n // POOL_HALO - 1
    bands, inv = _pool_windows(n, tp)
    edge = lambda i: jnp.where(i == 0, 0, jnp.where(i == nt - 1, 2, 1)) if nt > 1 else 3
    return pl.pallas_call(
        _pool_kernel,
        grid=(b, nt),
        in_specs=[pl.BlockSpec((1, POOL_HALO, GW), lambda bi, i: (bi, jnp.maximum(i * hb - 1, 0), 0)),
                  pl.BlockSpec((1, tp, GW), lambda bi, i: (bi, i, 0)),
                  pl.BlockSpec((1, POOL_HALO, GW), lambda bi, i: (bi, jnp.minimum((i + 1) * hb, last), 0)),
                  pl.BlockSpec((1,) + bands.shape[1:], lambda bi, i: (edge(i), 0, 0, 0)),
                  pl.BlockSpec((1,) + inv.shape[1:], lambda bi, i: (edge(i), 0, 0)),
                  _const_spec(w_pool.shape), _const_spec(scale.shape)],
        out_specs=pl.BlockSpec((1, tp, GW), lambda bi, i: (bi, i, 0)),
        out_shape=jax.ShapeDtypeStruct((b, n, GW), BF16),
        compiler_params=_cp(("parallel", "parallel")),
        name="pool_mixer",
    )(pu, pu, pu, bands, inv, w_pool, scale)


def _outproj_kernel(oa_ref, ob_ref, oc_ref, od_ref, x_ref, mod_ref, g2_ref, w_ref, o_ref, h_ref):
    mix = jnp.dot(oa_ref[...], w_ref[0:GW, :], preferred_element_type=F32)
    mix += jnp.dot(ob_ref[...], w_ref[GW:2 * GW, :], preferred_element_type=F32)
    mix += jnp.dot(oc_ref[...], w_ref[2 * GW:3 * GW, :], preferred_element_type=F32)
    mix += jnp.dot(od_ref[...], w_ref[3 * GW:4 * GW, :], preferred_element_type=F32)
    mod = mod_ref[0]
    x = x_ref[...] + mod[2:3, :] * mix
    o_ref[...] = x
    h_ref[...] = _modulated(x, g2_ref[...], mod, 3).astype(BF16)


def _outproj(oa, ob, oc, od, x2, mod, mod_row, g2, w_out, tm=256):
    t, d = x2.shape
    tok = lambda w: pl.BlockSpec((tm, w), lambda i: (i, 0))
    return pl.pallas_call(
        _outproj_kernel,
        grid=(t // tm,),
        in_specs=[tok(GW)] * 4 + [tok(d), pl.BlockSpec((1, N_MOD, d), lambda i: (mod_row(i), 0, 0)),
                                   _const_spec((1, d)), _const_spec(w_out.shape)],
        out_specs=[tok(d), tok(d)],
        out_shape=[jax.ShapeDtypeStruct((t, d), F32), jax.ShapeDtypeStruct((t, d), BF16)],
        compiler_params=_cp(("parallel",)),
        name="outproj",
    )(oa, ob, oc, od, x2, mod, g2, w_out)


FFN_HALO = 16


def _ffn_kernel(prev_ref, hm_ref, next_ref, x_ref, mod_ref, wg_ref, wv_ref, cwg_ref, cwv_ref, cbg_ref,
                cbv_ref, wd_ref, o_ref, h_ref, ug_ref, uv_ref, *, seq):
    i = pl.program_id(0)
    j = pl.program_id(1)
    tm = x_ref.shape[0]
    tiles_per_seq = max(seq // tm, 1)

    @pl.when(j == 0)
    def _():
        first = (i % tiles_per_seq) == 0
        last = (i % tiles_per_seq) == tiles_per_seq - 1
        h_ref[0:FFN_HALO, :] = jnp.where(first, jnp.zeros_like(prev_ref), prev_ref[...])
        h_ref[FFN_HALO:FFN_HALO + tm, :] = hm_ref[...]
        h_ref[FFN_HALO + tm:, :] = jnp.where(last, jnp.zeros_like(next_ref), next_ref[...])
        o_ref[...] = jnp.zeros_like(o_ref)

    h = h_ref[...]
    if seq < tm:
        pos = lax.broadcasted_iota(jnp.int32, (tm, 1), 0)
        seq_start = (pos % seq) == 0
        seq_end = (pos % seq) == seq - 1

    def conv(w_ref, cw_ref, cb_ref, u_ref):
        u_ref[...] = jnp.dot(h, w_ref[...], preferred_element_type=F32)
        cw = cw_ref[...]
        before = u_ref[FFN_HALO - 1:FFN_HALO - 1 + tm, :]
        after = u_ref[FFN_HALO + 1:FFN_HALO + 1 + tm, :]
        if seq < tm:
            before = jnp.where(seq_start, 0.0, before)
            after = jnp.where(seq_end, 0.0, after)
        return (before * cw[0:1] + u_ref[FFN_HALO:FFN_HALO + tm, :] * cw[1:2] + after * cw[2:3]
                + cb_ref[...])

    gate = conv(wg_ref, cwg_ref, cbg_ref, ug_ref)
    val = conv(wv_ref, cwv_ref, cbv_ref, uv_ref)
    a = (gate * jax.nn.sigmoid(gate) * val).astype(BF16)
    o_ref[...] += jnp.dot(a, wd_ref[...], preferred_element_type=F32)

    @pl.when(j == pl.num_programs(1) - 1)
    def _():
        o_ref[...] = x_ref[...] + mod_ref[0, 5:6, :] * o_ref[...]


def _ffn(x2, h2, mod, mod_row, w_up, conv_w, conv_b, w_down, layer, seq, tm, tn=512):
    t, d = x2.shape
    dff = w_down.shape[1]
    assert t % tm == 0 and (seq % tm == 0 or tm % seq == 0)
    nj = dff // tn
    hb = tm // FFN_HALO
    nhb = t // FFN_HALO
    return pl.pallas_call(
        functools.partial(_ffn_kernel, seq=seq),
        grid=(t // tm, nj),
        in_specs=[pl.BlockSpec((FFN_HALO, d), lambda i, j: (jnp.maximum(i * hb - 1, 0), 0)),
                  pl.BlockSpec((tm, d), lambda i, j: (i, 0)),
                  pl.BlockSpec((FFN_HALO, d), lambda i, j: (jnp.minimum((i + 1) * hb, nhb - 1), 0)),
                  pl.BlockSpec((tm, d), lambda i, j: (i, 0)),
                  pl.BlockSpec((1, N_MOD, d), lambda i, j: (mod_row(i), 0, 0)),
                  pl.BlockSpec((None, d, tn), lambda i, j: (layer, 0, j)),
                  pl.BlockSpec((None, d, tn), lambda i, j: (layer, 0, nj + j)),
                  pl.BlockSpec((CONV_W, tn), lambda i, j: (0, j)),
                  pl.BlockSpec((CONV_W, tn), lambda i, j: (0, nj + j)),
                  pl.BlockSpec((1, tn), lambda i, j: (0, j)),
                  pl.BlockSpec((1, tn), lambda i, j: (0, nj + j)),
                  pl.BlockSpec((None, tn, d), lambda i, j: (layer, j, 0))],
        out_specs=pl.BlockSpec((tm, d), lambda i, j: (i, 0)),
        out_shape=jax.ShapeDtypeStruct((t, d), F32),
        scratch_shapes=[pltpu.VMEM((tm + 2 * FFN_HALO, d), BF16),
                        pltpu.VMEM((tm + 2 * FFN_HALO, tn), F32), pltpu.VMEM((tm + 2 * FFN_HALO, tn), F32)],
        compiler_params=_cp(("parallel", "arbitrary")),
        name="conv_ffn",
    )(h2, h2, h2, x2, mod, w_up, w_up, conv_w, conv_w, conv_b, conv_b, w_down)


def _block_ones(group):
    idx = np.arange(GW) // group
    return jnp.asarray(idx[:, None] == idx[None, :], dtype=BF16)


def _pad_heads(w, heads, width):
    lead = w.shape[:-1]
    w = w.reshape(lead + (heads, width))
    w = jnp.pad(w, [(0, 0)] * len(lead) + [(0, 0), (0, LANES - width)])
    return w.reshape(lead + (heads * LANES,))


def _layer_weights(P, l):
    d = P["w_in"].shape[1]
    w_in = P["w_in"][l].astype(BF16)
    ml0, ml1 = 3 * GW, 3 * GW + MLA_Q_LORA + MLA_KV_LORA + MLA_ROPE
    w_all = jnp.concatenate([w_in[:, :ml0], w_in[:, ml1:], w_in[:, ml0:ml1],
                             jnp.zeros((d, ML_W - (ml1 - ml0)), BF16)], axis=1)
    wukv = P["mla_w_ukv"][l].reshape(MLA_KV_LORA, MLA_H, MLA_NOPE + MLA_V)
    wuk = _pad_heads(wukv[:, :, :MLA_NOPE].reshape(MLA_KV_LORA, MLA_H * MLA_NOPE), MLA_H, MLA_NOPE)
    wuv = wukv[:, :, MLA_NOPE:].reshape(MLA_KV_LORA, MLA_H * MLA_V)
    tile = lambda g, reps: jnp.tile(g.astype(F32), reps)
    padw = lambda g: jnp.pad(g.astype(F32), (0, GW - g.shape[0]))
    gains = jnp.stack([
        tile(P["diff_qn_g"][l], 8) * (DIFF_DH ** -0.5 * LOG2E),
        tile(P["diff_kn_g"][l], 8),
        tile(P["na_qn_g"][l], 8) * (NA_DH ** -0.5 * LOG2E),
        tile(P["na_kn_g"][l], 8),
        _pad_heads(tile(P["mla_qn_g"][l], MLA_H), MLA_H, MLA_QK) * (MLA_QK ** -0.5 * LOG2E),
        _pad_heads(tile(P["mla_kn_g"][l], MLA_H), MLA_H, MLA_QK),
        padw(P["mla_qa_g"][l]),
        padw(P["mla_kva_g"][l]),
    ])
    return dict(
        w_in=w_all,
        wuq=_pad_heads(P["mla_w_uq"][l], MLA_H, MLA_QK).astype(BF16),
        wuk=wuk.astype(BF16), wuv=wuv.astype(BF16),
        gains=gains, g64=_block_ones(64), g128=_block_ones(LANES),
        norm1=P["norm1_g"][l].reshape(1, d).astype(F32),
        norm2=P["norm2_g"][l].reshape(1, d).astype(F32),
        lam=P["diff_lam"][l].astype(F32),
        subg=P["diff_sub_g"][l].reshape(1, 2 * DIFF_DH).astype(F32),
        pool_w=P["pool_w"][l].astype(BF16),
        pool_scale=P["pool_scale"][l].reshape(1, GW).astype(F32),
        w_out=P["w_out"][l].astype(BF16),
        conv_w=P["conv_w"][l].astype(F32),
        conv_b=P["conv_b"][l].reshape(1, -1).astype(F32),
    )


def _rope_tables(n):
    t = jnp.arange(n)
    rows = (t // GRID_W).astype(F32)
    cols = (t % GRID_W).astype(F32)

    def ang(dim):
        quarter = dim // 4
        inv = ROPE_BASE ** (-jnp.arange(quarter, dtype=F32) / quarter)
        return jnp.concatenate([rows[:, None] * inv, cols[:, None] * inv], axis=-1)

    a = ang(DIFF_DH)
    cos_a, sin_a = jnp.cos(a), jnp.sin(a)
    cd = jnp.tile(cos_a, (1, 4))
    sd = jnp.tile(jnp.concatenate([-sin_a, sin_a], axis=1), (1, 2))
    b = ang(MLA_ROPE)
    cos_b, sin_b = jnp.cos(b), jnp.sin(b)
    ones = jnp.ones((n, MLA_NOPE), F32)
    pad = LANES - MLA_NOPE - MLA_ROPE
    cm = jnp.concatenate([ones, cos_b, cos_b, jnp.ones((n, pad), F32)], axis=1)
    sm = jnp.concatenate([0.0 * ones, -sin_b, sin_b, jnp.zeros((n, pad), F32)], axis=1)
    return jnp.stack([cd, sd, cm, sm])


def _layer(x, mod, mod_row_of_batch, W, layer, cache, rope_tab, bias_tab):
    b, n, d = x.shape
    t = b * n
    x2 = x.reshape(t, d)
    lam_init = 0.8 - 0.6 * math.exp(-0.3 * layer)
    tm = min(512, n)
    row_in = lambda i: mod_row_of_batch(i // (n // tm))
    outs = _inproj(x2, mod, row_in, W["norm1"], W, rope_tab, n, states=cache is None, tm=tm)
    qd, kd, vd, qn, kn, vn, pu, qm, km, vm = [a.reshape(b, n, GW) for a in outs[:10]]
    if cache is None:
        oa = _attention("diff", qd, [(kd, vd)], n, n, W["lam"], W["subg"], lam_init)
        ob = _attention("full", qm, [(km, vm)], n, n)
        oc = _attention("pair", qn, [(kn, vn)], n, n)
        state = outs[10:]
    else:
        a_k, a_v, b_ckv, b_kpe, c_k, c_v = cache
        pc = a_k.shape[1]
        ck = a_k.reshape(b, pc, GW).astype(BF16)
        cv = a_v.reshape(b, pc, GW).astype(BF16)
        oa = _attention_latent("diff", qd, ck, cv, kd, vd, W["gains"][0:1], DIFF_DH, W["lam"], W["subg"],
                               lam_init)
        kpe_pad = jnp.pad(b_kpe.reshape(b * pc, MLA_ROPE), ((0, 0), (0, LANES - MLA_ROPE)))
        kmc, vmc = _mla_cache(b_ckv.reshape(b * pc, MLA_KV_LORA), kpe_pad, W)
        ob = _attention_latent("full", qm, kmc.reshape(b, pc, GW), vmc.reshape(b, pc, GW), km, vm,
                               W["gains"][4:5], MLA_QK)
        nk = c_k.reshape(b, pc, GW).astype(BF16)
        nv = c_v.reshape(b, pc, GW).astype(BF16)
        oc = _na_latent(qn, kn, vn, nk, nv, bias_tab)
        state = ()
    od = _pool(pu, W["pool_w"], W["pool_scale"])
    flat = lambda a: a.reshape(t, GW)
    x2, h2 = _outproj(flat(oa), flat(ob), flat(oc), flat(od), x2, mod, row_in, W["norm2"], W["w_out"], tm=tm)
    tmf = min(512, t) if cache is None else min(512, n)
    row_ffn = lambda i: mod_row_of_batch((i * tmf) // n)
    x2 = _ffn(x2, h2, mod, row_ffn, W["w_up"], W["conv_w"], W["conv_b"], W["w_down"], layer, n, tm=tmf)
    return x2.reshape(b, n, d), state


def kernel(x_prompt, x_sample, cache_diff_k, cache_diff_v, cache_mla_ckv, cache_mla_kpe, cache_na_k, cache_na_v, c, c_ctx, norm1_g, norm2_g, ada_w, ada_b, w_in, diff_qn_g, diff_kn_g, diff_lam, diff_sub_g, mla_qa_g, mla_kva_g, mla_w_uq, mla_w_ukv, mla_qn_g, mla_kn_g, na_qn_g, na_kn_g, na_bias, pool_w, pool_scale, w_out, w_up, conv_w, conv_b, w_down):
    P = dict(norm1_g=norm1_g, norm2_g=norm2_g, w_in=w_in, diff_qn_g=diff_qn_g, diff_kn_g=diff_kn_g,
             diff_lam=diff_lam, diff_sub_g=diff_sub_g, mla_qa_g=mla_qa_g, mla_kva_g=mla_kva_g,
             mla_w_uq=mla_w_uq, mla_w_ukv=mla_w_ukv, mla_qn_g=mla_qn_g, mla_kn_g=mla_kn_g,
             na_qn_g=na_qn_g, na_kn_g=na_kn_g, pool_w=pool_w, pool_scale=pool_scale, w_out=w_out,
             w_up=w_up, conv_w=conv_w, conv_b=conv_b, w_down=w_down)
    depth = w_in.shape[0]
    d = x_prompt.shape[-1]
    bd, nd = x_sample.shape[:2]
    n_rows = -(-(bd + 1) // 8) * 8
    cv = jnp.concatenate([c, c_ctx[None, :], jnp.zeros((n_rows - bd - 1, d), F32)], axis=0)
    mod_all = _ada(cv, ada_w, ada_b).reshape(depth, n_rows, N_MOD, d)
    rope_tab = _rope_tables(nd)
    Ws = [_layer_weights(P, l) for l in range(depth)]
    w_up_all, w_down_all = w_up.astype(BF16), w_down.astype(BF16)
    for W in Ws:
        W.update(w_up=w_up_all, w_down=w_down_all)

    xp = x_prompt
    states = []
    for l in range(depth):
        xp, st = _layer(xp, mod_all[l], lambda bi: bd, Ws[l], l, None, None, None)
        states.append(st)
    xs = x_sample
    for l in range(depth):
        cache_l = (cache_diff_k[:, l], cache_diff_v[:, l], cache_mla_ckv[:, l], cache_mla_kpe[:, l],
                   cache_na_k[:, l], cache_na_v[:, l])
        bias_tab = _na_bias_table(na_bias[l], nd // GRID_W)
        xs, _ = _layer(xs, mod_all[l], lambda bi: bi, Ws[l], l, cache_l, rope_tab, bias_tab)

    bp, npr = x_prompt.shape[:2]

    def stack(k, shape):
        return jnp.stack([s[k].reshape((bp, npr) + shape) for s in states], axis=1)

    return (xp, xs,
            stack(0, (DIFF_H, 2, DIFF_DH)), stack(1, (DIFF_H, 2 * DIFF_DH)),
            stack(2, (MLA_KV_LORA,)), stack(3, (MLA_ROPE,)),
            stack(4, (NA_H, NA_DH)), stack(5, (NA_H, NA_DH)))
```

```python
import functools
import math

import numpy as np
import jax
import jax.numpy as jnp
from jax import lax
from jax.experimental import pallas as pl
from jax.experimental.pallas import tpu as pltpu

F32 = jnp.float32
BF16 = jnp.bfloat16

GRID_W = 64
ROPE_BASE = 10000.0
EPS = 1e-6
N_MOD = 6
DIFF_H = 4
DIFF_DH = 64
MLA_H = 4
MLA_NOPE = 64
MLA_ROPE = 32
MLA_V = 128
MLA_Q_LORA = 384
MLA_KV_LORA = 128
NA_H = 8
NA_DH = 64
NA_ROWS = 8
NA_COLS = 16
POOL_WINDOWS = (2, 4, 8, 16)
POOL_G = 4
POOL_C = 128
CONV_W = 3

GW = 512
LANES = 128
NSLOT = GW // LANES
MXU_DIM = 256
MLA_QK = MLA_NOPE + MLA_ROPE
V7X_VMEM_LIMIT = 56 * 1024 * 1024
NEG = -1e30
LOG2E = math.log2(math.e)

C_DQ, C_DK, C_DV, C_NQ, C_NK, C_NV, C_PU, C_ML = 0, 512, 1024, 1536, 2048, 2560, 3072, 3584
ML_W = 640
IN_WP = C_ML + ML_W


def _cp(sem, vmem=V7X_VMEM_LIMIT):
    return pltpu.CompilerParams(dimension_semantics=sem, vmem_limit_bytes=vmem)


def _const_spec(shape):
    nd = len(shape)
    return pl.BlockSpec(shape, lambda *_: (0,) * nd, pipeline_mode=pl.Buffered(1))


def _ada_kernel(c_ref, w_ref, b_ref, o_ref):
    c = c_ref[...]
    s = c * jax.nn.sigmoid(c)
    o_ref[0] = jnp.dot(s.astype(BF16), w_ref[0].astype(BF16), preferred_element_type=F32) + b_ref[0]


def _ada(cv, ada_w, ada_b, tn=1024):
    depth, d, n = ada_w.shape
    r = cv.shape[0]
    return pl.pallas_call(
        _ada_kernel,
        grid=(depth, n // tn),
        in_specs=[pl.BlockSpec((r, d), lambda l, j: (0, 0)),
                  pl.BlockSpec((1, d, tn), lambda l, j: (l, 0, j)),
                  pl.BlockSpec((1, 1, tn), lambda l, j: (l, 0, j))],
        out_specs=pl.BlockSpec((1, r, tn), lambda l, j: (l, 0, j)),
        out_shape=jax.ShapeDtypeStruct((depth, r, n), F32),
        compiler_params=_cp(("parallel", "parallel")),
        name="adaln",
    )(cv, ada_w, ada_b.reshape(depth, 1, n))


def _modulated(x, g, mod, k):
    ms = jnp.mean(x * x, axis=-1, keepdims=True)
    y = x * lax.rsqrt(ms + EPS) * g
    return y * (1.0 + mod[k + 1:k + 2, :]) + mod[k:k + 1, :]


def _group_rms(z, gsum_ref, inv_n, gain):
    sq = (z * z).astype(BF16)
    ss = jnp.concatenate(
        [jnp.dot(sq[:, c:c + MXU_DIM], gsum_ref[c:c + MXU_DIM, c:c + MXU_DIM], preferred_element_type=F32)
         for c in range(0, z.shape[1], MXU_DIM)], axis=1)
    return z * lax.rsqrt(ss * inv_n + EPS) * gain


def _rope_chunks(z, cos_ref, sin_ref, lo_pred, shift_lo, shift_hi):
    c = cos_ref[...]
    s = sin_ref[...]
    outs = []
    for j in range(z.shape[1] // LANES):
        zc = z[:, j * LANES:(j + 1) * LANES]
        partner = jnp.where(lo_pred, pltpu.roll(zc, shift_lo, 1), pltpu.roll(zc, shift_hi, 1))
        outs.append(zc * c + partner * s)
    return jnp.concatenate(outs, axis=1)


def _mla_kv(ckv_b, kp, wuk_ref, wuv_ref, g128_ref, kgain):
    kp_slot = pltpu.roll(kp, MLA_NOPE, 1)
    kh = jnp.dot(ckv_b, wuk_ref[...], preferred_element_type=F32) + jnp.concatenate([kp_slot] * MLA_H, axis=1)
    k = _group_rms(kh, g128_ref, 1.0 / MLA_QK, kgain)
    v = jnp.dot(ckv_b, wuv_ref[...], preferred_element_type=F32)
    return k, v


def _inproj_kernel(*refs, rope, states):
    it = iter(refs)
    x_ref, mod_ref, g_ref, w_ref, wuq_ref, wuk_ref, wuv_ref = (next(it) for _ in range(7))
    gains_ref, g64_ref, g128_ref = (next(it) for _ in range(3))
    if rope:
        rope_ref = next(it)
    qd_ref, kd_ref, vd_ref, qn_ref, kn_ref, vn_ref, pu_ref, qm_ref, km_ref, vm_ref = (
        next(it) for _ in range(10))
    if states:
        ska_ref, sva_ref, sckv_ref, skpe_ref, skc_ref, svc_ref = (next(it) for _ in range(6))

    h = _modulated(x_ref[...], g_ref[...], mod_ref[0], 0).astype(BF16)
    gains = gains_ref[...]
    lane = lax.broadcasted_iota(jnp.int32, (1, LANES), 1)

    def seg(c0, w=GW):
        return jnp.dot(h, w_ref[:, c0:c0 + w], preferred_element_type=F32)

    def rope_d(z):
        if not rope:
            return z
        return _rope_chunks(z, rope_ref.at[0], rope_ref.at[1], (lane % 64) < 32, LANES - 32, 32)

    def rope_m(z):
        if not rope:
            return z
        return _rope_chunks(z, rope_ref.at[2], rope_ref.at[3], lane < MLA_NOPE + MLA_ROPE // 2,
                            LANES - MLA_ROPE // 2, MLA_ROPE // 2)

    qd_ref[...] = rope_d(_group_rms(seg(C_DQ), g64_ref, 1.0 / DIFF_DH, gains[0:1])).astype(BF16)
    kd = _group_rms(seg(C_DK), g64_ref, 1.0 / DIFF_DH, gains[1:2])
    kd_ref[...] = rope_d(kd).astype(BF16)
    vd = seg(C_DV)
    vd_ref[...] = vd.astype(BF16)
    qn_ref[...] = _group_rms(seg(C_NQ), g64_ref, 1.0 / NA_DH, gains[2:3]).astype(BF16)
    kn = _group_rms(seg(C_NK), g64_ref, 1.0 / NA_DH, gains[3:4])
    kn_ref[...] = kn.astype(BF16)
    vn = seg(C_NV)
    vn_ref[...] = vn.astype(BF16)
    pu_ref[...] = seg(C_PU)
    zm = seg(C_ML, ML_W)
    cq = zm[:, :MLA_Q_LORA]
    cq = cq * lax.rsqrt(jnp.mean(cq * cq, axis=-1, keepdims=True) + EPS) * gains[6:7, :MLA_Q_LORA]
    qh = jnp.dot(cq.astype(BF16), wuq_ref[...], preferred_element_type=F32)
    qm_ref[...] = rope_m(_group_rms(qh, g128_ref, 1.0 / MLA_QK, gains[4:5])).astype(BF16)
    ck = zm[:, MLA_Q_LORA:MLA_Q_LORA + MLA_KV_LORA]
    ckv = ck * lax.rsqrt(jnp.mean(ck * ck, axis=-1, keepdims=True) + EPS) * gains[7:8, :MLA_KV_LORA]
    kp = zm[:, GW:GW + LANES]
    km, vm = _mla_kv(ckv.astype(BF16), kp, wuk_ref, wuv_ref, g128_ref, gains[5:6])
    km_ref[...] = rope_m(km).astype(BF16)
    vm_ref[...] = vm.astype(BF16)
    if states:
        ska_ref[...] = kd
        sva_ref[...] = vd
        sckv_ref[...] = ckv
        skpe_ref[...] = kp[:, :MLA_ROPE]
        skc_ref[...] = kn
        svc_ref[...] = vn


def _inproj(x2, mod, mod_row, g1, W, rope_tab, seq, states, tm=256):
    t, d = x2.shape
    rope = rope_tab is not None
    tiles_per_seq = seq // tm
    tok = lambda w: pl.BlockSpec((tm, w), lambda i: (i, 0))
    in_specs = [tok(d),
                pl.BlockSpec((1, N_MOD, d), lambda i: (mod_row(i), 0, 0)),
                _const_spec((1, d)),
                _const_spec(W["w_in"].shape), _const_spec(W["wuq"].shape),
                _const_spec(W["wuk"].shape), _const_spec(W["wuv"].shape),
                _const_spec(W["gains"].shape), _const_spec(W["g64"].shape),
                _const_spec(W["g128"].shape)]
    args = [x2, mod, g1, W["w_in"], W["wuq"], W["wuk"], W["wuv"], W["gains"], W["g64"], W["g128"]]
    if rope:
        in_specs.append(pl.BlockSpec((4, tm, LANES), lambda i: (0, i % tiles_per_seq, 0)))
        args.append(rope_tab)
    out_specs = [tok(GW)] * 10
    out_shape = [jax.ShapeDtypeStruct((t, GW), BF16)] * 6 + [jax.ShapeDtypeStruct((t, GW), F32)] + \
                [jax.ShapeDtypeStruct((t, GW), BF16)] * 3
    if states:
        widths = (GW, GW, MLA_KV_LORA, MLA_ROPE, GW, GW)
        out_specs += [tok(w) for w in widths]
        out_shape += [jax.ShapeDtypeStruct((t, w), F32) for w in widths]
    return pl.pallas_call(
        functools.partial(_inproj_kernel, rope=rope, states=states),
        grid=(t // tm,),
        in_specs=in_specs, out_specs=out_specs, out_shape=out_shape,
        compiler_params=_cp(("parallel",)),
        name="inproj",
    )(*args)


def _mla_cache_kernel(ckv_ref, kp_ref, wuk_ref, wuv_ref, g128_ref, gains_ref, k_ref, v_ref):
    k, v = _mla_kv(ckv_ref[...].astype(BF16), kp_ref[...], wuk_ref, wuv_ref, g128_ref, gains_ref[5:6, :])
    k_ref[...] = k.astype(BF16)
    v_ref[...] = v.astype(BF16)


def _mla_cache(ckv2, kp2, W, tm=256):
    t = ckv2.shape[0]
    tok = lambda w: pl.BlockSpec((tm, w), lambda i: (i, 0))
    return pl.pallas_call(
        _mla_cache_kernel,
        grid=(t // tm,),
        in_specs=[tok(MLA_KV_LORA), tok(LANES), _const_spec(W["wuk"].shape), _const_spec(W["wuv"].shape),
                  _const_spec(W["g128"].shape), _const_spec(W["gains"].shape)],
        out_specs=[tok(GW), tok(GW)],
        out_shape=[jax.ShapeDtypeStruct((t, GW), BF16)] * 2,
        compiler_params=_cp(("parallel",)),
        name="mla_cache_kv",
    )(ckv2, kp2, W["wuk"], W["wuv"], W["g128"], W["gains"])


def _diff_lambda(lam_ref, lam_init):
    lf = lam_ref[...]
    return (jnp.exp(jnp.sum(lf[0:1] * lf[1:2], axis=-1, keepdims=True))
            - jnp.exp(jnp.sum(lf[2:3] * lf[3:4], axis=-1, keepdims=True)) + lam_init)


def _stream_queries(q, mode):
    if mode == "full":
        return [q]
    lane = lax.broadcasted_iota(jnp.int32, (1, LANES), 1)
    zero = jnp.zeros_like(q)
    return [jnp.where(lane < 64, q, zero), jnp.where(lane < 64, zero, q)]


def _combine_streams(outs, mode, lam, subg_ref, lam_init):
    if mode == "full":
        return outs[0]
    if mode == "pair":
        lane = lax.broadcasted_iota(jnp.int32, (1, LANES), 1)
        return jnp.where(lane < 64, outs[0], outs[1])
    o = outs[0] - lam * outs[1]
    o = o * lax.rsqrt(jnp.mean(o * o, axis=-1, keepdims=True) + EPS) * subg_ref[...]
    return o * (1.0 - lam_init)


def _online_attend(q_ref, srcs, chunks, mode, lam, subg_ref, lam_init, o_ref):
    tq = q_ref.shape[1]
    nstream = 1 if mode == "full" else 2
    for slot in range(NSLOT):
        sl = slice(slot * LANES, (slot + 1) * LANES)
        qs = _stream_queries(q_ref[0, :, sl], mode)

        def step(k_c, v_cs, state):
            new = []
            for si in range(nstream):
                m, l, acc = state[3 * si:3 * si + 3]
                s = lax.dot_general(qs[si], k_c, (((1,), (1,)), ((), ())), preferred_element_type=F32)
                m_new = jnp.maximum(m, jnp.max(s, axis=-1, keepdims=True))
                alpha = jnp.exp2(m - m_new)
                p = jnp.exp2(s - m_new)
                l = alpha * l + jnp.sum(p, axis=-1, keepdims=True)
                acc = alpha * acc + jnp.dot(p.astype(BF16), v_cs[si % len(v_cs)],
                                            preferred_element_type=F32)
                new += [m_new, l, acc]
            return tuple(new)

        state = ()
        for _ in range(nstream):
            state += (jnp.full((tq, 1), NEG, F32), jnp.zeros((tq, 1), F32), jnp.zeros((tq, LANES), F32))
        for (k_ref, v_refs), (n_chunk, tk) in zip(srcs, chunks):
            if n_chunk == 1:
                state = step(k_ref[0, :, sl], [v[0, :, sl] for v in v_refs], state)
            else:
                def body(c, st, k_ref=k_ref, v_refs=v_refs, tk=tk):
                    rows = pl.ds(pl.multiple_of(c * tk, tk), tk)
                    return step(k_ref[0, rows, sl], [v[0, rows, sl] for v in v_refs], st)
                state = lax.fori_loop(0, n_chunk, body, state)

        outs = [state[3 * si + 2] / state[3 * si + 1] for si in range(nstream)]
        o_ref[0, :, sl] = _combine_streams(outs, mode, lam, subg_ref, lam_init).astype(o_ref.dtype)


def _attn_kernel(*refs, mode, chunks, lam_init):
    it = iter(refs)
    q_ref = next(it)
    srcs = []
    for _ in chunks:
        k_ref = next(it)
        srcs.append((k_ref, (next(it),)))
    lam, subg_ref = None, None
    if mode == "diff":
        lam = _diff_lambda(next(it), lam_init)
        subg_ref = next(it)
    o_ref = next(it)
    _online_attend(q_ref, srcs, chunks, mode, lam, subg_ref, lam_init, o_ref)


AUG = 2 * LANES
SHIFT_MARGIN = 1.0 + 2.0 ** -6
SHIFT_LIMIT = 48.0
ATTN_ROWS = {"diff": 512, "full": 1024}
VT_ROWS = LANES + 16


def _attn_fast_kernel(*refs, mode, chunks, lam_init, head_dim, sub):
    it = iter(refs)
    q_ref, kc_ref, vc_ref, k_ref, v_ref, qgain_ref = (next(it) for _ in range(6))
    lam, subg_ref = None, None
    if mode == "diff":
        lam = _diff_lambda(next(it), lam_init)
        subg_ref = next(it)
    o_ref, kaug_ref, vext_ref, shift_ref = (next(it) for _ in range(4))

    tq = q_ref.shape[1]
    pc, n = kc_ref.shape[1], k_ref.shape[1]
    kt = pc + n
    nstream = 1 if mode == "full" else 2
    lane = lax.broadcasted_iota(jnp.int32, (1, LANES), 1)

    @pl.when(pl.program_id(1) == 0)
    def _():
        q_bound = math.sqrt(head_dim) * jnp.max(jnp.abs(qgain_ref[...]), axis=-1, keepdims=True)
        shift_ref[...] = jnp.zeros_like(shift_ref)
        for slot in range(NSLOT):
            sl = slice(slot * LANES, (slot + 1) * LANES)
            a0 = slot * AUG
            kaug_ref[0:pc, a0:a0 + LANES] = kc_ref[0, :, sl]
            kaug_ref[pc:kt, a0:a0 + LANES] = k_ref[0, :, sl]
            r0 = slot * VT_ROWS
            for c0 in range(0, kt, MXU_DIM):
                chunk = (vc_ref[0, c0:c0 + MXU_DIM, sl] if c0 < pc
                         else v_ref[0, c0 - pc:c0 - pc + MXU_DIM, sl])
                vext_ref[r0:r0 + LANES, c0:c0 + MXU_DIM] = chunk.astype(F32).T.astype(BF16)
            sub = lax.broadcasted_iota(jnp.int32, (VT_ROWS - LANES, 1), 0)
            vext_ref[r0 + LANES:r0 + VT_ROWS, :] = jnp.broadcast_to(
                jnp.where(sub == 0, 1.0, 0.0).astype(BF16), (VT_ROWS - LANES, kt))
            shift_lanes = jnp.zeros((1, LANES), F32)
            for si in range(nstream):
                best = None
                for src in (kc_ref, k_ref):
                    kk = _stream_queries(src[0, :, sl], mode)[si].astype(F32)
                    nrm = jnp.max(jnp.sum(kk * kk, axis=-1, keepdims=True), axis=0, keepdims=True)
                    best = nrm if best is None else jnp.maximum(best, nrm)
                shift = q_bound * jnp.sqrt(best) * SHIFT_MARGIN
                row = slot * nstream + si
                shift_ref[row:row + 1, :] = jnp.broadcast_to(shift, (1, LANES))
                shift_lanes = jnp.where(lane == si, -shift, shift_lanes)
            kaug_ref[:, a0 + LANES:a0 + AUG] = jnp.broadcast_to(shift_lanes.astype(BF16), (kt, LANES))

    safe = jnp.max(shift_ref[...]) <= SHIFT_LIMIT

    @pl.when(safe)
    def _():
        def sub_tile(t, carry):
            rows = pl.ds(pl.multiple_of(t * sub, sub), sub)
            for slot in range(NSLOT):
                sl = slice(slot * LANES, (slot + 1) * LANES)
                a0 = slot * AUG
                outs = []
                for si, qm in enumerate(_stream_queries(q_ref[0, rows, sl], mode)):
                    pick = jnp.broadcast_to(jnp.where(lane == si, 1.0, 0.0).astype(BF16), (sub, LANES))
                    qa = jnp.concatenate([qm, pick], axis=1)
                    s_t = lax.dot_general(kaug_ref[:, a0:a0 + AUG], qa, (((1,), (1,)), ((), ())),
                                          preferred_element_type=F32)
                    acc_t = jnp.dot(vext_ref[slot * VT_ROWS:(slot + 1) * VT_ROWS, :],
                                    jnp.exp2(s_t).astype(BF16), preferred_element_type=F32)
                    outs.append((acc_t[0:LANES] / acc_t[LANES:LANES + 1]).T)
                o = _combine_streams(outs, mode, lam, subg_ref, lam_init)
                o_ref[0, rows, sl] = o.astype(o_ref.dtype)
            return carry
        lax.fori_loop(0, tq // sub, sub_tile, 0)

    @pl.when(jnp.logical_not(safe))
    def _():
        _online_attend(q_ref, [(kc_ref, (vc_ref,)), (k_ref, (v_ref,))], chunks, mode, lam, subg_ref,
                       lam_init, o_ref)


def _attention_latent(mode, q, kc, vc, k, v, qgain, head_dim, lam=None, subg=None, lam_init=0.0,
                      tk=512):
    b, n, _ = q.shape
    tq = min(ATTN_ROWS[mode], n)
    pc = kc.shape[1]
    kt = pc + n
    res = lambda m: pl.BlockSpec((1, m, GW), lambda bi, qi: (bi, 0, 0))
    in_specs = [pl.BlockSpec((1, tq, GW), lambda bi, qi: (bi, qi, 0)), res(pc), res(pc), res(n), res(n),
                _const_spec(qgain.shape)]
    args = [q, kc, vc, k, v, qgain]
    if mode == "diff":
        in_specs += [_const_spec(lam.shape), _const_spec(subg.shape)]
        args += [lam, subg]
    chunks = ((1, pc), (n // tk, tk))
    return pl.pallas_call(
        functools.partial(_attn_fast_kernel, mode=mode, chunks=chunks, lam_init=lam_init, head_dim=head_dim,
                          sub=tq),
        grid=(b, n // tq),
        in_specs=in_specs,
        out_specs=pl.BlockSpec((1, tq, GW), lambda bi, qi: (bi, qi, 0)),
        out_shape=jax.ShapeDtypeStruct((b, n, GW), BF16),
        scratch_shapes=[pltpu.VMEM((kt, NSLOT * AUG), BF16), pltpu.VMEM((NSLOT * VT_ROWS, kt), BF16),
                        pltpu.VMEM((8, LANES), F32)],
        compiler_params=_cp(("parallel", "arbitrary")),
        name="attn_latent_" + mode,
    )(*args)


def _attention(mode, q, srcs, tq, tk, lam=None, subg=None, lam_init=0.0):
    b, n, _ = q.shape
    in_specs = [pl.BlockSpec((1, tq, GW), lambda bi, qi: (bi, qi, 0))]
    args = [q]
    chunks = []
    for src in srcs:
        ks = src[0].shape[1]
        step = min(tk, ks)
        chunks.append((ks // step, step))
        for a in src:
            in_specs.append(pl.BlockSpec((1, ks, GW), lambda bi, qi: (bi, 0, 0)))
            args.append(a)
    if mode == "diff":
        in_specs += [_const_spec(lam.shape), _const_spec(subg.shape)]
        args += [lam, subg]
    return pl.pallas_call(
        functools.partial(_attn_kernel, mode=mode, chunks=tuple(chunks), lam_init=lam_init),
        grid=(b, n // tq),
        in_specs=in_specs,
        out_specs=pl.BlockSpec((1, tq, GW), lambda bi, qi: (bi, qi, 0)),
        out_shape=jax.ShapeDtypeStruct((b, n, GW), BF16),
        compiler_params=_cp(("parallel", "parallel")),
        name="attn_" + mode,
    )(*args)


NA_QROWS = 4
NA_KROWS = NA_QROWS + NA_ROWS


def _na_key_start(r0, rows):
    return jnp.clip(r0 - NA_ROWS // 2, 0, rows - NA_KROWS)


def _na_kernel(q_ref, k_ref, v_ref, kc_ref, vc_ref, bias_ref, o_ref, *, rows):
    r0 = pl.program_id(1) * NA_QROWS
    ks = _na_key_start(r0, rows)
    loc = pl.ds(pl.multiple_of(ks * GRID_W, GRID_W), NA_KROWS * GRID_W)
    lane = lax.broadcasted_iota(jnp.int32, (1, LANES), 1)
    nt = (((1,), (1,)), ((), ()))
    for slot in range(NSLOT):
        sl = slice(slot * LANES, (slot + 1) * LANES)
        q = q_ref[0, :, sl]
        k_loc = k_ref[0, loc, sl]
        k_ctx = kc_ref[0, :, sl]
        v_loc = v_ref[0, loc, sl]
        v_ctx = vc_ref[0, :, sl]
        v_loc = jnp.concatenate([v_loc, jnp.ones_like(v_loc)], axis=1)
        v_ctx = jnp.concatenate([v_ctx, jnp.ones_like(v_ctx)], axis=1)
        halves = []
        for half, qh in enumerate(_stream_queries(q, "pair")):
            s_loc = lax.dot_general(qh, k_loc, nt, preferred_element_type=F32) + bias_ref[0, 2 * slot + half]
            s_ctx = lax.dot_general(qh, k_ctx, nt, preferred_element_type=F32)
            m = jnp.maximum(jnp.max(s_loc, axis=-1, keepdims=True), jnp.max(s_ctx, axis=-1, keepdims=True))
            acc = (jnp.dot(jnp.exp2(s_loc - m).astype(BF16), v_loc, preferred_element_type=F32)
                   + jnp.dot(jnp.exp2(s_ctx - m).astype(BF16), v_ctx, preferred_element_type=F32))
            halves.append(acc[:, 0:LANES] / acc[:, LANES:2 * LANES])
        o_ref[0, :, sl] = jnp.where(lane < 64, halves[0], halves[1]).astype(o_ref.dtype)


def _na_bias_table(bias_tab, rows):
    assert rows >= NA_KROWS + NA_QROWS and rows % NA_QROWS == 0
    qc = np.arange(GRID_W)[:, None]
    kc = np.arange(GRID_W)[None, :]
    cstart = np.clip(qc - NA_COLS // 2, 0, GRID_W - NA_COLS)
    in_win = (kc >= cstart) & (kc < cstart + NA_COLS)
    dc_idx = np.clip(kc - qc, 1 - NA_COLS, NA_COLS - 1) + NA_COLS - 1
    col_pick = (dc_idx[:, :, None] == np.arange(2 * NA_COLS - 1)).astype(np.float32)
    variants = (0, NA_ROWS // 2, NA_ROWS)
    row_pick = np.zeros((len(variants), NA_QROWS, NA_KROWS, 2 * NA_ROWS - 1), np.float32)
    for vi, var in enumerate(variants):
        for j in range(NA_QROWS):
            first = min(max(j + var - NA_ROWS // 2, 0), NA_KROWS - NA_ROWS)
            for i in range(first, first + NA_ROWS):
                row_pick[vi, j, i, i - j - var + NA_ROWS - 1] = 1.0
    valid = (row_pick.sum(-1) > 0)[:, None, :, None, :, None] & in_win[None, None, None, :, None, :]
    toep = jnp.einsum("qkd,hrd->hrqk", jnp.asarray(col_pick), bias_tab.astype(F32),
                      precision=lax.Precision.HIGHEST)
    tab = jnp.einsum("vjir,hrqk->vhjqik", jnp.asarray(row_pick), toep, precision=lax.Precision.HIGHEST)
    tab = jnp.where(jnp.asarray(valid), tab * LOG2E, NEG)
    return tab.reshape(len(variants), NA_H, NA_QROWS * GRID_W, NA_KROWS * GRID_W)


def _na_latent(q, k, v, kc, vc, bias):
    b, n, _ = q.shape
    rows = n // GRID_W
    pc = kc.shape[1]
    full = lambda m: pl.BlockSpec((1, m, GW), lambda bi, r: (bi, 0, 0))
    tq = NA_QROWS * GRID_W
    return pl.pallas_call(
        functools.partial(_na_kernel, rows=rows),
        grid=(b, rows // NA_QROWS),
        in_specs=[pl.BlockSpec((1, tq, GW), lambda bi, r: (bi, r, 0)),
                  full(n), full(n), full(pc), full(pc),
                  pl.BlockSpec((1, NA_H, tq, NA_KROWS * GRID_W),
                               lambda bi, r: ((r * NA_QROWS - _na_key_start(r * NA_QROWS, rows))
                                              // (NA_ROWS // 2), 0, 0, 0))],
        out_specs=pl.BlockSpec((1, tq, GW), lambda bi, r: (bi, r, 0)),
        out_shape=jax.ShapeDtypeStruct((b, n, GW), BF16),
        compiler_params=_cp(("parallel", "parallel")),
        name="attn_neighbourhood",
    )(q, k, v, kc, vc, bias)


POOL_HALO = 64


def _pool_kernel(prev_ref, main_ref, next_ref, band_ref, inv_ref, w_ref, scale_ref, o_ref):
    ext = jnp.concatenate([prev_ref[0], main_ref[0], next_ref[0]], axis=0)
    ext_hi = ext.astype(BF16)
    ext_lo = (ext - ext_hi.astype(F32)).astype(BF16)
    for g in range(POOL_G):
        sl = slice(g * POOL_C, (g + 1) * POOL_C)
        tot = jnp.dot(band_ref[0, g], jnp.concatenate([ext_hi[:, sl], ext_lo[:, sl]], axis=1),
                      preferred_element_type=F32)
        mean = (tot[:, :POOL_C] + tot[:, POOL_C:]) * inv_ref[0, :, sl]
        d = (mean - main_ref[0, :, sl]).astype(BF16)
        y = jnp.dot(d, w_ref[g], preferred_element_type=F32) * scale_ref[:, sl]
        o_ref[0, :, sl] = y.astype(o_ref.dtype)


def _pool_windows(n, tp):
    starts = (0, tp if n > 2 * tp else 0, n - tp, 0)
    seqs = (n if n > tp else 2 * tp, n if n > tp else 3 * tp, n, tp)
    bands = np.zeros((4, POOL_G, tp, tp + 2 * POOL_HALO), np.float32)
    inv = np.zeros((4, tp, GW), np.float32)
    for e, (t0, length) in enumerate(zip(starts, seqs)):
        t = t0 + np.arange(tp)[:, None]
        pos = t0 - POOL_HALO + np.arange(tp + 2 * POOL_HALO)[None, :]
        for g, win in enumerate(POOL_WINDOWS):
            lo = np.clip(t - win // 2, 0, length)
            hi = np.clip(t - win // 2 + win, 0, length)
            bands[e, g] = (pos >= lo) & (pos < hi)
            inv[e, :, g * POOL_C:(g + 1) * POOL_C] = 1.0 / (hi - lo)
    return jnp.asarray(bands, dtype=BF16), jnp.asarray(inv)


def _pool(pu, w_pool, scale, tp=256):
    b, n, _ = pu.shape
    hb = tp // POOL_HALO
    nt = n // tp
    last = n // POOL_HALO - 1
    bands, inv = _pool_windows(n, tp)
    edge = lambda i: jnp.where(i == 0, 0, jnp.where(i == nt - 1, 2, 1)) if nt > 1 else 3
    return pl.pallas_call(
        _pool_kernel,
        grid=(b, nt),
        in_specs=[pl.BlockSpec((1, POOL_HALO, GW), lambda bi, i: (bi, jnp.maximum(i * hb - 1, 0), 0)),
                  pl.BlockSpec((1, tp, GW), lambda bi, i: (bi, i, 0)),
                  pl.BlockSpec((1, POOL_HALO, GW), lambda bi, i: (bi, jnp.minimum((i + 1) * hb, last), 0)),
                  pl.BlockSpec((1,) + bands.shape[1:], lambda bi, i: (edge(i), 0, 0, 0)),
                  pl.BlockSpec((1,) + inv.shape[1:], lambda bi, i: (edge(i), 0, 0)),
                  _const_spec(w_pool.shape), _const_spec(scale.shape)],
        out_specs=pl.BlockSpec((1, tp, GW), lambda bi, i: (bi, i, 0)),
        out_shape=jax.ShapeDtypeStruct((b, n, GW), BF16),
        compiler_params=_cp(("parallel", "parallel")),
        name="pool_mixer",
    )(pu, pu, pu, bands, inv, w_pool, scale)


def _outproj_kernel(oa_ref, ob_ref, oc_ref, od_ref, x_ref, mod_ref, g2_ref, w_ref, o_ref, h_ref):
    mix = jnp.dot(oa_ref[...], w_ref[0:GW, :], preferred_element_type=F32)
    mix += jnp.dot(ob_ref[...], w_ref[GW:2 * GW, :], preferred_element_type=F32)
    mix += jnp.dot(oc_ref[...], w_ref[2 * GW:3 * GW, :], preferred_element_type=F32)
    mix += jnp.dot(od_ref[...], w_ref[3 * GW:4 * GW, :], preferred_element_type=F32)
    mod = mod_ref[0]
    x = x_ref[...] + mod[2:3, :] * mix
    o_ref[...] = x
    h_ref[...] = _modulated(x, g2_ref[...], mod, 3).astype(BF16)


def _outproj(oa, ob, oc, od, x2, mod, mod_row, g2, w_out, tm=256):
    t, d = x2.shape
    tok = lambda w: pl.BlockSpec((tm, w), lambda i: (i, 0))
    return pl.pallas_call(
        _outproj_kernel,
        grid=(t // tm,),
        in_specs=[tok(GW)] * 4 + [tok(d), pl.BlockSpec((1, N_MOD, d), lambda i: (mod_row(i), 0, 0)),
                                   _const_spec((1, d)), _const_spec(w_out.shape)],
        out_specs=[tok(d), tok(d)],
        out_shape=[jax.ShapeDtypeStruct((t, d), F32), jax.ShapeDtypeStruct((t, d), BF16)],
        compiler_params=_cp(("parallel",)),
        name="outproj",
    )(oa, ob, oc, od, x2, mod, g2, w_out)


FFN_HALO = 16


def _ffn_kernel(prev_ref, hm_ref, next_ref, x_ref, mod_ref, wg_ref, wv_ref, cwg_ref, cwv_ref, cbg_ref,
                cbv_ref, wd_ref, o_ref, h_ref, ug_ref, uv_ref, *, seq):
    i = pl.program_id(0)
    j = pl.program_id(1)
    tm = x_ref.shape[0]
    tiles_per_seq = max(seq // tm, 1)

    @pl.when(j == 0)
    def _():
        first = (i % tiles_per_seq) == 0
        last = (i % tiles_per_seq) == tiles_per_seq - 1
        h_ref[0:FFN_HALO, :] = jnp.where(first, jnp.zeros_like(prev_ref), prev_ref[...])
        h_ref[FFN_HALO:FFN_HALO + tm, :] = hm_ref[...]
        h_ref[FFN_HALO + tm:, :] = jnp.where(last, jnp.zeros_like(next_ref), next_ref[...])
        o_ref[...] = jnp.zeros_like(o_ref)

    h = h_ref[...]
    if seq < tm:
        pos = lax.broadcasted_iota(jnp.int32, (tm, 1), 0)
        seq_start = (pos % seq) == 0
        seq_end = (pos % seq) == seq - 1

    def conv(w_ref, cw_ref, cb_ref, u_ref):
        u_ref[...] = jnp.dot(h, w_ref[...], preferred_element_type=F32)
        cw = cw_ref[...]
        before = u_ref[FFN_HALO - 1:FFN_HALO - 1 + tm, :]
        after = u_ref[FFN_HALO + 1:FFN_HALO + 1 + tm, :]
        if seq < tm:
            before = jnp.where(seq_start, 0.0, before)
            after = jnp.where(seq_end, 0.0, after)
        return (before * cw[0:1] + u_ref[FFN_HALO:FFN_HALO + tm, :] * cw[1:2] + after * cw[2:3]
                + cb_ref[...])

    gate = conv(wg_ref, cwg_ref, cbg_ref, ug_ref)
    val = conv(wv_ref, cwv_ref, cbv_ref, uv_ref)
    a = (gate * jax.nn.sigmoid(gate) * val).astype(BF16)
    o_ref[...] += jnp.dot(a, wd_ref[...], preferred_element_type=F32)

    @pl.when(j == pl.num_programs(1) - 1)
    def _():
        o_ref[...] = x_ref[...] + mod_ref[0, 5:6, :] * o_ref[...]


def _ffn(x2, h2, mod, mod_row, w_up, conv_w, conv_b, w_down, layer, seq, tm, tn=512):
    t, d = x2.shape
    dff = w_down.shape[1]
    assert t % tm == 0 and (seq % tm == 0 or tm % seq == 0)
    nj = dff // tn
    hb = tm // FFN_HALO
    nhb = t // FFN_HALO
    return pl.pallas_call(
        functools.partial(_ffn_kernel, seq=seq),
        grid=(t // tm, nj),
        in_specs=[pl.BlockSpec((FFN_HALO, d), lambda i, j: (jnp.maximum(i * hb - 1, 0), 0)),
                  pl.BlockSpec((tm, d), lambda i, j: (i, 0)),
                  pl.BlockSpec((FFN_HALO, d), lambda i, j: (jnp.minimum((i + 1) * hb, nhb - 1), 0)),
                  pl.BlockSpec((tm, d), lambda i, j: (i, 0)),
                  pl.BlockSpec((1, N_MOD, d), lambda i, j: (mod_row(i), 0, 0)),
                  pl.BlockSpec((None, d, tn), lambda i, j: (layer, 0, j)),
                  pl.BlockSpec((None, d, tn), lambda i, j: (layer, 0, nj + j)),
                  pl.BlockSpec((CONV_W, tn), lambda i, j: (0, j)),
                  pl.BlockSpec((CONV_W, tn), lambda i, j: (0, nj + j)),
                  pl.BlockSpec((1, tn), lambda i, j: (0, j)),
                  pl.BlockSpec((1, tn), lambda i, j: (0, nj + j)),
                  pl.BlockSpec((None, tn, d), lambda i, j: (layer, j, 0))],
        out_specs=pl.BlockSpec((tm, d), lambda i, j: (i, 0)),
        out_shape=jax.ShapeDtypeStruct((t, d), F32),
        scratch_shapes=[pltpu.VMEM((tm + 2 * FFN_HALO, d), BF16),
                        pltpu.VMEM((tm + 2 * FFN_HALO, tn), F32), pltpu.VMEM((tm + 2 * FFN_HALO, tn), F32)],
        compiler_params=_cp(("parallel", "arbitrary")),
        name="conv_ffn",
    )(h2, h2, h2, x2, mod, w_up, w_up, conv_w, conv_w, conv_b, conv_b, w_down)


def _block_ones(group):
    idx = np.arange(GW) // group
    return jnp.asarray(idx[:, None] == idx[None, :], dtype=BF16)


def _pad_heads(w, heads, width):
    lead = w.shape[:-1]
    w = w.reshape(lead + (heads, width))
    w = jnp.pad(w, [(0, 0)] * len(lead) + [(0, 0), (0, LANES - width)])
    return w.reshape(lead + (heads * LANES,))


def _layer_weights(P, l):
    d = P["w_in"].shape[1]
    w_in = P["w_in"][l].astype(BF16)
    ml0, ml1 = 3 * GW, 3 * GW + MLA_Q_LORA + MLA_KV_LORA + MLA_ROPE
    w_all = jnp.concatenate([w_in[:, :ml0], w_in[:, ml1:], w_in[:, ml0:ml1],
                             jnp.zeros((d, ML_W - (ml1 - ml0)), BF16)], axis=1)
    wukv = P["mla_w_ukv"][l].reshape(MLA_KV_LORA, MLA_H, MLA_NOPE + MLA_V)
    wuk = _pad_heads(wukv[:, :, :MLA_NOPE].reshape(MLA_KV_LORA, MLA_H * MLA_NOPE), MLA_H, MLA_NOPE)
    wuv = wukv[:, :, MLA_NOPE:].reshape(MLA_KV_LORA, MLA_H * MLA_V)
    tile = lambda g, reps: jnp.tile(g.astype(F32), reps)
    padw = lambda g: jnp.pad(g.astype(F32), (0, GW - g.shape[0]))
    gains = jnp.stack([
        tile(P["diff_qn_g"][l], 8) * (DIFF_DH ** -0.5 * LOG2E),
        tile(P["diff_kn_g"][l], 8),
        tile(P["na_qn_g"][l], 8) * (NA_DH ** -0.5 * LOG2E),
        tile(P["na_kn_g"][l], 8),
        _pad_heads(tile(P["mla_qn_g"][l], MLA_H), MLA_H, MLA_QK) * (MLA_QK ** -0.5 * LOG2E),
        _pad_heads(tile(P["mla_kn_g"][l], MLA_H), MLA_H, MLA_QK),
        padw(P["mla_qa_g"][l]),
        padw(P["mla_kva_g"][l]),
    ])
    return dict(
        w_in=w_all,
        wuq=_pad_heads(P["mla_w_uq"][l], MLA_H, MLA_QK).astype(BF16),
        wuk=wuk.astype(BF16), wuv=wuv.astype(BF16),
        gains=gains, g64=_block_ones(64), g128=_block_ones(LANES),
        norm1=P["norm1_g"][l].reshape(1, d).astype(F32),
        norm2=P["norm2_g"][l].reshape(1, d).astype(F32),
        lam=P["diff_lam"][l].astype(F32),
        subg=P["diff_sub_g"][l].reshape(1, 2 * DIFF_DH).astype(F32),
        pool_w=P["pool_w"][l].astype(BF16),
        pool_scale=P["pool_scale"][l].reshape(1, GW).astype(F32),
        w_out=P["w_out"][l].astype(BF16),
        conv_w=P["conv_w"][l].astype(F32),
        conv_b=P["conv_b"][l].reshape(1, -1).astype(F32),
    )


def _rope_tables(n):
    t = jnp.arange(n)
    rows = (t // GRID_W).astype(F32)
    cols = (t % GRID_W).astype(F32)

    def ang(dim):
        quarter = dim // 4
        inv = ROPE_BASE ** (-jnp.arange(quarter, dtype=F32) / quarter)
        return jnp.concatenate([rows[:, None] * inv, cols[:, None] * inv], axis=-1)

    a = ang(DIFF_DH)
    cos_a, sin_a = jnp.cos(a), jnp.sin(a)
    cd = jnp.tile(cos_a, (1, 4))
    sd = jnp.tile(jnp.concatenate([-sin_a, sin_a], axis=1), (1, 2))
    b = ang(MLA_ROPE)
    cos_b, sin_b = jnp.cos(b), jnp.sin(b)
    ones = jnp.ones((n, MLA_NOPE), F32)
    pad = LANES - MLA_NOPE - MLA_ROPE
    cm = jnp.concatenate([ones, cos_b, cos_b, jnp.ones((n, pad), F32)], axis=1)
    sm = jnp.concatenate([0.0 * ones, -sin_b, sin_b, jnp.zeros((n, pad), F32)], axis=1)
    return jnp.stack([cd, sd, cm, sm])


def _layer(x, mod, mod_row_of_batch, W, layer, cache, rope_tab, bias_tab):
    b, n, d = x.shape
    t = b * n
    x2 = x.reshape(t, d)
    lam_init = 0.8 - 0.6 * math.exp(-0.3 * layer)
    tm = min(512, n)
    row_in = lambda i: mod_row_of_batch(i // (n // tm))
    outs = _inproj(x2, mod, row_in, W["norm1"], W, rope_tab, n, states=cache is None, tm=tm)
    qd, kd, vd, qn, kn, vn, pu, qm, km, vm = [a.reshape(b, n, GW) for a in outs[:10]]
    if cache is None:
        oa = _attention("diff", qd, [(kd, vd)], n, n, W["lam"], W["subg"], lam_init)
        ob = _attention("full", qm, [(km, vm)], n, n)
        oc = _attention("pair", qn, [(kn, vn)], n, n)
        state = outs[10:]
    else:
        a_k, a_v, b_ckv, b_kpe, c_k, c_v = cache
        pc = a_k.shape[1]
        ck = a_k.reshape(b, pc, GW).astype(BF16)
        cv = a_v.reshape(b, pc, GW).astype(BF16)
        oa = _attention_latent("diff", qd, ck, cv, kd, vd, W["gains"][0:1], DIFF_DH, W["lam"], W["subg"],
                               lam_init)
        kpe_pad = jnp.pad(b_kpe.reshape(b * pc, MLA_ROPE), ((0, 0), (0, LANES - MLA_ROPE)))
        kmc, vmc = _mla_cache(b_ckv.reshape(b * pc, MLA_KV_LORA), kpe_pad, W)
        ob = _attention_latent("full", qm, kmc.reshape(b, pc, GW), vmc.reshape(b, pc, GW), km, vm,
                               W["gains"][4:5], MLA_QK)
        nk = c_k.reshape(b, pc, GW).astype(BF16)
        nv = c_v.reshape(b, pc, GW).astype(BF16)
        oc = _na_latent(qn, kn, vn, nk, nv, bias_tab)
        state = ()
    od = _pool(pu, W["pool_w"], W["pool_scale"])
    flat = lambda a: a.reshape(t, GW)
    x2, h2 = _outproj(flat(oa), flat(ob), flat(oc), flat(od), x2, mod, row_in, W["norm2"], W["w_out"], tm=tm)
    tmf = min(512, t) if cache is None else min(512, n)
    row_ffn = lambda i: mod_row_of_batch((i * tmf) // n)
    x2 = _ffn(x2, h2, mod, row_ffn, W["w_up"], W["conv_w"], W["conv_b"], W["w_down"], layer, n, tm=tmf)
    return x2.reshape(b, n, d), state


def kernel(x_prompt, x_sample, cache_diff_k, cache_diff_v, cache_mla_ckv, cache_mla_kpe, cache_na_k, cache_na_v, c, c_ctx, norm1_g, norm2_g, ada_w, ada_b, w_in, diff_qn_g, diff_kn_g, diff_lam, diff_sub_g, mla_qa_g, mla_kva_g, mla_w_uq, mla_w_ukv, mla_qn_g, mla_kn_g, na_qn_g, na_kn_g, na_bias, pool_w, pool_scale, w_out, w_up, conv_w, conv_b, w_down):
    P = dict(norm1_g=norm1_g, norm2_g=norm2_g, w_in=w_in, diff_qn_g=diff_qn_g, diff_kn_g=diff_kn_g,
             diff_lam=diff_lam, diff_sub_g=diff_sub_g, mla_qa_g=mla_qa_g, mla_kva_g=mla_kva_g,
             mla_w_uq=mla_w_uq, mla_w_ukv=mla_w_ukv, mla_qn_g=mla_qn_g, mla_kn_g=mla_kn_g,
             na_qn_g=na_qn_g, na_kn_g=na_kn_g, pool_w=pool_w, pool_scale=pool_scale, w_out=w_out,
             w_up=w_up, conv_w=conv_w, conv_b=conv_b, w_down=w_down)
    depth = w_in.shape[0]
    d = x_prompt.shape[-1]
    bd, nd = x_sample.shape[:2]
    n_rows = -(-(bd + 1) // 8) * 8
    cv = jnp.concatenate([c, c_ctx[None, :], jnp.zeros((n_rows - bd - 1, d), F32)], axis=0)
    mod_all = _ada(cv, ada_w, ada_b).reshape(depth, n_rows, N_MOD, d)
    rope_tab = _rope_tables(nd)
    Ws = [_layer_weights(P, l) for l in range(depth)]
    w_up_all, w_down_all = w_up.astype(BF16), w_down.astype(BF16)
    for W in Ws:
        W.update(w_up=w_up_all, w_down=w_down_all)

    xp = x_prompt
    states = []
    for l in range(depth):
        xp, st = _layer(xp, mod_all[l], lambda bi: bd, Ws[l], l, None, None, None)
        states.append(st)
    xs = x_sample
    for l in range(depth):
        cache_l = (cache_diff_k[:, l], cache_diff_v[:, l], cache_mla_ckv[:, l], cache_mla_kpe[:, l],
                   cache_na_k[:, l], cache_na_v[:, l])
        bias_tab = _na_bias_table(na_bias[l], nd // GRID_W)
        xs, _ = _layer(xs, mod_all[l], lambda bi: bi, Ws[l], l, cache_l, rope_tab, bias_tab)

    bp, npr = x_prompt.shape[:2]

    def stack(k, shape):
        return jnp.stack([s[k].reshape((bp, npr) + shape) for s in states], axis=1)

    return (xp, xs,
            stack(0, (DIFF_H, 2, DIFF_DH)), stack(1, (DIFF_H, 2 * DIFF_DH)),
            stack(2, (MLA_KV_LORA,)), stack(3, (MLA_ROPE,)),
            stack(4, (NA_H, NA_DH)), stack(5, (NA_H, NA_DH)))
```

```python
import functools
import math

import numpy as np
import jax
import jax.numpy as jnp
from jax import lax
from jax.experimental import pallas as pl
from jax.experimental.pallas import tpu as pltpu

F32 = jnp.float32
BF16 = jnp.bfloat16

GRID_W = 64
ROPE_BASE = 10000.0
EPS = 1e-6
N_MOD = 6
DIFF_H = 4
DIFF_DH = 64
MLA_H = 4
MLA_NOPE = 64
MLA_ROPE = 32
MLA_V = 128
MLA_Q_LORA = 384
MLA_KV_LORA = 128
NA_H = 8
NA_DH = 64
NA_ROWS = 8
NA_COLS = 16
POOL_WINDOWS = (2, 4, 8, 16)
POOL_G = 4
POOL_C = 128
CONV_W = 3

GW = 512
LANES = 128
NSLOT = GW // LANES
MXU_DIM = 256
MLA_QK = MLA_NOPE + MLA_ROPE
V7X_VMEM_LIMIT = 56 * 1024 * 1024
NEG = -1e30
LOG2E = math.log2(math.e)

C_DQ, C_DK, C_DV, C_NQ, C_NK, C_NV, C_PU, C_ML = 0, 512, 1024, 1536, 2048, 2560, 3072, 3584
ML_W = 640
IN_WP = C_ML + ML_W


def _cp(sem, vmem=V7X_VMEM_LIMIT):
    return pltpu.CompilerParams(dimension_semantics=sem, vmem_limit_bytes=vmem)


def _const_spec(shape):
    nd = len(shape)
    return pl.BlockSpec(shape, lambda *_: (0,) * nd, pipeline_mode=pl.Buffered(1))


def _ada_kernel(c_ref, w_ref, b_ref, o_ref):
    c = c_ref[...]
    s = c * jax.nn.sigmoid(c)
    o_ref[0] = jnp.dot(s.astype(BF16), w_ref[0].astype(BF16), preferred_element_type=F32) + b_ref[0]


def _ada(cv, ada_w, ada_b, tn=1024):
    depth, d, n = ada_w.shape
    r = cv.shape[0]
    return pl.pallas_call(
        _ada_kernel,
        grid=(depth, n // tn),
        in_specs=[pl.BlockSpec((r, d), lambda l, j: (0, 0)),
                  pl.BlockSpec((1, d, tn), lambda l, j: (l, 0, j)),
                  pl.BlockSpec((1, 1, tn), lambda l, j: (l, 0, j))],
        out_specs=pl.BlockSpec((1, r, tn), lambda l, j: (l, 0, j)),
        out_shape=jax.ShapeDtypeStruct((depth, r, n), F32),
        compiler_params=_cp(("parallel", "parallel")),
        name="adaln",
    )(cv, ada_w, ada_b.reshape(depth, 1, n))


def _modulated(x, g, mod, k):
    ms = jnp.mean(x * x, axis=-1, keepdims=True)
    y = x * lax.rsqrt(ms + EPS) * g
    return y * (1.0 + mod[k + 1:k + 2, :]) + mod[k:k + 1, :]


def _group_rms(z, gsum_ref, inv_n, gain):
    sq = (z * z).astype(BF16)
    ss = jnp.concatenate(
        [jnp.dot(sq[:, c:c + MXU_DIM], gsum_ref[c:c + MXU_DIM, c:c + MXU_DIM], preferred_element_type=F32)
         for c in range(0, z.shape[1], MXU_DIM)], axis=1)
    return z * lax.rsqrt(ss * inv_n + EPS) * gain


def _rope_chunks(z, cos_ref, sin_ref, lo_pred, shift_lo, shift_hi):
    c = cos_ref[...]
    s = sin_ref[...]
    outs = []
    for j in range(z.shape[1] // LANES):
        zc = z[:, j * LANES:(j + 1) * LANES]
        partner = jnp.where(lo_pred, pltpu.roll(zc, shift_lo, 1), pltpu.roll(zc, shift_hi, 1))
        outs.append(zc * c + partner * s)
    return jnp.concatenate(outs, axis=1)


def _mla_kv(ckv_b, kp, wuk_ref, wuv_ref, g128_ref, kgain):
    kp_slot = pltpu.roll(kp, MLA_NOPE, 1)
    kh = jnp.dot(ckv_b, wuk_ref[...], preferred_element_type=F32) + jnp.concatenate([kp_slot] * MLA_H, axis=1)
    k = _group_rms(kh, g128_ref, 1.0 / MLA_QK, kgain)
    v = jnp.dot(ckv_b, wuv_ref[...], preferred_element_type=F32)
    return k, v


def _inproj_kernel(*refs, rope, states):
    it = iter(refs)
    x_ref, mod_ref, g_ref, w_ref, wuq_ref, wuk_ref, wuv_ref = (next(it) for _ in range(7))
    gains_ref, g64_ref, g128_ref = (next(it) for _ in range(3))
    if rope:
        rope_ref = next(it)
    qd_ref, kd_ref, vd_ref, qn_ref, kn_ref, vn_ref, pu_ref, qm_ref, km_ref, vm_ref = (
        next(it) for _ in range(10))
    if states:
        ska_ref, sva_ref, sckv_ref, skpe_ref, skc_ref, svc_ref = (next(it) for _ in range(6))

    h = _modulated(x_ref[...], g_ref[...], mod_ref[0], 0).astype(BF16)
    gains = gains_ref[...]
    lane = lax.broadcasted_iota(jnp.int32, (1, LANES), 1)

    def seg(c0, w=GW):
        return jnp.dot(h, w_ref[:, c0:c0 + w], preferred_element_type=F32)

    def rope_d(z):
        if not rope:
            return z
        return _rope_chunks(z, rope_ref.at[0], rope_ref.at[1], (lane % 64) < 32, LANES - 32, 32)

    def rope_m(z):
        if not rope:
            return z
        return _rope_chunks(z, rope_ref.at[2], rope_ref.at[3], lane < MLA_NOPE + MLA_ROPE // 2,
                            LANES - MLA_ROPE // 2, MLA_ROPE // 2)

    qd_ref[...] = rope_d(_group_rms(seg(C_DQ), g64_ref, 1.0 / DIFF_DH, gains[0:1])).astype(BF16)
    kd = _group_rms(seg(C_DK), g64_ref, 1.0 / DIFF_DH, gains[1:2])
    kd_ref[...] = rope_d(kd).astype(BF16)
    vd = seg(C_DV)
    vd_ref[...] = vd.astype(BF16)
    qn_ref[...] = _group_rms(seg(C_NQ), g64_ref, 1.0 / NA_DH, gains[2:3]).astype(BF16)
    kn = _group_rms(seg(C_NK), g64_ref, 1.0 / NA_DH, gains[3:4])
    kn_ref[...] = kn.astype(BF16)
    vn = seg(C_NV)
    vn_ref[...] = vn.astype(BF16)
    pu_ref[...] = seg(C_PU)
    zm = seg(C_ML, ML_W)
    cq = zm[:, :MLA_Q_LORA]
    cq = cq * lax.rsqrt(jnp.mean(cq * cq, axis=-1, keepdims=True) + EPS) * gains[6:7, :MLA_Q_LORA]
    qh = jnp.dot(cq.astype(BF16), wuq_ref[...], preferred_element_type=F32)
    qm_ref[...] = rope_m(_group_rms(qh, g128_ref, 1.0 / MLA_QK, gains[4:5])).astype(BF16)
    ck = zm[:, MLA_Q_LORA:MLA_Q_LORA + MLA_KV_LORA]
    ckv = ck * lax.rsqrt(jnp.mean(ck * ck, axis=-1, keepdims=True) + EPS) * gains[7:8, :MLA_KV_LORA]
    kp = zm[:, GW:GW + LANES]
    km, vm = _mla_kv(ckv.astype(BF16), kp, wuk_ref, wuv_ref, g128_ref, gains[5:6])
    km_ref[...] = rope_m(km).astype(BF16)
    vm_ref[...] = vm.astype(BF16)
    if states:
        ska_ref[...] = kd
        sva_ref[...] = vd
        sckv_ref[...] = ckv
        skpe_ref[...] = kp[:, :MLA_ROPE]
        skc_ref[...] = kn
        svc_ref[...] = vn


def _inproj(x2, mod, mod_row, g1, W, rope_tab, seq, states, tm=256):
    t, d = x2.shape
    rope = rope_tab is not None
    tiles_per_seq = seq // tm
    tok = lambda w: pl.BlockSpec((tm, w), lambda i: (i, 0))
    in_specs = [tok(d),
                pl.BlockSpec((1, N_MOD, d), lambda i: (mod_row(i), 0, 0)),
                _const_spec((1, d)),
                _const_spec(W["w_in"].shape), _const_spec(W["wuq"].shape),
                _const_spec(W["wuk"].shape), _const_spec(W["wuv"].shape),
                _const_spec(W["gains"].shape), _const_spec(W["g64"].shape),
                _const_spec(W["g128"].shape)]
    args = [x2, mod, g1, W["w_in"], W["wuq"], W["wuk"], W["wuv"], W["gains"], W["g64"], W["g128"]]
    if rope:
        in_specs.append(pl.BlockSpec((4, tm, LANES), lambda i: (0, i % tiles_per_seq, 0)))
        args.append(rope_tab)
    out_specs = [tok(GW)] * 10
    out_shape = [jax.ShapeDtypeStruct((t, GW), BF16)] * 6 + [jax.ShapeDtypeStruct((t, GW), F32)] + \
                [jax.ShapeDtypeStruct((t, GW), BF16)] * 3
    if states:
        widths = (GW, GW, MLA_KV_LORA, MLA_ROPE, GW, GW)
        out_specs += [tok(w) for w in widths]
        out_shape += [jax.ShapeDtypeStruct((t, w), F32) for w in widths]
    return pl.pallas_call(
        functools.partial(_inproj_kernel, rope=rope, states=states),
        grid=(t // tm,),
        in_specs=in_specs, out_specs=out_specs, out_shape=out_shape,
        compiler_params=_cp(("parallel",)),
        name="inproj",
    )(*args)


def _mla_cache_kernel(ckv_ref, kp_ref, wuk_ref, wuv_ref, g128_ref, gains_ref, k_ref, v_ref):
    k, v = _mla_kv(ckv_ref[...].astype(BF16), kp_ref[...], wuk_ref, wuv_ref, g128_ref, gains_ref[5:6, :])
    k_ref[...] = k.astype(BF16)
    v_ref[...] = v.astype(BF16)


def _mla_cache(ckv2, kp2, W, tm=256):
    t = ckv2.shape[0]
    tok = lambda w: pl.BlockSpec((tm, w), lambda i: (i, 0))
    return pl.pallas_call(
        _mla_cache_kernel,
        grid=(t // tm,),
        in_specs=[tok(MLA_KV_LORA), tok(LANES), _const_spec(W["wuk"].shape), _const_spec(W["wuv"].shape),
                  _const_spec(W["g128"].shape), _const_spec(W["gains"].shape)],
        out_specs=[tok(GW), tok(GW)],
        out_shape=[jax.ShapeDtypeStruct((t, GW), BF16)] * 2,
        compiler_params=_cp(("parallel",)),
        name="mla_cache_kv",
    )(ckv2, kp2, W["wuk"], W["wuv"], W["g128"], W["gains"])


def _diff_lambda(lam_ref, lam_init):
    lf = lam_ref[...]
    return (jnp.exp(jnp.sum(lf[0:1] * lf[1:2], axis=-1, keepdims=True))
            - jnp.exp(jnp.sum(lf[2:3] * lf[3:4], axis=-1, keepdims=True)) + lam_init)


def _stream_queries(q, mode):
    if mode == "full":
        return [q]
    lane = lax.broadcasted_iota(jnp.int32, (1, LANES), 1)
    zero = jnp.zeros_like(q)
    return [jnp.where(lane < 64, q, zero), jnp.where(lane < 64, zero, q)]


def _combine_streams(outs, mode, lam, subg_ref, lam_init):
    if mode == "full":
        return outs[0]
    if mode == "pair":
        lane = lax.broadcasted_iota(jnp.int32, (1, LANES), 1)
        return jnp.where(lane < 64, outs[0], outs[1])
    o = outs[0] - lam * outs[1]
    o = o * lax.rsqrt(jnp.mean(o * o, axis=-1, keepdims=True) + EPS) * subg_ref[...]
    return o * (1.0 - lam_init)


def _online_attend(q_ref, srcs, chunks, mode, lam, subg_ref, lam_init, o_ref):
    tq = q_ref.shape[1]
    nstream = 1 if mode == "full" else 2
    for slot in range(NSLOT):
        sl = slice(slot * LANES, (slot + 1) * LANES)
        qs = _stream_queries(q_ref[0, :, sl], mode)

        def step(k_c, v_cs, state):
            new = []
            for si in range(nstream):
                m, l, acc = state[3 * si:3 * si + 3]
                s = lax.dot_general(qs[si], k_c, (((1,), (1,)), ((), ())), preferred_element_type=F32)
                m_new = jnp.maximum(m, jnp.max(s, axis=-1, keepdims=True))
                alpha = jnp.exp2(m - m_new)
                p = jnp.exp2(s - m_new)
                l = alpha * l + jnp.sum(p, axis=-1, keepdims=True)
                acc = alpha * acc + jnp.dot(p.astype(BF16), v_cs[si % len(v_cs)],
                                            preferred_element_type=F32)
                new += [m_new, l, acc]
            return tuple(new)

        state = ()
        for _ in range(nstream):
            state += (jnp.full((tq, 1), NEG, F32), jnp.zeros((tq, 1), F32), jnp.zeros((tq, LANES), F32))
        for (k_ref, v_refs), (n_chunk, tk) in zip(srcs, chunks):
            if n_chunk == 1:
                state = step(k_ref[0, :, sl], [v[0, :, sl] for v in v_refs], state)
            else:
                def body(c, st, k_ref=k_ref, v_refs=v_refs, tk=tk):
                    rows = pl.ds(pl.multiple_of(c * tk, tk), tk)
                    return step(k_ref[0, rows, sl], [v[0, rows, sl] for v in v_refs], st)
                state = lax.fori_loop(0, n_chunk, body, state)

        outs = [state[3 * si + 2] / state[3 * si + 1] for si in range(nstream)]
        o_ref[0, :, sl] = _combine_streams(outs, mode, lam, subg_ref, lam_init).astype(o_ref.dtype)


def _attn_kernel(*refs, mode, chunks, lam_init):
    it = iter(refs)
    q_ref = next(it)
    srcs = []
    for _ in chunks:
        k_ref = next(it)
        srcs.append((k_ref, (next(it),)))
    lam, subg_ref = None, None
    if mode == "diff":
        lam = _diff_lambda(next(it), lam_init)
        subg_ref = next(it)
    o_ref = next(it)
    _online_attend(q_ref, srcs, chunks, mode, lam, subg_ref, lam_init, o_ref)


AUG = 2 * LANES
SHIFT_MARGIN = 1.0 + 2.0 ** -6
SHIFT_LIMIT = 48.0
ATTN_ROWS = {"diff": 512, "full": 1024}
VT_ROWS = LANES + 16


def _attn_fast_kernel(*refs, mode, chunks, lam_init, head_dim, sub):
    it = iter(refs)
    q_ref, kc_ref, vc_ref, k_ref, v_ref, qgain_ref = (next(it) for _ in range(6))
    lam, subg_ref = None, None
    if mode == "diff":
        lam = _diff_lambda(next(it), lam_init)
        subg_ref = next(it)
    o_ref, kaug_ref, vext_ref, shift_ref = (next(it) for _ in range(4))

    tq = q_ref.shape[1]
    pc, n = kc_ref.shape[1], k_ref.shape[1]
    kt = pc + n
    nstream = 1 if mode == "full" else 2
    lane = lax.broadcasted_iota(jnp.int32, (1, LANES), 1)

    @pl.when(pl.program_id(1) == 0)
    def _():
        q_bound = math.sqrt(head_dim) * jnp.max(jnp.abs(qgain_ref[...]), axis=-1, keepdims=True)
        shift_ref[...] = jnp.zeros_like(shift_ref)
        for slot in range(NSLOT):
            sl = slice(slot * LANES, (slot + 1) * LANES)
            a0 = slot * AUG
            kaug_ref[0:pc, a0:a0 + LANES] = kc_ref[0, :, sl]
            kaug_ref[pc:kt, a0:a0 + LANES] = k_ref[0, :, sl]
            r0 = slot * VT_ROWS
            for c0 in range(0, kt, MXU_DIM):
                chunk = (vc_ref[0, c0:c0 + MXU_DIM, sl] if c0 < pc
                         else v_ref[0, c0 - pc:c0 - pc + MXU_DIM, sl])
                vext_ref[r0:r0 + LANES, c0:c0 + MXU_DIM] = chunk.astype(F32).T.astype(BF16)
            sub = lax.broadcasted_iota(jnp.int32, (VT_ROWS - LANES, 1), 0)
            vext_ref[r0 + LANES:r0 + VT_ROWS, :] = jnp.broadcast_to(
                jnp.where(sub == 0, 1.0, 0.0).astype(BF16), (VT_ROWS - LANES, kt))
            shift_lanes = jnp.zeros((1, LANES), F32)
            for si in range(nstream):
                best = None
                for src in (kc_ref, k_ref):
                    kk = _stream_queries(src[0, :, sl], mode)[si].astype(F32)
                    nrm = jnp.max(jnp.sum(kk * kk, axis=-1, keepdims=True), axis=0, keepdims=True)
                    best = nrm if best is None else jnp.maximum(best, nrm)
                shift = q_bound * jnp.sqrt(best) * SHIFT_MARGIN
                row = slot * nstream + si
                shift_ref[row:row + 1, :] = jnp.broadcast_to(shift, (1, LANES))
                shift_lanes = jnp.where(lane == si, -shift, shift_lanes)
            kaug_ref[:, a0 + LANES:a0 + AUG] = jnp.broadcast_to(shift_lanes.astype(BF16), (kt, LANES))

    safe = jnp.max(shift_ref[...]) <= SHIFT_LIMIT

    @pl.when(safe)
    def _():
        def sub_tile(t, carry):
            rows = pl.ds(pl.multiple_of(t * sub, sub), sub)
            for slot in range(NSLOT):
                sl = slice(slot * LANES, (slot + 1) * LANES)
                a0 = slot * AUG
                outs = []
                for si, qm in enumerate(_stream_queries(q_ref[0, rows, sl], mode)):
                    pick = jnp.broadcast_to(jnp.where(lane == si, 1.0, 0.0).astype(BF16), (sub, LANES))
                    qa = jnp.concatenate([qm, pick], axis=1)
                    s_t = lax.dot_general(kaug_ref[:, a0:a0 + AUG], qa, (((1,), (1,)), ((), ())),
                                          preferred_element_type=F32)
                    acc_t = jnp.dot(vext_ref[slot * VT_ROWS:(slot + 1) * VT_ROWS, :],
                                    jnp.exp2(s_t).astype(BF16), preferred_element_type=F32)
                    outs.append((acc_t[0:LANES] / acc_t[LANES:LANES + 1]).T)
                o = _combine_streams(outs, mode, lam, subg_ref, lam_init)
                o_ref[0, rows, sl] = o.astype(o_ref.dtype)
            return carry
        lax.fori_loop(0, tq // sub, sub_tile, 0)

    @pl.when(jnp.logical_not(safe))
    def _():
        _online_attend(q_ref, [(kc_ref, (vc_ref,)), (k_ref, (v_ref,))], chunks, mode, lam, subg_ref,
                       lam_init, o_ref)


def _attention_latent(mode, q, kc, vc, k, v, qgain, head_dim, lam=None, subg=None, lam_init=0.0,
                      tk=512):
    b, n, _ = q.shape
    tq = min(ATTN_ROWS[mode], n)
    pc = kc.shape[1]
    kt = pc + n
    res = lambda m: pl.BlockSpec((1, m, GW), lambda bi, qi: (bi, 0, 0))
    in_specs = [pl.BlockSpec((1, tq, GW), lambda bi, qi: (bi, qi, 0)), res(pc), res(pc), res(n), res(n),
                _const_spec(qgain.shape)]
    args = [q, kc, vc, k, v, qgain]
    if mode == "diff":
        in_specs += [_const_spec(lam.shape), _const_spec(subg.shape)]
        args += [lam, subg]
    chunks = ((1, pc), (n // tk, tk))
    return pl.pallas_call(
        functools.partial(_attn_fast_kernel, mode=mode, chunks=chunks, lam_init=lam_init, head_dim=head_dim,
                          sub=tq),
        grid=(b, n // tq),
        in_specs=in_specs,
        out_specs=pl.BlockSpec((1, tq, GW), lambda bi, qi: (bi, qi, 0)),
        out_shape=jax.ShapeDtypeStruct((b, n, GW), BF16),
        scratch_shapes=[pltpu.VMEM((kt, NSLOT * AUG), BF16), pltpu.VMEM((NSLOT * VT_ROWS, kt), BF16),
                        pltpu.VMEM((8, LANES), F32)],
        compiler_params=_cp(("parallel", "arbitrary")),
        name="attn_latent_" + mode,
    )(*args)


def _attention(mode, q, srcs, tq, tk, lam=None, subg=None, lam_init=0.0):
    b, n, _ = q.shape
    in_specs = [pl.BlockSpec((1, tq, GW), lambda bi, qi: (bi, qi, 0))]
    args = [q]
    chunks = []
    for src in srcs:
        ks = src[0].shape[1]
        step = min(tk, ks)
        chunks.append((ks // step, step))
        for a in src:
            in_specs.append(pl.BlockSpec((1, ks, GW), lambda bi, qi: (bi, 0, 0)))
            args.append(a)
    if mode == "diff":
        in_specs += [_const_spec(lam.shape), _const_spec(subg.shape)]
        args += [lam, subg]
    return pl.pallas_call(
        functools.partial(_attn_kernel, mode=mode, chunks=tuple(chunks), lam_init=lam_init),
        grid=(b, n // tq),
        in_specs=in_specs,
        out_specs=pl.BlockSpec((1, tq, GW), lambda bi, qi: (bi, qi, 0)),
        out_shape=jax.ShapeDtypeStruct((b, n, GW), BF16),
        compiler_params=_cp(("parallel", "parallel")),
        name="attn_" + mode,
    )(*args)


NA_QROWS = 4
NA_KROWS = NA_QROWS + NA_ROWS


def _na_key_start(r0, rows):
    return jnp.clip(r0 - NA_ROWS // 2, 0, rows - NA_KROWS)


def _na_kernel(q_ref, k_ref, v_ref, kc_ref, vc_ref, bias_ref, o_ref, *, rows):
    r0 = pl.program_id(1) * NA_QROWS
    ks = _na_key_start(r0, rows)
    loc = pl.ds(pl.multiple_of(ks * GRID_W, GRID_W), NA_KROWS * GRID_W)
    lane = lax.broadcasted_iota(jnp.int32, (1, LANES), 1)
    nt = (((1,), (1,)), ((), ()))
    for slot in range(NSLOT):
        sl = slice(slot * LANES, (slot + 1) * LANES)
        q = q_ref[0, :, sl]
        k_loc = k_ref[0, loc, sl]
        k_ctx = kc_ref[0, :, sl]
        v_loc = v_ref[0, loc, sl]
        v_ctx = vc_ref[0, :, sl]
        v_loc = jnp.concatenate([v_loc, jnp.ones_like(v_loc)], axis=1)
        v_ctx = jnp.concatenate([v_ctx, jnp.ones_like(v_ctx)], axis=1)
        halves = []
        for half, qh in enumerate(_stream_queries(q, "pair")):
            s_loc = lax.dot_general(qh, k_loc, nt, preferred_element_type=F32) + bias_ref[0, 2 * slot + half]
            s_ctx = lax.dot_general(qh, k_ctx, nt, preferred_element_type=F32)
            m = jnp.maximum(jnp.max(s_loc, axis=-1, keepdims=True), jnp.max(s_ctx, axis=-1, keepdims=True))
            acc = (jnp.dot(jnp.exp2(s_loc - m).astype(BF16), v_loc, preferred_element_type=F32)
                   + jnp.dot(jnp.exp2(s_ctx - m).astype(BF16), v_ctx, preferred_element_type=F32))
            halves.append(acc[:, 0:LANES] / acc[:, LANES:2 * LANES])
        o_ref[0, :, sl] = jnp.where(lane < 64, halves[0], halves[1]).astype(o_ref.dtype)


def _na_bias_table(bias_tab, rows):
    assert rows >= NA_KROWS + NA_QROWS and rows % NA_QROWS == 0
    qc = np.arange(GRID_W)[:, None]
    kc = np.arange(GRID_W)[None, :]
    cstart = np.clip(qc - NA_COLS // 2, 0, GRID_W - NA_COLS)
    in_win = (kc >= cstart) & (kc < cstart + NA_COLS)
    dc_idx = np.clip(kc - qc, 1 - NA_COLS, NA_COLS - 1) + NA_COLS - 1
    col_pick = (dc_idx[:, :, None] == np.arange(2 * NA_COLS - 1)).astype(np.float32)
    variants = (0, NA_ROWS // 2, NA_ROWS)
    row_pick = np.zeros((len(variants), NA_QROWS, NA_KROWS, 2 * NA_ROWS - 1), np.float32)
    for vi, var in enumerate(variants):
        for j in range(NA_QROWS):
            first = min(max(j + var - NA_ROWS // 2, 0), NA_KROWS - NA_ROWS)
            for i in range(first, first + NA_ROWS):
                row_pick[vi, j, i, i - j - var + NA_ROWS - 1] = 1.0
    valid = (row_pick.sum(-1) > 0)[:, None, :, None, :, None] & in_win[None, None, None, :, None, :]
    toep = jnp.einsum("qkd,hrd->hrqk", jnp.asarray(col_pick), bias_tab.astype(F32),
                      precision=lax.Precision.HIGHEST)
    tab = jnp.einsum("vjir,hrqk->vhjqik", jnp.asarray(row_pick), toep, precision=lax.Precision.HIGHEST)
    tab = jnp.where(jnp.asarray(valid), tab * LOG2E, NEG)
    return tab.reshape(len(variants), NA_H, NA_QROWS * GRID_W, NA_KROWS * GRID_W)


def _na_latent(q, k, v, kc, vc, bias):
    b, n, _ = q.shape
    rows = n // GRID_W
    pc = kc.shape[1]
    full = lambda m: pl.BlockSpec((1, m, GW), lambda bi, r: (bi, 0, 0))
    tq = NA_QROWS * GRID_W
    return pl.pallas_call(
        functools.partial(_na_kernel, rows=rows),
        grid=(b, rows // NA_QROWS),
        in_specs=[pl.BlockSpec((1, tq, GW), lambda bi, r: (bi, r, 0)),
                  full(n), full(n), full(pc), full(pc),
                  pl.BlockSpec((1, NA_H, tq, NA_KROWS * GRID_W),
                               lambda bi, r: ((r * NA_QROWS - _na_key_start(r * NA_QROWS, rows))
                                              // (NA_ROWS // 2), 0, 0, 0))],
        out_specs=pl.BlockSpec((1, tq, GW), lambda bi, r: (bi, r, 0)),
        out_shape=jax.ShapeDtypeStruct((b, n, GW), BF16),
        compiler_params=_cp(("parallel", "parallel")),
        name="attn_neighbourhood",
    )(q, k, v, kc, vc, bias)


POOL_HALO = 64


def _pool_kernel(prev_ref, main_ref, next_ref, band_ref, inv_ref, w_ref, scale_ref, o_ref):
    ext = jnp.concatenate([prev_ref[0], main_ref[0], next_ref[0]], axis=0)
    ext_hi = ext.astype(BF16)
    ext_lo = (ext - ext_hi.astype(F32)).astype(BF16)
    for g in range(POOL_G):
        sl = slice(g * POOL_C, (g + 1) * POOL_C)
        tot = jnp.dot(band_ref[0, g], jnp.concatenate([ext_hi[:, sl], ext_lo[:, sl]], axis=1),
                      preferred_element_type=F32)
        mean = (tot[:, :POOL_C] + tot[:, POOL_C:]) * inv_ref[0, :, sl]
        d = (mean - main_ref[0, :, sl]).astype(BF16)
        y = jnp.dot(d, w_ref[g], preferred_element_type=F32) * scale_ref[:, sl]
        o_ref[0, :, sl] = y.astype(o_ref.dtype)


def _pool_windows(n, tp):
    starts = (0, tp if n > 2 * tp else 0, n - tp, 0)
    seqs = (n if n > tp else 2 * tp, n if n > tp else 3 * tp, n, tp)
    bands = np.zeros((4, POOL_G, tp, tp + 2 * POOL_HALO), np.float32)
    inv = np.zeros((4, tp, GW), np.float32)
    for e, (t0, length) in enumerate(zip(starts, seqs)):
        t = t0 + np.arange(tp)[:, None]
        pos = t0 - POOL_HALO + np.arange(tp + 2 * POOL_HALO)[None, :]
        for g, win in enumerate(POOL_WINDOWS):
            lo = np.clip(t - win // 2, 0, length)
            hi = np.clip(t - win // 2 + win, 0, length)
            bands[e, g] = (pos >= lo) & (pos < hi)
            inv[e, :, g * POOL_C:(g + 1) * POOL_C] = 1.0 / (hi - lo)
    return jnp.asarray(bands, dtype=BF16), jnp.asarray(inv)


def _pool(pu, w_pool, scale, tp=256):
    b, n, _ = pu.shape
    hb = tp // POOL_HALO
    nt = n // tp
    last = n // POOL_HALO - 1
    bands, inv = _pool_windows(n, tp)
    edge = lambda i: jnp.where(i == 0, 0, jnp.where(i == nt - 1, 2, 1)) if nt > 1 else 3
    return pl.pallas_call(
        _pool_kernel,
        grid=(b, nt),
        in_specs=[pl.BlockSpec((1, POOL_HALO, GW), lambda bi, i: (bi, jnp.maximum(i * hb - 1, 0), 0)),
                  pl.BlockSpec((1, tp, GW), lambda bi, i: (bi, i, 0)),
                  pl.BlockSpec((1, POOL_HALO, GW), lambda bi, i: (bi, jnp.minimum((i + 1) * hb, last), 0)),
                  pl.BlockSpec((1,) + bands.shape[1:], lambda bi, i: (edge(i), 0, 0, 0)),
                  pl.BlockSpec((1,) + inv.shape[1:], lambda bi, i: (edge(i), 0, 0)),
                  _const_spec(w_pool.shape), _const_spec(scale.shape)],
        out_specs=pl.BlockSpec((1, tp, GW), lambda bi, i: (bi, i, 0)),
        out_shape=jax.ShapeDtypeStruct((b, n, GW), BF16),
        compiler_params=_cp(("parallel", "parallel")),
        name="pool_mixer",
    )(pu, pu, pu, bands, inv, w_pool, scale)


def _outproj_kernel(oa_ref, ob_ref, oc_ref, od_ref, x_ref, mod_ref, g2_ref, w_ref, o_ref, h_ref):
    mix = jnp.dot(oa_ref[...], w_ref[0:GW, :], preferred_element_type=F32)
    mix += jnp.dot(ob_ref[...], w_ref[GW:2 * GW, :], preferred_element_type=F32)
    mix += jnp.dot(oc_ref[...], w_ref[2 * GW:3 * GW, :], preferred_element_type=F32)
    mix += jnp.dot(od_ref[...], w_ref[3 * GW:4 * GW, :], preferred_element_type=F32)
    mod = mod_ref[0]
    x = x_ref[...] + mod[2:3, :] * mix
    o_ref[...] = x
    h_ref[...] = _modulated(x, g2_ref[...], mod, 3).astype(BF16)


def _outproj(oa, ob, oc, od, x2, mod, mod_row, g2, w_out, tm=256):
    t, d = x2.shape
    tok = lambda w: pl.BlockSpec((tm, w), lambda i: (i, 0))
    return pl.pallas_call(
        _outproj_kernel,
        grid=(t // tm,),
        in_specs=[tok(GW)] * 4 + [tok(d), pl.BlockSpec((1, N_MOD, d), lambda i: (mod_row(i), 0, 0)),
                                   _const_spec((1, d)), _const_spec(w_out.shape)],
        out_specs=[tok(d), tok(d)],
        out_shape=[jax.ShapeDtypeStruct((t, d), F32), jax.ShapeDtypeStruct((t, d), BF16)],
        compiler_params=_cp(("parallel",)),
        name="outproj",
    )(oa, ob, oc, od, x2, mod, g2, w_out)


FFN_HALO = 16


def _ffn_kernel(prev_ref, hm_ref, next_ref, x_ref, mod_ref, wg_ref, wv_ref, cwg_ref, cwv_ref, cbg_ref,
                cbv_ref, wd_ref, o_ref, h_ref, ug_ref, uv_ref, *, seq):
    i = pl.program_id(0)
    j = pl.program_id(1)
    tm = x_ref.shape[0]
    tiles_per_seq = max(seq // tm, 1)

    @pl.when(j == 0)
    def _():
        first = (i % tiles_per_seq) == 0
        last = (i % tiles_per_seq) == tiles_per_seq - 1
        h_ref[0:FFN_HALO, :] = jnp.where(first, jnp.zeros_like(prev_ref), prev_ref[...])
        h_ref[FFN_HALO:FFN_HALO + tm, :] = hm_ref[...]
        h_ref[FFN_HALO + tm:, :] = jnp.where(last, jnp.zeros_like(next_ref), next_ref[...])
        o_ref[...] = jnp.zeros_like(o_ref)

    h = h_ref[...]
    if seq < tm:
        pos = lax.broadcasted_iota(jnp.int32, (tm, 1), 0)
        seq_start = (pos % seq) == 0
        seq_end = (pos % seq) == seq - 1

    def conv(w_ref, cw_ref, cb_ref, u_ref):
        u_ref[...] = jnp.dot(h, w_ref[...], preferred_element_type=F32)
        cw = cw_ref[...]
        before = u_ref[FFN_HALO - 1:FFN_HALO - 1 + tm, :]
        after = u_ref[FFN_HALO + 1:FFN_HALO + 1 + tm, :]
        if seq < tm:
            before = jnp.where(seq_start, 0.0, before)
            after = jnp.where(seq_end, 0.0, after)
        return (before * cw[0:1] + u_ref[FFN_HALO:FFN_HALO + tm, :] * cw[1:2] + after * cw[2:3]
                + cb_ref[...])

    gate = conv(wg_ref, cwg_ref, cbg_ref, ug_ref)
    val = conv(wv_ref, cwv_ref, cbv_ref, uv_ref)
    a = (gate * jax.nn.sigmoid(gate) * val).astype(BF16)
    o_ref[...] += jnp.dot(a, wd_ref[...], preferred_element_type=F32)

    @pl.when(j == pl.num_programs(1) - 1)
    def _():
        o_ref[...] = x_ref[...] + mod_ref[0, 5:6, :] * o_ref[...]


def _ffn(x2, h2, mod, mod_row, w_up, conv_w, conv_b, w_down, layer, seq, tm, tn=512):
    t, d = x2.shape
    dff = w_down.shape[1]
    assert t % tm == 0 and (seq % tm == 0 or tm % seq == 0) and dff // tn > 1
    nj = dff // tn
    hb = tm // FFN_HALO
    nhb = t // FFN_HALO
    return pl.pallas_call(
        functools.partial(_ffn_kernel, seq=seq),
        grid=(t // tm, nj),
        in_specs=[pl.BlockSpec((FFN_HALO, d), lambda i, j: (jnp.maximum(i * hb - 1, 0), 0)),
                  pl.BlockSpec((tm, d), lambda i, j: (i, 0)),
                  pl.BlockSpec((FFN_HALO, d), lambda i, j: (jnp.minimum((i + 1) * hb, nhb - 1), 0)),
                  pl.BlockSpec((tm, d), lambda i, j: (jnp.where(j == 0, jnp.maximum(i - 1, 0), i), 0)),
                  pl.BlockSpec((1, N_MOD, d), lambda i, j: (mod_row(i), 0, 0)),
                  pl.BlockSpec((None, d, tn), lambda i, j: (layer, 0, j)),
                  pl.BlockSpec((None, d, tn), lambda i, j: (layer, 0, nj + j)),
                  pl.BlockSpec((CONV_W, tn), lambda i, j: (0, j)),
                  pl.BlockSpec((CONV_W, tn), lambda i, j: (0, nj + j)),
                  pl.BlockSpec((1, tn), lambda i, j: (0, j)),
                  pl.BlockSpec((1, tn), lambda i, j: (0, nj + j)),
                  pl.BlockSpec((None, tn, d), lambda i, j: (layer, j, 0))],
        out_specs=pl.BlockSpec((tm, d), lambda i, j: (i, 0)),
        out_shape=jax.ShapeDtypeStruct((t, d), F32),
        scratch_shapes=[pltpu.VMEM((tm + 2 * FFN_HALO, d), BF16),
                        pltpu.VMEM((tm + 2 * FFN_HALO, tn), F32), pltpu.VMEM((tm + 2 * FFN_HALO, tn), F32)],
        compiler_params=_cp(("parallel", "arbitrary")),
        name="conv_ffn",
    )(h2, h2, h2, x2, mod, w_up, w_up, conv_w, conv_w, conv_b, conv_b, w_down)


def _block_ones(group):
    idx = np.arange(GW) // group
    return jnp.asarray(idx[:, None] == idx[None, :], dtype=BF16)


def _pad_heads(w, heads, width):
    lead = w.shape[:-1]
    w = w.reshape(lead + (heads, width))
    w = jnp.pad(w, [(0, 0)] * len(lead) + [(0, 0), (0, LANES - width)])
    return w.reshape(lead + (heads * LANES,))


def _layer_weights(P, l):
    d = P["w_in"].shape[1]
    w_in = P["w_in"][l].astype(BF16)
    ml0, ml1 = 3 * GW, 3 * GW + MLA_Q_LORA + MLA_KV_LORA + MLA_ROPE
    w_all = jnp.concatenate([w_in[:, :ml0], w_in[:, ml1:], w_in[:, ml0:ml1],
                             jnp.zeros((d, ML_W - (ml1 - ml0)), BF16)], axis=1)
    wukv = P["mla_w_ukv"][l].reshape(MLA_KV_LORA, MLA_H, MLA_NOPE + MLA_V)
    wuk = _pad_heads(wukv[:, :, :MLA_NOPE].reshape(MLA_KV_LORA, MLA_H * MLA_NOPE), MLA_H, MLA_NOPE)
    wuv = wukv[:, :, MLA_NOPE:].reshape(MLA_KV_LORA, MLA_H * MLA_V)
    tile = lambda g, reps: jnp.tile(g.astype(F32), reps)
    padw = lambda g: jnp.pad(g.astype(F32), (0, GW - g.shape[0]))
    gains = jnp.stack([
        tile(P["diff_qn_g"][l], 8) * (DIFF_DH ** -0.5 * LOG2E),
        tile(P["diff_kn_g"][l], 8),
        tile(P["na_qn_g"][l], 8) * (NA_DH ** -0.5 * LOG2E),
        tile(P["na_kn_g"][l], 8),
        _pad_heads(tile(P["mla_qn_g"][l], MLA_H), MLA_H, MLA_QK) * (MLA_QK ** -0.5 * LOG2E),
        _pad_heads(tile(P["mla_kn_g"][l], MLA_H), MLA_H, MLA_QK),
        padw(P["mla_qa_g"][l]),
        padw(P["mla_kva_g"][l]),
    ])
    return dict(
        w_in=w_all,
        wuq=_pad_heads(P["mla_w_uq"][l], MLA_H, MLA_QK).astype(BF16),
        wuk=wuk.astype(BF16), wuv=wuv.astype(BF16),
        gains=gains, g64=_block_ones(64), g128=_block_ones(LANES),
        norm1=P["norm1_g"][l].reshape(1, d).astype(F32),
        norm2=P["norm2_g"][l].reshape(1, d).astype(F32),
        lam=P["diff_lam"][l].astype(F32),
        subg=P["diff_sub_g"][l].reshape(1, 2 * DIFF_DH).astype(F32),
        pool_w=P["pool_w"][l].astype(BF16),
        pool_scale=P["pool_scale"][l].reshape(1, GW).astype(F32),
        w_out=P["w_out"][l].astype(BF16),
        conv_w=P["conv_w"][l].astype(F32),
        conv_b=P["conv_b"][l].reshape(1, -1).astype(F32),
    )


def _rope_tables(n):
    t = jnp.arange(n)
    rows = (t // GRID_W).astype(F32)
    cols = (t % GRID_W).astype(F32)

    def ang(dim):
        quarter = dim // 4
        inv = ROPE_BASE ** (-jnp.arange(quarter, dtype=F32) / quarter)
        return jnp.concatenate([rows[:, None] * inv, cols[:, None] * inv], axis=-1)

    a = ang(DIFF_DH)
    cos_a, sin_a = jnp.cos(a), jnp.sin(a)
    cd = jnp.tile(cos_a, (1, 4))
    sd = jnp.tile(jnp.concatenate([-sin_a, sin_a], axis=1), (1, 2))
    b = ang(MLA_ROPE)
    cos_b, sin_b = jnp.cos(b), jnp.sin(b)
    ones = jnp.ones((n, MLA_NOPE), F32)
    pad = LANES - MLA_NOPE - MLA_ROPE
    cm = jnp.concatenate([ones, cos_b, cos_b, jnp.ones((n, pad), F32)], axis=1)
    sm = jnp.concatenate([0.0 * ones, -sin_b, sin_b, jnp.zeros((n, pad), F32)], axis=1)
    return jnp.stack([cd, sd, cm, sm])


def _layer(x, mod, mod_row_of_batch, W, layer, cache, rope_tab, bias_tab):
    b, n, d = x.shape
    t = b * n
    x2 = x.reshape(t, d)
    lam_init = 0.8 - 0.6 * math.exp(-0.3 * layer)
    tm = min(512, n)
    row_in = lambda i: mod_row_of_batch(i // (n // tm))
    outs = _inproj(x2, mod, row_in, W["norm1"], W, rope_tab, n, states=cache is None, tm=tm)
    qd, kd, vd, qn, kn, vn, pu, qm, km, vm = [a.reshape(b, n, GW) for a in outs[:10]]
    if cache is None:
        oa = _attention("diff", qd, [(kd, vd)], n, n, W["lam"], W["subg"], lam_init)
        ob = _attention("full", qm, [(km, vm)], n, n)
        oc = _attention("pair", qn, [(kn, vn)], n, n)
        state = outs[10:]
    else:
        a_k, a_v, b_ckv, b_kpe, c_k, c_v = cache
        pc = a_k.shape[1]
        ck = a_k.reshape(b, pc, GW).astype(BF16)
        cv = a_v.reshape(b, pc, GW).astype(BF16)
        oa = _attention_latent("diff", qd, ck, cv, kd, vd, W["gains"][0:1], DIFF_DH, W["lam"], W["subg"],
                               lam_init)
        kpe_pad = jnp.pad(b_kpe.reshape(b * pc, MLA_ROPE), ((0, 0), (0, LANES - MLA_ROPE)))
        kmc, vmc = _mla_cache(b_ckv.reshape(b * pc, MLA_KV_LORA), kpe_pad, W)
        ob = _attention_latent("full", qm, kmc.reshape(b, pc, GW), vmc.reshape(b, pc, GW), km, vm,
                               W["gains"][4:5], MLA_QK)
        nk = c_k.reshape(b, pc, GW).astype(BF16)
        nv = c_v.reshape(b, pc, GW).astype(BF16)
        oc = _na_latent(qn, kn, vn, nk, nv, bias_tab)
        state = ()
    od = _pool(pu, W["pool_w"], W["pool_scale"])
    flat = lambda a: a.reshape(t, GW)
    x2, h2 = _outproj(flat(oa), flat(ob), flat(oc), flat(od), x2, mod, row_in, W["norm2"], W["w_out"], tm=tm)
    tmf = min(512, t) if cache is None else min(512, n)
    row_ffn = lambda i: mod_row_of_batch((i * tmf) // n)
    x2 = _ffn(x2, h2, mod, row_ffn, W["w_up"], W["conv_w"], W["conv_b"], W["w_down"], layer, n, tm=tmf)
    return x2.reshape(b, n, d), state


def kernel(x_prompt, x_sample, cache_diff_k, cache_diff_v, cache_mla_ckv, cache_mla_kpe, cache_na_k, cache_na_v, c, c_ctx, norm1_g, norm2_g, ada_w, ada_b, w_in, diff_qn_g, diff_kn_g, diff_lam, diff_sub_g, mla_qa_g, mla_kva_g, mla_w_uq, mla_w_ukv, mla_qn_g, mla_kn_g, na_qn_g, na_kn_g, na_bias, pool_w, pool_scale, w_out, w_up, conv_w, conv_b, w_down):
    P = dict(norm1_g=norm1_g, norm2_g=norm2_g, w_in=w_in, diff_qn_g=diff_qn_g, diff_kn_g=diff_kn_g,
             diff_lam=diff_lam, diff_sub_g=diff_sub_g, mla_qa_g=mla_qa_g, mla_kva_g=mla_kva_g,
             mla_w_uq=mla_w_uq, mla_w_ukv=mla_w_ukv, mla_qn_g=mla_qn_g, mla_kn_g=mla_kn_g,
             na_qn_g=na_qn_g, na_kn_g=na_kn_g, pool_w=pool_w, pool_scale=pool_scale, w_out=w_out,
             w_up=w_up, conv_w=conv_w, conv_b=conv_b, w_down=w_down)
    depth = w_in.shape[0]
    d = x_prompt.shape[-1]
    bd, nd = x_sample.shape[:2]
    n_rows = -(-(bd + 1) // 8) * 8
    cv = jnp.concatenate([c, c_ctx[None, :], jnp.zeros((n_rows - bd - 1, d), F32)], axis=0)
    mod_all = _ada(cv, ada_w, ada_b).reshape(depth, n_rows, N_MOD, d)
    rope_tab = _rope_tables(nd)
    Ws = [_layer_weights(P, l) for l in range(depth)]
    w_up_all, w_down_all = w_up.astype(BF16), w_down.astype(BF16)
    for W in Ws:
        W.update(w_up=w_up_all, w_down=w_down_all)

    xp = x_prompt
    states = []
    for l in range(depth):
        xp, st = _layer(xp, mod_all[l], lambda bi: bd, Ws[l], l, None, None, None)
        states.append(st)
    xs = x_sample
    for l in range(depth):
        cache_l = (cache_diff_k[:, l], cache_diff_v[:, l], cache_mla_ckv[:, l], cache_mla_kpe[:, l],
                   cache_na_k[:, l], cache_na_v[:, l])
        bias_tab = _na_bias_table(na_bias[l], nd // GRID_W)
        xs, _ = _layer(xs, mod_all[l], lambda bi: bi, Ws[l], l, cache_l, rope_tab, bias_tab)

    bp, npr = x_prompt.shape[:2]

    def stack(k, shape):
        return jnp.stack([s[k].reshape((bp, npr) + shape) for s in states], axis=1)

    return (xp, xs,
            stack(0, (DIFF_H, 2, DIFF_DH)), stack(1, (DIFF_H, 2 * DIFF_DH)),
            stack(2, (MLA_KV_LORA,)), stack(3, (MLA_ROPE,)),
            stack(4, (NA_H, NA_DH)), stack(5, (NA_H, NA_DH)))
```
